```python
import jax, jax.numpy as jnp
from jax import lax
import numpy as np

D_MODEL = 2048
BATCH = 4
SEQ = 2048
DEPTH = 1
DEC_BATCH = 8
DEC_SEQ = 16
PAST_LEN = 1024

CHUNK = 64
A_GROUPS = 8
A_DG = 128
A_WIDTH = A_GROUPS * A_DG
A_CHUNK = 128
SB_HEADS = 8
SB_DH = 128
SB_WIDTH = SB_HEADS * SB_DH
SB_BLOCK = 128
MIX_WIDTH = A_WIDTH + SB_WIDTH
IN_WIDTH = 2 * A_WIDTH + 3 * SB_WIDTH
N_MEM = 256
CA_HEADS = 4
CA_DH = 128
CA_WIDTH = CA_HEADS * CA_DH
N_EXPERTS = 32
TOP_K = 4
D_EXPERT = D_MODEL
SWIGLU_LIMIT = 7.0
SWIGLU_ALPHA = 1.702
MOE_BLOCK = 128
RMS_EPS = 1e-6

kernel_name = 'hybrid_chunkmlp_stickbreak_memxattn_moe_stream_step'


def _rmsnorm(x, g):
    xf = x.astype(jnp.float32)
    y = xf * lax.rsqrt(jnp.mean(xf * xf, axis=-1, keepdims=True) + RMS_EPS)
    return (y * g.astype(jnp.float32)).astype(x.dtype)


def _split_proj(h, w_in):
    bsz, t_len = h.shape[:2]
    p = h @ w_in
    u, va, q, k, vb = jnp.split(p, [A_WIDTH, 2 * A_WIDTH, 2 * A_WIDTH + SB_WIDTH, 2 * A_WIDTH + 2 * SB_WIDTH], axis=-1)
    ga = lambda t: t.reshape(bsz, t_len, A_GROUPS, A_DG)
    gb = lambda t: t.reshape(bsz, t_len, SB_HEADS, SB_DH)
    return ga(u), ga(va), gb(q), gb(k), gb(vb)


def _chunk_mlp(u, v, w_sp, b_sp):
    bsz, t_len = v.shape[:2]
    blk = min(t_len, A_CHUNK)
    n_chunks = t_len // blk
    tri = jnp.tril(jnp.ones((blk, blk), v.dtype))
    w = w_sp[:, :blk, :blk] * tri
    vc = v.reshape(bsz, n_chunks, blk, A_GROUPS, A_DG)
    mixed = jnp.einsum('gts,bnsgd->bntgd', w, vc) + b_sp[:, :blk].T[None, None, :, :, None]
    return u * mixed.reshape(u.shape)


def _sb_attend(q, k, v, q_pos, k_pos):
    z = jnp.einsum('bqhd,bkhd->bhqk', q, k).astype(jnp.float32) * (SB_DH ** -0.5)
    causal = k_pos[None, :] < q_pos[:, None]
    log_1m = jnp.where(causal, jax.nn.log_sigmoid(-z), 0.0)
    rest = lax.cumsum(log_1m, axis=3, reverse=True) - log_1m
    a = jnp.where(causal, jnp.exp(jax.nn.log_sigmoid(z) + rest), 0.0)
    return jnp.einsum('bhqk,bkhd->bqhd', a.astype(v.dtype), v)


def _sb_prompt(q, k, v):
    t_len = q.shape[1]
    outs = []
    for i in range(t_len // SB_BLOCK):
        lo, hi = i * SB_BLOCK, (i + 1) * SB_BLOCK
        outs.append(_sb_attend(q[:, lo:hi], k[:, :hi], v[:, :hi], jnp.arange(lo, hi), jnp.arange(hi)))
    return jnp.concatenate(outs, axis=1)


def _mem_kv(mem, g_mem, w_ck, w_cv):
    bsz, n = mem.shape[:2]
    hm = _rmsnorm(mem, g_mem)
    return ((hm @ w_ck).reshape(bsz, n, CA_HEADS, CA_DH), (hm @ w_cv).reshape(bsz, n, CA_HEADS, CA_DH))


def _cross_attn(h, mem_k, mem_v, w_cq, w_co):
    bsz, t_len = h.shape[:2]
    q = (h @ w_cq).reshape(bsz, t_len, CA_HEADS, CA_DH)
    s = jnp.einsum('bqhd,bkhd->bhqk', q, mem_k).astype(jnp.float32) * (CA_DH ** -0.5)
    p = jax.nn.softmax(s, axis=-1).astype(mem_v.dtype)
    o = jnp.einsum('bhqk,bkhd->bqhd', p, mem_v).reshape(bsz, t_len, CA_WIDTH)
    return o @ w_co


def _moe_block_rows(n_assign):
    per_expert = max(1, n_assign // N_EXPERTS)
    return min(MOE_BLOCK, max(8, 1 << (per_expert - 1).bit_length()))


def _moe(h, w_router, b_router, w_gu, b_gu, w_dn, b_dn):
    lead = h.shape[:-1]
    t = h.reshape(-1, D_MODEL)
    n_tok = t.shape[0]
    logits = (t @ w_router).astype(jnp.float32) + b_router.astype(jnp.float32)
    top_val, top_idx = lax.top_k(logits, TOP_K)
    gate = jax.nn.softmax(top_val, axis=-1).reshape(-1)
    flat_e = top_idx.reshape(-1).astype(jnp.int32)
    n_assign = n_tok * TOP_K
    blk = _moe_block_rows(n_assign)
    n_blocks = -(-(n_assign + N_EXPERTS * (blk - 1)) // blk)
    order = jnp.argsort(flat_e).astype(jnp.int32)
    e_sorted = flat_e[order]
    counts = jnp.zeros((N_EXPERTS,), jnp.int32).at[flat_e].add(1)
    starts = jnp.cumsum(counts) - counts
    padded = (counts + blk - 1) // blk * blk
    pad_end = jnp.cumsum(padded)
    dest = pad_end[e_sorted] - padded[e_sorted] + jnp.arange(n_assign, dtype=jnp.int32) - starts[e_sorted]
    row_assign = jnp.full((n_blocks * blk,), n_assign, jnp.int32).at[dest].set(order)
    block_e = jnp.minimum(jnp.searchsorted(pad_end, jnp.arange(n_blocks, dtype=jnp.int32) * blk, side='right'), N_EXPERTS - 1)
    t_pad = jnp.concatenate([t, jnp.zeros((1, D_MODEL), t.dtype)], axis=0)

    def expert_block(args):
        rows, e = args
        xb = t_pad[rows // TOP_K]
        gu = xb @ w_gu[e] + b_gu[e]
        glu = jnp.minimum(gu[:, :D_EXPERT], SWIGLU_LIMIT)
        lin = jnp.clip(gu[:, D_EXPERT:], -SWIGLU_LIMIT, SWIGLU_LIMIT)
        act = glu * jax.nn.sigmoid(SWIGLU_ALPHA * glu) * (lin + 1.0)
        return act @ w_dn[e] + b_dn[e]

    y_rows = lax.map(expert_block, (row_assign.reshape(n_blocks, blk), block_e)).reshape(-1, D_MODEL)
    gate_rows = jnp.concatenate([gate, jnp.zeros((1,), gate.dtype)])[row_assign]
    out = jax.ops.segment_sum(y_rows.astype(jnp.float32) * gate_rows[:, None], row_assign // TOP_K, num_segments=n_tok + 1)[:n_tok]
    return out.astype(h.dtype).reshape(lead + (D_MODEL,))


def _layer(x, sb_fn, mem_k, mem_v, g_mix, w_in, w_sp, b_sp, g_a, g_b, w_out, g_cross, w_cq, w_co,
           g_moe, w_router, b_router, w_gu, b_gu, w_dn, b_dn):
    bsz, t_len = x.shape[:2]
    h = _rmsnorm(x, g_mix)
    u, va, q, k, vb = _split_proj(h, w_in)
    a_out = _chunk_mlp(u, va, w_sp, b_sp).reshape(bsz, t_len, A_WIDTH)
    b_out = sb_fn(q, k, vb).reshape(bsz, t_len, SB_WIDTH)
    mixed = jnp.concatenate([_rmsnorm(a_out, g_a), _rmsnorm(b_out, g_b)], axis=-1)
    x = x + mixed @ w_out
    x = x + _cross_attn(_rmsnorm(x, g_cross), mem_k, mem_v, w_cq, w_co)
    x = x + _moe(_rmsnorm(x, g_moe), w_router, b_router, w_gu, b_gu, w_dn, b_dn)
    return x, k, vb, va


def setup_inputs(seed: int = 0) -> dict:
    key = jax.random.key(seed)
    ks = jax.random.split(key, 28)
    f32 = jnp.float32
    nrm = lambda k, shape, scale: jax.random.normal(k, shape, f32) * scale
    gain = lambda k, shape: 1.0 + 0.05 * jax.random.normal(k, shape, f32)
    return {
        'x_prompt': nrm(ks[0], (BATCH, SEQ, D_MODEL), 1.0),
        'x_sample': nrm(ks[1], (DEC_BATCH, DEC_SEQ, D_MODEL), 1.0),
        'cache_sb_k': nrm(ks[2], (DEPTH, DEC_BATCH, PAST_LEN, SB_HEADS, SB_DH), 1.0),
        'cache_sb_v': nrm(ks[3], (DEPTH, DEC_BATCH, PAST_LEN, SB_HEADS, SB_DH), 1.0),
        'cache_mem_k': nrm(ks[4], (DEPTH, DEC_BATCH, N_MEM, CA_HEADS, CA_DH), 1.0),
        'cache_mem_v': nrm(ks[5], (DEPTH, DEC_BATCH, N_MEM, CA_HEADS, CA_DH), 1.0),
        'mem_prompt': nrm(ks[6], (BATCH, N_MEM, D_MODEL), 1.0),
        'g_mix': gain(ks[7], (DEPTH, D_MODEL)),
        'w_in': nrm(ks[8], (DEPTH, D_MODEL, IN_WIDTH), D_MODEL ** -0.5),
        'w_sp': nrm(ks[9], (DEPTH, A_GROUPS, A_CHUNK, A_CHUNK), A_CHUNK ** -0.5),
        'b_sp': 1.0 + 0.1 * jax.random.normal(ks[10], (DEPTH, A_GROUPS, A_CHUNK), f32),
        'g_a': gain(ks[11], (DEPTH, A_WIDTH)),
        'g_b': gain(ks[12], (DEPTH, SB_WIDTH)),
        'w_out': nrm(ks[13], (DEPTH, MIX_WIDTH, D_MODEL), MIX_WIDTH ** -0.5),
        'g_cross': gain(ks[14], (DEPTH, D_MODEL)),
        'g_mem': gain(ks[15], (DEPTH, D_MODEL)),
        'w_cq': nrm(ks[16], (DEPTH, D_MODEL, CA_WIDTH), D_MODEL ** -0.5),
        'w_ck': nrm(ks[17], (DEPTH, D_MODEL, CA_WIDTH), D_MODEL ** -0.5),
        'w_cv': nrm(ks[18], (DEPTH, D_MODEL, CA_WIDTH), D_MODEL ** -0.5),
        'w_co': nrm(ks[19], (DEPTH, CA_WIDTH, D_MODEL), CA_WIDTH ** -0.5),
        'g_moe': gain(ks[20], (DEPTH, D_MODEL)),
        'w_router': nrm(ks[21], (DEPTH, D_MODEL, N_EXPERTS), D_MODEL ** -0.5),
        'b_router': nrm(ks[22], (DEPTH, N_EXPERTS), 0.01),
        'w_gu': nrm(ks[23], (DEPTH, N_EXPERTS, D_MODEL, 2 * D_EXPERT), D_MODEL ** -0.5),
        'b_gu': nrm(ks[24], (DEPTH, N_EXPERTS, 2 * D_EXPERT), 0.02),
        'w_dn': nrm(ks[25], (DEPTH, N_EXPERTS, D_EXPERT, D_MODEL), D_EXPERT ** -0.5),
        'b_dn': nrm(ks[26], (DEPTH, N_EXPERTS, D_MODEL), 0.02),
        'g_final': gain(ks[27], (D_MODEL,)),
    }


def reference(x_prompt, x_sample, cache_sb_k, cache_sb_v, cache_mem_k, cache_mem_v, mem_prompt,
              g_mix, w_in, w_sp, b_sp, g_a, g_b, w_out, g_cross, g_mem, w_cq, w_ck, w_cv, w_co,
              g_moe, w_router, b_router, w_gu, b_gu, w_dn, b_dn, g_final):
    past = cache_sb_k.shape[2]
    n_new = x_sample.shape[1]
    xp, xs = x_prompt, x_sample
    kp_l, vp_l, mkp_l, mvp_l, ks_l, vs_l, avs_l = [], [], [], [], [], [], []
    for l in range(DEPTH):
        lw = (g_mix[l], w_in[l], w_sp[l], b_sp[l], g_a[l], g_b[l], w_out[l], g_cross[l], w_cq[l], w_co[l],
              g_moe[l], w_router[l], b_router[l], w_gu[l], b_gu[l], w_dn[l], b_dn[l])
        mk, mv = _mem_kv(mem_prompt, g_mem[l], w_ck[l], w_cv[l])
        xp, k_p, v_p, _ = _layer(xp, _sb_prompt, mk, mv, *lw)
        kp_l.append(k_p)
        vp_l.append(v_p)
        mkp_l.append(mk)
        mvp_l.append(mv)

        def sb_sample(q, k, v, ck=cache_sb_k[l], cv=cache_sb_v[l]):
            k_all = jnp.concatenate([ck, k], axis=1)
            v_all = jnp.concatenate([cv, v], axis=1)
            return _sb_attend(q, k_all, v_all, past + jnp.arange(n_new), jnp.arange(past + n_new))

        xs, k_s, v_s, va_s = _layer(xs, sb_sample, cache_mem_k[l], cache_mem_v[l], *lw)
        ks_l.append(k_s)
        vs_l.append(v_s)
        avs_l.append(va_s)

    y_prompt = _rmsnorm(xp, g_final)
    y_sample = _rmsnorm(xs, g_final)
    new_sb_k_prompt = jnp.stack(kp_l, axis=0)
    new_sb_v_prompt = jnp.stack(vp_l, axis=0)
    new_mem_k_prompt = jnp.stack(mkp_l, axis=0)
    new_mem_v_prompt = jnp.stack(mvp_l, axis=0)
    new_sb_k_sample = jnp.stack(ks_l, axis=0)
    new_sb_v_sample = jnp.stack(vs_l, axis=0)
    new_amlp_v_sample = jnp.stack(avs_l, axis=0)
    return (y_prompt, y_sample, new_sb_k_prompt, new_sb_v_prompt, new_mem_k_prompt, new_mem_v_prompt,
            new_sb_k_sample, new_sb_v_sample, new_amlp_v_sample)
```

```python
import functools

import jax
import jax.numpy as jnp
from jax import lax
from jax.experimental import pallas as pl
from jax.experimental.pallas import tpu as pltpu

F32 = jnp.float32
BF16 = jnp.bfloat16

D_MODEL = 2048
A_GROUPS = 8
A_DG = 128
A_WIDTH = A_GROUPS * A_DG
A_CHUNK = 128
SB_HEADS = 8
SB_DH = 128
SB_WIDTH = SB_HEADS * SB_DH
IN_WIDTH = 2 * A_WIDTH + 3 * SB_WIDTH
N_SEG = IN_WIDTH // 1024
CA_HEADS = 4
CA_DH = 128
CA_WIDTH = CA_HEADS * CA_DH
N_EXPERTS = 32
TOP_K = 4
D_EXPERT = D_MODEL
SWIGLU_LIMIT = 7.0
SWIGLU_ALPHA = 1.702
RMS_EPS = 1e-6

LANES = 128
SB_TILE = 256
VMEM_LIMIT = 56 * 1024 * 1024

MOE_SUB = 256
MOE_ITEM_ROWS = 2048
MOE_CHUNK = 256
MOE_NCH = D_EXPERT // MOE_CHUNK
GATHER_TILE = 128
COMBINE_TILE = 128


def _rms(x, g):
    return x * lax.rsqrt(jnp.mean(x * x, axis=-1, keepdims=True) + RMS_EPS) * g


def _dot(a, b):
    return jnp.dot(a, b, preferred_element_type=F32)


def _dot_nt(a, b):
    return lax.dot_general(a, b, (((1,), (1,)), ((), ())), preferred_element_type=F32)


def _split_bf16(x):
    hi = x.astype(BF16)
    lo = (x - hi.astype(F32)).astype(BF16)
    return hi, lo


def _params(n_axes):
    return pltpu.CompilerParams(dimension_semantics=("arbitrary",) * n_axes,
                                vmem_limit_bytes=VMEM_LIMIT)


def _const_spec(shape):
    return pl.BlockSpec(shape, lambda *_: (0,) * len(shape), pipeline_mode=pl.Buffered(1))


def _proj_kernel(x_ref, g_ref, w_ref, *out_refs):
    h = _rms(x_ref[...], g_ref[...]).astype(BF16)
    for s, o_ref in enumerate(out_refs):
        o_ref[...] = _dot(h, w_ref[:, s * 1024:(s + 1) * 1024])


def _proj(x, g, w, tm):
    n = x.shape[0]
    seg = jax.ShapeDtypeStruct((n, 1024), F32)
    return pl.pallas_call(
        _proj_kernel,
        grid=(n // tm,),
        in_specs=[pl.BlockSpec((tm, D_MODEL), lambda i: (i, 0)),
                  _const_spec((1, D_MODEL)),
                  _const_spec((D_MODEL, IN_WIDTH))],
        out_specs=[pl.BlockSpec((tm, 1024), lambda i: (i, 0))] * N_SEG,
        out_shape=[seg] * N_SEG,
        compiler_params=_params(1),
        name="proj",
    )(x, g, w)


def _log_sigmoid_neg(z):
    return -(jnp.maximum(z, 0.0) + jnp.log1p(jnp.exp(-jnp.abs(z))))


def _sb_block(qb, kb, vb, carry, later, mask):
    z = _dot_nt(qb, kb) * (SB_DH ** -0.5)
    l = _log_sigmoid_neg(z)
    if mask is not None:
        l = jnp.where(mask, l, 0.0)
    hi, lo = _split_bf16(l)
    cs = _dot(hi, later) + _dot(lo, later)
    a = jnp.exp(l + z + cs + carry)
    if mask is not None:
        a = jnp.where(mask, a, 0.0)
    contrib = _dot(a.astype(BF16), vb)
    return contrib, carry + cs[:, 0:1] + l[:, 0:1]


def _later_matrix(n):
    r = lax.broadcasted_iota(jnp.int32, (n, n), 0)
    c = lax.broadcasted_iota(jnp.int32, (n, n), 1)
    return (r > c).astype(BF16)


def _sb_prompt_kernel(q_ref, k_ref, v_ref, o_ref):
    qi = pl.program_id(2)
    t = SB_TILE
    qb = q_ref[...].astype(BF16)
    later = _later_matrix(t)
    r = lax.broadcasted_iota(jnp.int32, (t, t), 0)
    c = lax.broadcasted_iota(jnp.int32, (t, t), 1)

    def kv(j):
        off = pl.multiple_of(j * t, t)
        return k_ref[pl.ds(off, t), :].astype(BF16), v_ref[pl.ds(off, t), :].astype(BF16)

    kb, vb = kv(qi)
    acc, carry = _sb_block(qb, kb, vb, jnp.zeros((t, 1), F32), later, c < r)

    def body(step, state):
        acc, carry = state
        kb, vb = kv(qi - 1 - step)
        contrib, carry = _sb_block(qb, kb, vb, carry, later, None)
        return acc + contrib, carry

    acc, _ = lax.fori_loop(0, qi, body, (acc, carry))
    o_ref[...] = acc


def _sb_prompt(q, k, v):
    bsz, t_len, _ = q.shape
    t = SB_TILE
    kv_spec = pl.BlockSpec((None, t_len, SB_DH), lambda b, h, i: (b, 0, h))
    return pl.pallas_call(
        _sb_prompt_kernel,
        grid=(bsz, SB_HEADS, t_len // t),
        in_specs=[pl.BlockSpec((None, t, SB_DH), lambda b, h, i: (b, i, h)), kv_spec, kv_spec],
        out_specs=pl.BlockSpec((None, t, SB_DH), lambda b, h, i: (b, i, h)),
        out_shape=jax.ShapeDtypeStruct(q.shape, F32),
        compiler_params=_params(3),
        name="sb_prompt",
    )(q, k, v)


def _sb_sample_kernel(q_ref, kn_ref, vn_ref, ck_ref, cv_ref, o_ref, *, n_cache_blocks):
    t = SB_TILE
    n_new = q_ref.shape[0]
    qb = q_ref[...].astype(BF16)
    later = _later_matrix(t)
    r = lax.broadcasted_iota(jnp.int32, (n_new, t), 0)
    c = lax.broadcasted_iota(jnp.int32, (n_new, t), 1)
    acc, carry = _sb_block(qb, kn_ref[...].astype(BF16), vn_ref[...].astype(BF16),
                           jnp.zeros((n_new, 1), F32), later, c < r)
    for j in reversed(range(n_cache_blocks)):
        kb = ck_ref[j * t:(j + 1) * t, :].astype(BF16)
        vb = cv_ref[j * t:(j + 1) * t, :].astype(BF16)
        contrib, carry = _sb_block(qb, kb, vb, carry, later, None)
        acc = acc + contrib
    o_ref[...] = acc


def _sb_sample(q, k_new, v_new, cache_k, cache_v):
    bsz, n_new, _ = q.shape
    past = cache_k.shape[1]
    t = SB_TILE
    assert n_new <= t and past % t == 0
    pad = ((0, 0), (0, t - n_new), (0, 0))
    kn = jnp.pad(k_new, pad)
    vn = jnp.pad(v_new, pad)
    head = lambda rows: pl.BlockSpec((None, rows, SB_DH), lambda b, h: (b, 0, h))
    return pl.pallas_call(
        functools.partial(_sb_sample_kernel, n_cache_blocks=past // t),
        grid=(bsz, SB_HEADS),
        in_specs=[head(n_new), head(t), head(t), head(past), head(past)],
        out_specs=head(n_new),
        out_shape=jax.ShapeDtypeStruct(q.shape, F32),
        compiler_params=_params(2),
        name="sb_sample",
    )(q, kn, vn, cache_k, cache_v)


def _memkv_kernel(m_ref, g_ref, wk_ref, wv_ref, k_ref, v_ref):
    h = _rms(m_ref[...], g_ref[...]).astype(BF16)
    k_ref[...] = _dot(h, wk_ref[...])
    v_ref[...] = _dot(h, wv_ref[...])


def _memkv(mem, g, wk, wv):
    n = mem.shape[0]
    tm = 256
    out = jax.ShapeDtypeStruct((n, CA_WIDTH), F32)
    return pl.pallas_call(
        _memkv_kernel,
        grid=(n // tm,),
        in_specs=[pl.BlockSpec((tm, D_MODEL), lambda i: (i, 0)),
                  _const_spec((1, D_MODEL)),
                  _const_spec((D_MODEL, CA_WIDTH)),
                  _const_spec((D_MODEL, CA_WIDTH))],
        out_specs=[pl.BlockSpec((tm, CA_WIDTH), lambda i: (i, 0))] * 2,
        out_shape=[out, out],
        compiler_params=_params(1),
        name="memkv",
    )(mem, g, wk, wv)


def _post_kernel(x_ref, u_ref, va_ref, b_ref, wsp_ref, bsp_ref, ga_ref, gb_ref, wout_ref,
                 gc_ref, wcq_ref, mk_ref, mv_ref, wco_ref, gm_ref, wr_ref, br_ref,
                 x2_ref, h2_ref, idx_ref, gate_ref):
    tm = x_ref.shape[0]
    blk = wsp_ref.shape[1]

    r = lax.broadcasted_iota(jnp.int32, (blk, blk), 0)
    c = lax.broadcasted_iota(jnp.int32, (blk, blk), 1)
    tri = (c <= r).astype(F32)
    cols = []
    for g in range(A_GROUPS):
        w = (wsp_ref[g] * tri).astype(BF16)
        bias = bsp_ref[:, g:g + 1]
        sl = slice(g * A_DG, (g + 1) * A_DG)
        rows = []
        for ch in range(tm // blk):
            rs = slice(ch * blk, (ch + 1) * blk)
            mixed = _dot(w, va_ref[rs, sl].astype(BF16)) + bias
            rows.append(u_ref[rs, sl] * mixed)
        cols.append(rows[0] if len(rows) == 1 else jnp.concatenate(rows, axis=0))
    a_out = jnp.concatenate(cols, axis=1)

    a_n = _rms(a_out, ga_ref[...]).astype(BF16)
    b_n = _rms(b_ref[...], gb_ref[...]).astype(BF16)
    x1 = x_ref[...] + _dot(a_n, wout_ref[0:A_WIDTH, :]) + _dot(b_n, wout_ref[A_WIDTH:, :])

    hc = _rms(x1, gc_ref[...]).astype(BF16)
    q = _dot(hc, wcq_ref[...])
    heads = []
    for h in range(CA_HEADS):
        sl = slice(h * CA_DH, (h + 1) * CA_DH)
        s = _dot_nt(q[:, sl].astype(BF16), mk_ref[:, sl].astype(BF16)) * (CA_DH ** -0.5)
        p = jnp.exp(s - jnp.max(s, axis=-1, keepdims=True))
        p = p / jnp.sum(p, axis=-1, keepdims=True)
        heads.append(_dot(p.astype(BF16), mv_ref[:, sl].astype(BF16)))
    o = jnp.concatenate(heads, axis=1).astype(BF16)
    x2 = x1 + _dot(o, wco_ref[...])
    x2_ref[...] = x2

    h2 = _rms(x2, gm_ref[...])
    h2_ref[...] = h2
    hh, hl = _split_bf16(h2)
    wh, wl = _split_bf16(wr_ref[...])
    logits = _dot(hh, wh) + _dot(hh, wl) + _dot(hl, wh) + br_ref[...]
    lane = lax.broadcasted_iota(jnp.int32, (tm, LANES), 1).astype(F32)
    cur = jnp.where(lane < N_EXPERTS, logits, -jnp.inf)
    vals, idxs = [], []
    for _ in range(TOP_K):
        m = jnp.max(cur, axis=-1, keepdims=True)
        i = jnp.min(jnp.where(cur == m, lane, float(LANES)), axis=-1, keepdims=True)
        vals.append(m)
        idxs.append(i)
        cur = jnp.where(lane == i, -jnp.inf, cur)
    es = [jnp.exp(v - vals[0]) for v in vals]
    denom = es[0] + es[1] + es[2] + es[3]
    idx_out = jnp.zeros((tm, LANES), F32)
    gate_out = jnp.zeros((tm, LANES), F32)
    for k in range(TOP_K):
        idx_out = jnp.where(lane == k, idxs[k], idx_out)
        gate_out = jnp.where(lane == k, es[k] / denom, gate_out)
    idx_ref[...] = idx_out.astype(jnp.int32)
    gate_ref[...] = gate_out


def _post(x, u, va, b_out, mk, mv, lw, tm, rows_per_batch, blk):
    n = x.shape[0]
    tiles_per_batch = rows_per_batch // tm
    n_mem = mk.shape[0] // (n // rows_per_batch)
    row = lambda w: pl.BlockSpec((tm, w), lambda i: (i, 0))
    mem = pl.BlockSpec((n_mem, CA_WIDTH), lambda i: (i // tiles_per_batch, 0))
    wsp = lw["w_sp"][:, :blk, :blk]
    bsp = lw["b_sp"][:, :blk].T
    return pl.pallas_call(
        _post_kernel,
        grid=(n // tm,),
        in_specs=[row(D_MODEL), row(A_WIDTH), row(A_WIDTH), row(SB_WIDTH),
                  _const_spec((A_GROUPS, blk, blk)), _const_spec((blk, A_GROUPS)),
                  _const_spec((1, A_WIDTH)), _const_spec((1, SB_WIDTH)),
                  _const_spec((2 * A_WIDTH, D_MODEL)),
                  _const_spec((1, D_MODEL)), _const_spec((D_MODEL, CA_WIDTH)),
                  mem, mem, _const_spec((CA_WIDTH, D_MODEL)),
                  _const_spec((1, D_MODEL)), _const_spec((D_MODEL, LANES)), _const_spec((1, LANES))],
        out_specs=[row(D_MODEL), row(D_MODEL), row(LANES), row(LANES)],
        out_shape=[jax.ShapeDtypeStruct((n, D_MODEL), F32), jax.ShapeDtypeStruct((n, D_MODEL), F32),
                   jax.ShapeDtypeStruct((n, LANES), jnp.int32), jax.ShapeDtypeStruct((n, LANES), F32)],
        compiler_params=_params(1),
        name="post",
    )(x, u, va, b_out, wsp, bsp, lw["g_a"], lw["g_b"], lw["w_out"], lw["g_cross"], lw["w_cq"],
      mk, mv, lw["w_co"], lw["g_moe"], lw["w_router"], lw["b_router"])


def _gather_kernel(tok_ref, hp_hbm, hs_hbm, xs_ref, buf, sem, *, n_prompt):
    tg = buf.shape[0]

    def row_copy(src, t, r):
        return pltpu.make_async_copy(src.at[pl.ds(t, 1), :], buf.at[pl.ds(r, 1), :], sem)

    def issue(r, carry):
        t = tok_ref[0, 0, r]

        @pl.when(t < n_prompt)
        def _():
            row_copy(hp_hbm, t, r).start()

        @pl.when(t >= n_prompt)
        def _():
            row_copy(hs_hbm, t - n_prompt, r).start()

        return carry

    lax.fori_loop(0, tg, issue, 0)

    def drain(r, carry):
        row_copy(hp_hbm, 0, r).wait()
        return carry

    lax.fori_loop(0, tg, drain, 0)
    xs_ref[...] = buf[...].astype(BF16)


def _gather(row_token, h_prompt, h_sample):
    tg = GATHER_TILE
    p_rows = row_token.shape[0]
    tok = row_token.reshape(p_rows // tg, 1, tg)
    return pl.pallas_call(
        functools.partial(_gather_kernel, n_prompt=h_prompt.shape[0]),
        grid=(p_rows // tg,),
        in_specs=[pl.BlockSpec((1, 1, tg), lambda i: (i, 0, 0), memory_space=pltpu.SMEM),
                  pl.BlockSpec(memory_space=pl.ANY), pl.BlockSpec(memory_space=pl.ANY)],
        out_specs=pl.BlockSpec((tg, D_MODEL), lambda i: (i, 0)),
        out_shape=jax.ShapeDtypeStruct((p_rows, D_MODEL), BF16),
        scratch_shapes=[pltpu.VMEM((tg, D_MODEL), F32), pltpu.SemaphoreType.DMA],
        compiler_params=_params(1),
        name="moe_gather",
    )(tok, h_prompt, h_sample)


def _expert_kernel(item_e, item_start, item_nsub, n_items,
                   xs_hbm, wg_ref, wl_ref, wd_ref, bg_ref, bl_ref, bd_ref,
                   y_hbm, x_buf, y_buf, wg_bf, wl_bf, wd_bf, sem_in, sem_out):
    s = pl.program_id(0)
    j = pl.program_id(1)
    nsub = item_nsub[s]
    start = item_start[s]
    sub = MOE_SUB

    def rows(t):
        return pl.ds(pl.multiple_of(t * sub, sub), sub)

    def copy_in(t):
        src = xs_hbm.at[pl.ds(pl.multiple_of(start + t * sub, sub), sub), :]
        return pltpu.make_async_copy(src, x_buf.at[rows(t), :], sem_in)

    def copy_out(t):
        dst = y_hbm.at[pl.ds(pl.multiple_of(start + t * sub, sub), sub), :]
        return pltpu.make_async_copy(y_buf.at[rows(t), :], dst, sem_out)

    def for_subs(fn):
        def body(t, carry):
            fn(t)
            return carry
        lax.fori_loop(0, nsub, body, 0)

    @pl.when(jnp.logical_and(j == 0, nsub > 0))
    def _load():
        for_subs(lambda t: copy_in(t).start())
        bias = jnp.broadcast_to(bd_ref[...], (sub, D_MODEL))

        def init(t):
            y_buf[rows(t), :] = bias
        for_subs(init)
        for_subs(lambda t: copy_in(t).wait())

    @pl.when(nsub > 0)
    def _compute():
        wg_bf[...] = wg_ref[...].astype(BF16)
        wl_bf[...] = wl_ref[...].astype(BF16)
        wd_bf[...] = wd_ref[...].astype(BF16)

        def ffn(t):
            xb = x_buf[rows(t), :]
            glu = jnp.minimum(_dot(xb, wg_bf[...]) + bg_ref[...], SWIGLU_LIMIT)
            lin = jnp.clip(_dot(xb, wl_bf[...]) + bl_ref[...], -SWIGLU_LIMIT, SWIGLU_LIMIT)
            act = glu * jax.nn.sigmoid(SWIGLU_ALPHA * glu) * (lin + 1.0)
            y_buf[rows(t), :] += _dot(act.astype(BF16), wd_bf[...])
        for_subs(ffn)

    @pl.when(jnp.logical_and(j == MOE_NCH - 1, nsub > 0))
    def _store():
        for_subs(lambda t: copy_out(t).start())
        for_subs(lambda t: copy_out(t).wait())

    @pl.when(jnp.logical_and(s == pl.num_programs(0) - 1, j == MOE_NCH - 1))
    def _fill_tail():
        y_buf[pl.ds(0, sub), :] = jnp.zeros((sub, D_MODEL), F32)

        def tail_copy(t):
            dst = y_hbm.at[pl.ds(pl.multiple_of(t * sub, sub), sub), :]
            return pltpu.make_async_copy(y_buf.at[pl.ds(0, sub), :], dst, sem_out)

        def tail(fn):
            def body(t, carry):
                fn(t)
                return carry
            lax.fori_loop(n_items[1], y_hbm.shape[0] // sub, body, 0)

        tail(lambda t: tail_copy(t).start())
        tail(lambda t: tail_copy(t).wait())


def _experts(item_e, item_start, item_nsub, n_items, xs, w_gu, b_gu, w_dn, b_dn):
    n_slots = item_e.shape[0]
    p_rows = xs.shape[0]
    c = MOE_CHUNK

    def chunk(s, j, ie, ist, ins, ni):
        return jnp.where(s < ni[0], j, MOE_NCH - 1)

    grid_spec = pltpu.PrefetchScalarGridSpec(
        num_scalar_prefetch=4,
        grid=(n_slots, MOE_NCH),
        in_specs=[
            pl.BlockSpec(memory_space=pl.ANY),
            pl.BlockSpec((None, D_MODEL, c), lambda s, j, ie, *a: (ie[s], 0, chunk(s, j, ie, *a))),
            pl.BlockSpec((None, D_MODEL, c), lambda s, j, ie, *a: (ie[s], 0, MOE_NCH + chunk(s, j, ie, *a))),
            pl.BlockSpec((None, c, D_MODEL), lambda s, j, ie, *a: (ie[s], chunk(s, j, ie, *a), 0)),
            pl.BlockSpec((None, 1, c), lambda s, j, ie, *a: (ie[s], 0, chunk(s, j, ie, *a))),
            pl.BlockSpec((None, 1, c), lambda s, j, ie, *a: (ie[s], 0, MOE_NCH + chunk(s, j, ie, *a))),
            pl.BlockSpec((None, 1, D_MODEL), lambda s, j, ie, *a: (ie[s], 0, 0)),
        ],
        out_specs=pl.BlockSpec(memory_space=pl.ANY),
        scratch_shapes=[
            pltpu.VMEM((MOE_ITEM_ROWS, D_MODEL), BF16),
            pltpu.VMEM((MOE_ITEM_ROWS, D_MODEL), F32),
            pltpu.VMEM((D_MODEL, c), BF16),
            pltpu.VMEM((D_MODEL, c), BF16),
            pltpu.VMEM((c, D_MODEL), BF16),
            pltpu.SemaphoreType.DMA,
            pltpu.SemaphoreType.DMA,
        ],
    )
    return pl.pallas_call(
        _expert_kernel,
        grid_spec=grid_spec,
        out_shape=jax.ShapeDtypeStruct((p_rows, D_MODEL), F32),
        compiler_params=_params(2),
        name="moe_experts",
    )(item_e, item_start, item_nsub, n_items, xs, w_gu, w_gu, w_dn,
      b_gu.reshape(N_EXPERTS, 1, 2 * D_EXPERT), b_gu.reshape(N_EXPERTS, 1, 2 * D_EXPERT),
      b_dn.reshape(N_EXPERTS, 1, D_MODEL))


def _combine_kernel(dest_ref, x_ref, gate_ref, gf_ref, y_hbm, o_ref, buf, sem, *, final_norm):
    tc = x_ref.shape[0]

    def row_copy(d, k, r):
        return pltpu.make_async_copy(y_hbm.at[pl.ds(d, 1), :], buf.at[k, pl.ds(r, 1), :], sem)

    def issue(r, carry):
        for k in range(TOP_K):
            row_copy(dest_ref[0, 0, r * TOP_K + k], k, r).start()
        return carry

    lax.fori_loop(0, tc, issue, 0)

    def drain(r, carry):
        for k in range(TOP_K):
            row_copy(0, k, r).wait()
        return carry

    lax.fori_loop(0, tc, drain, 0)
    gate = gate_ref[...]
    acc = x_ref[...]
    for k in range(TOP_K):
        acc = acc + gate[:, k:k + 1] * buf[k]
    o_ref[...] = _rms(acc, gf_ref[...]) if final_norm else acc


def _combine(dest, x2, gate, g_final, y, final_norm):
    tc = COMBINE_TILE
    n = x2.shape[0]
    return pl.pallas_call(
        functools.partial(_combine_kernel, final_norm=final_norm),
        grid=(n // tc,),
        in_specs=[pl.BlockSpec((1, 1, tc * TOP_K), lambda i: (i, 0, 0), memory_space=pltpu.SMEM),
                  pl.BlockSpec((tc, D_MODEL), lambda i: (i, 0)),
                  pl.BlockSpec((tc, LANES), lambda i: (i, 0)),
                  _const_spec((1, D_MODEL)),
                  pl.BlockSpec(memory_space=pl.ANY)],
        out_specs=pl.BlockSpec((tc, D_MODEL), lambda i: (i, 0)),
        out_shape=jax.ShapeDtypeStruct((n, D_MODEL), F32),
        scratch_shapes=[pltpu.VMEM((TOP_K, tc, D_MODEL), F32), pltpu.SemaphoreType.DMA],
        compiler_params=_params(1),
        name="moe_combine",
    )(dest.reshape(n // tc, 1, tc * TOP_K), x2, gate, g_final, y)


def _route(top_idx):
    n_tok = top_idx.shape[0]
    n_assign = n_tok * TOP_K
    sub, item_rows = MOE_SUB, MOE_ITEM_ROWS
    p_rows = -(-(n_assign + N_EXPERTS * (sub - 1)) // sub) * sub
    n_slots = N_EXPERTS + -(-p_rows // item_rows)
    experts = jnp.arange(N_EXPERTS, dtype=jnp.int32)

    flat_e = top_idx.reshape(-1)
    onehot = (flat_e[:, None] == experts[None, :]).astype(jnp.int32)
    running = jnp.cumsum(onehot, axis=0)
    rank = jnp.sum(running * onehot, axis=1) - 1
    counts = running[-1]
    padded = (counts + sub - 1) // sub * sub
    pad_end = jnp.cumsum(padded)
    pad_start = pad_end - padded
    dest = pad_start[flat_e] + rank
    row_token = jnp.zeros((p_rows,), jnp.int32).at[dest].set(
        jnp.arange(n_assign, dtype=jnp.int32) // TOP_K)

    items_per_e = (padded + item_rows - 1) // item_rows
    item_end = jnp.cumsum(items_per_e)
    n_items = item_end[-1]
    slot = jnp.arange(n_slots, dtype=jnp.int32)
    live = slot < n_items
    s_eff = jnp.minimum(slot, n_items - 1)
    e = jnp.minimum(jnp.searchsorted(item_end, s_eff, side="right"), N_EXPERTS - 1).astype(jnp.int32)
    local = s_eff - (item_end[e] - items_per_e[e])
    start = pad_start[e] + local * item_rows
    n_rows = jnp.clip(padded[e] - local * item_rows, 0, item_rows)
    nsub = jnp.where(live, n_rows // sub, 0).astype(jnp.int32)
    meta = jnp.stack([n_items, pad_end[-1] // sub]).astype(jnp.int32)
    return dest.astype(jnp.int32), row_token, e, start.astype(jnp.int32), nsub, meta


def _group_front(x, sb_fn, mk, mv, lw, tm_proj, tm_post, rows_per_batch, blk):
    u, va, q, k, vb = _proj(x, lw["g_mix"], lw["w_in"], tm_proj)
    b_out = sb_fn(q, k, vb)
    x2, h2, idx, gate = _post(x, u, va, b_out, mk, mv, lw, tm_post, rows_per_batch, blk)
    return x2, h2, idx, gate, k, vb, va


def kernel(x_prompt, x_sample, cache_sb_k, cache_sb_v, cache_mem_k, cache_mem_v, mem_prompt, g_mix, w_in, w_sp, b_sp, g_a, g_b, w_out, g_cross, g_mem, w_cq, w_ck, w_cv, w_co, g_moe, w_router, b_router, w_gu, b_gu, w_dn, b_dn, g_final):
    depth = g_mix.shape[0]
    bsz, seq, _ = x_prompt.shape
    dbsz, dseq, _ = x_sample.shape
    past = cache_sb_k.shape[2]
    n_mem = mem_prompt.shape[1]
    n_p, n_s = bsz * seq, dbsz * dseq

    xp = x_prompt.reshape(n_p, D_MODEL)
    xs = x_sample.reshape(n_s, D_MODEL)
    mem = mem_prompt.reshape(bsz * n_mem, D_MODEL)
    outs = [[] for _ in range(7)]
    for l in range(depth):
        lw = {
            "g_mix": g_mix[l][None], "w_in": w_in[l].astype(BF16),
            "w_sp": w_sp[l], "b_sp": b_sp[l], "g_a": g_a[l][None], "g_b": g_b[l][None],
            "w_out": w_out[l].astype(BF16), "g_cross": g_cross[l][None],
            "w_cq": w_cq[l].astype(BF16), "w_co": w_co[l].astype(BF16), "g_moe": g_moe[l][None],
            "w_router": jnp.pad(w_router[l], ((0, 0), (0, LANES - N_EXPERTS))),
            "b_router": jnp.pad(b_router[l], (0, LANES - N_EXPERTS))[None],
        }
        mk, mv = _memkv(mem, g_mem[l][None], w_ck[l].astype(BF16), w_cv[l].astype(BF16))

        def sb_p(q, k, v):
            shp = (bsz, seq, SB_WIDTH)
            return _sb_prompt(q.reshape(shp), k.reshape(shp), v.reshape(shp)).reshape(n_p, SB_WIDTH)

        def sb_s(q, k, v):
            shp = (dbsz, dseq, SB_WIDTH)
            ck = cache_sb_k[l].reshape(dbsz, past, SB_WIDTH)
            cv = cache_sb_v[l].reshape(dbsz, past, SB_WIDTH)
            return _sb_sample(q.reshape(shp), k.reshape(shp), v.reshape(shp), ck, cv).reshape(n_s, SB_WIDTH)

        xp2, hp, idx_p, gate_p, k_p, v_p, _ = _group_front(
            xp, sb_p, mk, mv, lw, 256, 256, seq, A_CHUNK)
        xs2, hs, idx_s, gate_s, k_s, v_s, va_s = _group_front(
            xs, sb_s, cache_mem_k[l].reshape(dbsz * n_mem, CA_WIDTH),
            cache_mem_v[l].reshape(dbsz * n_mem, CA_WIDTH), lw, n_s, dseq, dseq, min(dseq, A_CHUNK))

        top_idx = jnp.concatenate([idx_p[:, :TOP_K], idx_s[:, :TOP_K]], axis=0)
        dest, row_token, item_e, item_start, item_nsub, n_items = _route(top_idx)
        x_sorted = _gather(row_token, hp, hs)
        y = _experts(item_e, item_start, item_nsub, n_items, x_sorted, w_gu[l], b_gu[l], w_dn[l], b_dn[l])
        last = l == depth - 1
        dest = dest.reshape(n_p + n_s, TOP_K)
        xp = _combine(dest[:n_p], xp2, gate_p, g_final[None], y, last)
        xs = _combine(dest[n_p:], xs2, gate_s, g_final[None], y, last)

        outs[0].append(k_p.reshape(bsz, seq, SB_HEADS, SB_DH))
        outs[1].append(v_p.reshape(bsz, seq, SB_HEADS, SB_DH))
        outs[2].append(mk.reshape(bsz, n_mem, CA_HEADS, CA_DH))
        outs[3].append(mv.reshape(bsz, n_mem, CA_HEADS, CA_DH))
        outs[4].append(k_s.reshape(dbsz, dseq, SB_HEADS, SB_DH))
        outs[5].append(v_s.reshape(dbsz, dseq, SB_HEADS, SB_DH))
        outs[6].append(va_s.reshape(dbsz, dseq, A_GROUPS, A_DG))

    stacked = [jnp.stack(o, axis=0) for o in outs]
    return (xp.reshape(bsz, seq, D_MODEL), xs.reshape(dbsz, dseq, D_MODEL), *stacked)
```

```python
import functools

import jax
import jax.numpy as jnp
from jax import lax
from jax.experimental import pallas as pl
from jax.experimental.pallas import tpu as pltpu

F32 = jnp.float32
BF16 = jnp.bfloat16

D_MODEL = 2048
A_GROUPS = 8
A_DG = 128
A_WIDTH = A_GROUPS * A_DG
A_CHUNK = 128
SB_HEADS = 8
SB_DH = 128
SB_WIDTH = SB_HEADS * SB_DH
IN_WIDTH = 2 * A_WIDTH + 3 * SB_WIDTH
N_SEG = IN_WIDTH // 1024
CA_HEADS = 4
CA_DH = 128
CA_WIDTH = CA_HEADS * CA_DH
N_EXPERTS = 32
TOP_K = 4
D_EXPERT = D_MODEL
SWIGLU_LIMIT = 7.0
SWIGLU_ALPHA = 1.702
RMS_EPS = 1e-6

LANES = 128
SB_TILE = 256
SB_HEADS_PER_STEP = 2
VMEM_LIMIT = 56 * 1024 * 1024

MOE_PAD = 128
MOE_TILE = 256
MOE_BODY_ROWS = 512
MOE_ITEM_ROWS = 2048
MOE_CHUNK = 256
MOE_NCH = D_EXPERT // MOE_CHUNK
GATHER_TILE = 128
COMBINE_TILE = 128


def _rms(x, g):
    return x * lax.rsqrt(jnp.mean(x * x, axis=-1, keepdims=True) + RMS_EPS) * g


def _dot(a, b):
    return jnp.dot(a, b, preferred_element_type=F32)


def _dot_nt(a, b):
    return lax.dot_general(a, b, (((1,), (1,)), ((), ())), preferred_element_type=F32)


def _split_bf16(x):
    hi = x.astype(BF16)
    lo = (x - hi.astype(F32)).astype(BF16)
    return hi, lo


def _params(n_axes):
    return pltpu.CompilerParams(dimension_semantics=("arbitrary",) * n_axes,
                                vmem_limit_bytes=VMEM_LIMIT)


def _const_spec(shape):
    return pl.BlockSpec(shape, lambda *_: (0,) * len(shape), pipeline_mode=pl.Buffered(1))


def _proj_kernel(x_ref, g_ref, w_ref, *out_refs):
    h = _rms(x_ref[...], g_ref[...]).astype(BF16)
    for s, o_ref in enumerate(out_refs):
        o_ref[...] = _dot(h, w_ref[:, s * 1024:(s + 1) * 1024])


def _proj(x, g, w, tm):
    n = x.shape[0]
    seg = jax.ShapeDtypeStruct((n, 1024), F32)
    return pl.pallas_call(
        _proj_kernel,
        grid=(n // tm,),
        in_specs=[pl.BlockSpec((tm, D_MODEL), lambda i: (i, 0)),
                  _const_spec((1, D_MODEL)),
                  _const_spec((D_MODEL, IN_WIDTH))],
        out_specs=[pl.BlockSpec((tm, 1024), lambda i: (i, 0))] * N_SEG,
        out_shape=[seg] * N_SEG,
        compiler_params=_params(1),
        name="proj",
    )(x, g, w)


def _log_sigmoid_neg(z):
    return -(jnp.maximum(z, 0.0) + jnp.log(1.0 + jnp.exp(-jnp.abs(z))))


def _sb_block(qb, kb, vb, carry, later, mask):
    z = _dot_nt(qb, kb) * (SB_DH ** -0.5)
    l = _log_sigmoid_neg(z)
    if mask is not None:
        l = jnp.where(mask, l, 0.0)
    hi, lo = _split_bf16(l)
    cs = _dot(hi, later) + _dot(lo, later)
    a = jnp.exp(l + z + cs + carry)
    if mask is not None:
        a = jnp.where(mask, a, 0.0)
    contrib = _dot(a.astype(BF16), vb)
    return contrib, carry + cs[:, 0:1] + l[:, 0:1]


def _later_matrix(n):
    r = lax.broadcasted_iota(jnp.int32, (n, n), 0)
    c = lax.broadcasted_iota(jnp.int32, (n, n), 1)
    return (r > c).astype(BF16)


def _sb_prompt_kernel(q_ref, k_ref, v_ref, o_ref, later_ref):
    qi = pl.program_id(2)
    t = SB_TILE
    later_ref[...] = _later_matrix(t)
    r = lax.broadcasted_iota(jnp.int32, (t, t), 0)
    c = lax.broadcasted_iota(jnp.int32, (t, t), 1)
    heads = [slice(h * SB_DH, (h + 1) * SB_DH) for h in range(SB_HEADS_PER_STEP)]

    def sweep(j, carries, mask):
        off = pl.multiple_of(j * t, t)
        out = []
        for sl, carry in zip(heads, carries):
            contrib, carry = _sb_block(
                q_ref[:, sl].astype(BF16), k_ref[pl.ds(off, t), sl].astype(BF16),
                v_ref[pl.ds(off, t), sl].astype(BF16), carry, later_ref[...], mask)
            if mask is None:
                o_ref[:, sl] += contrib
            else:
                o_ref[:, sl] = contrib
            out.append(carry)
        return tuple(out)

    carries = sweep(qi, (jnp.zeros((t, 1), F32),) * len(heads), c < r)
    lax.fori_loop(0, qi, lambda step, cs: sweep(qi - 1 - step, cs, None), carries)


def _sb_prompt(q, k, v):
    bsz, t_len, _ = q.shape
    t = SB_TILE
    w = SB_HEADS_PER_STEP * SB_DH
    kv_spec = pl.BlockSpec((None, t_len, w), lambda b, h, i: (b, 0, h))
    return pl.pallas_call(
        _sb_prompt_kernel,
        grid=(bsz, SB_HEADS // SB_HEADS_PER_STEP, t_len // t),
        in_specs=[pl.BlockSpec((None, t, w), lambda b, h, i: (b, i, h)), kv_spec, kv_spec],
        out_specs=pl.BlockSpec((None, t, w), lambda b, h, i: (b, i, h)),
        out_shape=jax.ShapeDtypeStruct(q.shape, F32),
        scratch_shapes=[pltpu.VMEM((t, t), BF16)],
        compiler_params=_params(3),
        name="sb_prompt",
    )(q, k, v)


def _sb_sample_kernel(q_ref, kn_ref, vn_ref, ck_ref, cv_ref, o_ref, later_ref):
    t = SB_TILE
    n_new = q_ref.shape[0]
    later_ref[...] = _later_matrix(t)
    r = lax.broadcasted_iota(jnp.int32, (n_new, t), 0)
    c = lax.broadcasted_iota(jnp.int32, (n_new, t), 1)
    for h in range(SB_HEADS):
        sl = slice(h * SB_DH, (h + 1) * SB_DH)
        qb = q_ref[:, sl].astype(BF16)
        acc, carry = _sb_block(qb, kn_ref[:, sl].astype(BF16), vn_ref[:, sl].astype(BF16),
                               jnp.zeros((n_new, 1), F32), later_ref[...], c < r)
        for j in reversed(range(ck_ref.shape[0] // t)):
            kb = ck_ref[j * t:(j + 1) * t, h, :].astype(BF16)
            vb = cv_ref[j * t:(j + 1) * t, h, :].astype(BF16)
            contrib, carry = _sb_block(qb, kb, vb, carry, later_ref[...], None)
            acc = acc + contrib
        o_ref[:, sl] = acc


def _sb_sample(q, k_new, v_new, cache_k, cache_v, layer):
    bsz, n_new, _ = q.shape
    past = cache_k.shape[2]
    t = SB_TILE
    assert n_new <= t and past % t == 0
    pad = ((0, 0), (0, t - n_new), (0, 0))
    kn = jnp.pad(k_new, pad)
    vn = jnp.pad(v_new, pad)
    rows = lambda n: pl.BlockSpec((None, n, SB_WIDTH), lambda b: (b, 0, 0))
    cache = pl.BlockSpec((None, None, past, SB_HEADS, SB_DH), lambda b: (layer, b, 0, 0, 0))
    return pl.pallas_call(
        _sb_sample_kernel,
        grid=(bsz,),
        in_specs=[rows(n_new), rows(t), rows(t), cache, cache],
        out_specs=rows(n_new),
        out_shape=jax.ShapeDtypeStruct(q.shape, F32),
        scratch_shapes=[pltpu.VMEM((t, t), BF16)],
        compiler_params=_params(1),
        name="sb_sample",
    )(q, kn, vn, cache_k, cache_v)


def _memkv_kernel(m_ref, g_ref, wk_ref, wv_ref, k_ref, v_ref):
    h = _rms(m_ref[...], g_ref[...]).astype(BF16)
    k_ref[...] = _dot(h, wk_ref[...])
    v_ref[...] = _dot(h, wv_ref[...])


def _memkv(mem, g, wk, wv):
    n = mem.shape[0]
    tm = 256
    out = jax.ShapeDtypeStruct((n, CA_WIDTH), F32)
    return pl.pallas_call(
        _memkv_kernel,
        grid=(n // tm,),
        in_specs=[pl.BlockSpec((tm, D_MODEL), lambda i: (i, 0)),
                  _const_spec((1, D_MODEL)),
                  _const_spec((D_MODEL, CA_WIDTH)),
                  _const_spec((D_MODEL, CA_WIDTH))],
        out_specs=[pl.BlockSpec((tm, CA_WIDTH), lambda i: (i, 0))] * 2,
        out_shape=[out, out],
        compiler_params=_params(1),
        name="memkv",
    )(mem, g, wk, wv)


def _post_kernel(x_ref, u_ref, va_ref, b_ref, wsp_ref, bsp_ref, ga_ref, gb_ref, wout_ref,
                 gc_ref, wcq_ref, mk_ref, mv_ref, wco_ref, gm_ref, wr_ref, br_ref,
                 x2_ref, h2_ref, idx_ref, gate_ref):
    tm = x_ref.shape[0]
    blk = wsp_ref.shape[1]

    r = lax.broadcasted_iota(jnp.int32, (blk, blk), 0)
    c = lax.broadcasted_iota(jnp.int32, (blk, blk), 1)
    tri = (c <= r).astype(F32)
    cols = []
    for g in range(A_GROUPS):
        w = (wsp_ref[g] * tri).astype(BF16)
        bias = bsp_ref[:, g:g + 1]
        sl = slice(g * A_DG, (g + 1) * A_DG)
        rows = []
        for ch in range(tm // blk):
            rs = slice(ch * blk, (ch + 1) * blk)
            mixed = _dot(w, va_ref[rs, sl].astype(BF16)) + bias
            rows.append(u_ref[rs, sl] * mixed)
        cols.append(rows[0] if len(rows) == 1 else jnp.concatenate(rows, axis=0))
    a_out = jnp.concatenate(cols, axis=1)

    a_n = _rms(a_out, ga_ref[...]).astype(BF16)
    b_n = _rms(b_ref[...], gb_ref[...]).astype(BF16)
    x1 = x_ref[...] + _dot(a_n, wout_ref[0:A_WIDTH, :]) + _dot(b_n, wout_ref[A_WIDTH:, :])

    hc = _rms(x1, gc_ref[...]).astype(BF16)
    q = _dot(hc, wcq_ref[...])
    heads = []
    for h in range(CA_HEADS):
        sl = slice(h * CA_DH, (h + 1) * CA_DH)
        s = _dot_nt(q[:, sl].astype(BF16), mk_ref[:, sl].astype(BF16)) * (CA_DH ** -0.5)
        p = jnp.exp(s - jnp.max(s, axis=-1, keepdims=True))
        p = p / jnp.sum(p, axis=-1, keepdims=True)
        heads.append(_dot(p.astype(BF16), mv_ref[:, sl].astype(BF16)))
    o = jnp.concatenate(heads, axis=1).astype(BF16)
    x2 = x1 + _dot(o, wco_ref[...])
    x2_ref[...] = x2

    h2 = _rms(x2, gm_ref[...])
    h2_ref[...] = h2
    hh, hl = _split_bf16(h2)
    wh, wl = _split_bf16(wr_ref[...])
    logits = _dot(hh, wh) + _dot(hh, wl) + _dot(hl, wh) + br_ref[...]
    lane = lax.broadcasted_iota(jnp.int32, (tm, LANES), 1).astype(F32)
    cur = jnp.where(lane < N_EXPERTS, logits, -jnp.inf)
    vals, idxs = [], []
    for _ in range(TOP_K):
        m = jnp.max(cur, axis=-1, keepdims=True)
        i = jnp.min(jnp.where(cur == m, lane, float(LANES)), axis=-1, keepdims=True)
        vals.append(m)
        idxs.append(i)
        cur = jnp.where(lane == i, -jnp.inf, cur)
    es = [jnp.exp(v - vals[0]) for v in vals]
    denom = es[0] + es[1] + es[2] + es[3]
    idx_out = jnp.zeros((tm, LANES), F32)
    gate_out = jnp.zeros((tm, LANES), F32)
    for k in range(TOP_K):
        idx_out = jnp.where(lane == k, idxs[k], idx_out)
        gate_out = jnp.where(lane == k, es[k] / denom, gate_out)
    idx_ref[...] = idx_out.astype(jnp.int32)
    gate_ref[...] = gate_out


def _post(x, u, va, b_out, mk, mv, lw, tm, rows_per_batch, blk):
    n = x.shape[0]
    tiles_per_batch = rows_per_batch // tm
    n_mem = mk.shape[0] // (n // rows_per_batch)
    row = lambda w: pl.BlockSpec((tm, w), lambda i: (i, 0))
    mem = pl.BlockSpec((n_mem, CA_WIDTH), lambda i: (i // tiles_per_batch, 0))
    wsp = lw["w_sp"][:, :blk, :blk]
    bsp = lw["b_sp"][:, :blk].T
    return pl.pallas_call(
        _post_kernel,
        grid=(n // tm,),
        in_specs=[row(D_MODEL), row(A_WIDTH), row(A_WIDTH), row(SB_WIDTH),
                  _const_spec((A_GROUPS, blk, blk)), _const_spec((blk, A_GROUPS)),
                  _const_spec((1, A_WIDTH)), _const_spec((1, SB_WIDTH)),
                  _const_spec((2 * A_WIDTH, D_MODEL)),
                  _const_spec((1, D_MODEL)), _const_spec((D_MODEL, CA_WIDTH)),
                  mem, mem, _const_spec((CA_WIDTH, D_MODEL)),
                  _const_spec((1, D_MODEL)), _const_spec((D_MODEL, LANES)), _const_spec((1, LANES))],
        out_specs=[row(D_MODEL), row(D_MODEL), row(LANES), row(LANES)],
        out_shape=[jax.ShapeDtypeStruct((n, D_MODEL), F32), jax.ShapeDtypeStruct((n, D_MODEL), F32),
                   jax.ShapeDtypeStruct((n, LANES), jnp.int32), jax.ShapeDtypeStruct((n, LANES), F32)],
        compiler_params=_params(1),
        name="post",
    )(x, u, va, b_out, wsp, bsp, lw["g_a"], lw["g_b"], lw["w_out"], lw["g_cross"], lw["w_cq"],
      mk, mv, lw["w_co"], lw["g_moe"], lw["w_router"], lw["b_router"])


def _gather_kernel(meta, tok_ref, h_hbm, xs_ref, buf, sem):
    tg = buf.shape[0]
    live = pl.program_id(0) * tg < meta[1] * MOE_PAD

    @pl.when(live)
    def _():
        def issue(r, carry):
            t = tok_ref[0, 0, r]
            pltpu.make_async_copy(h_hbm.at[pl.ds(t, 1), :], buf.at[pl.ds(r, 1), :], sem).start()
            return carry

        lax.fori_loop(0, tg, issue, 0, unroll=8)
        pltpu.make_async_copy(h_hbm.at[pl.ds(0, tg), :], buf, sem).wait()
        xs_ref[...] = buf[...].astype(BF16)

    @pl.when(jnp.logical_not(live))
    def _():
        xs_ref[...] = jnp.zeros(xs_ref.shape, BF16)


def _gather(meta, row_token, h_all):
    tg = GATHER_TILE
    p_rows = row_token.shape[0]
    tok = row_token.reshape(p_rows // tg, 1, tg)
    grid_spec = pltpu.PrefetchScalarGridSpec(
        num_scalar_prefetch=1,
        grid=(p_rows // tg,),
        in_specs=[pl.BlockSpec((1, 1, tg), lambda i, m: (i, 0, 0), memory_space=pltpu.SMEM),
                  pl.BlockSpec(memory_space=pl.ANY)],
        out_specs=pl.BlockSpec((tg, D_MODEL), lambda i, m: (i, 0)),
        scratch_shapes=[pltpu.VMEM((tg, D_MODEL), F32), pltpu.SemaphoreType.DMA],
    )
    return pl.pallas_call(
        _gather_kernel,
        grid_spec=grid_spec,
        out_shape=jax.ShapeDtypeStruct((p_rows, D_MODEL), BF16),
        compiler_params=_params(1),
        name="moe_gather",
    )(meta, tok, h_all)


def _expert_kernel(item_e, item_start, item_units, meta,
                   xs_hbm, wg_ref, wl_ref, wd_ref, bg_ref, bl_ref, bd_ref,
                   y_hbm, x_buf, y_buf, wg_bf, wl_bf, wd_bf, sem_in, sem_out):
    s = pl.program_id(0)
    j = pl.program_id(1)
    units = item_units[s]
    start = item_start[s]
    pad = MOE_PAD

    def unit_rows(t):
        return pl.ds(pl.multiple_of(t * pad, pad), pad)

    def hbm_rows(t):
        return pl.ds(pl.multiple_of(start + t * pad, pad), pad)

    def copy_in(t):
        return pltpu.make_async_copy(xs_hbm.at[hbm_rows(t), :], x_buf.at[unit_rows(t), :], sem_in)

    def copy_out(t):
        return pltpu.make_async_copy(y_buf.at[unit_rows(t), :], y_hbm.at[hbm_rows(t), :], sem_out)

    def for_range(lo, hi, fn):
        def body(t, carry):
            fn(t)
            return carry
        lax.fori_loop(lo, hi, body, 0)

    @pl.when(jnp.logical_and(j == 0, units > 0))
    def _load():
        for_range(0, units, lambda t: copy_in(t).start())
        bias = jnp.broadcast_to(bd_ref[...], (pad, D_MODEL))

        def init(t):
            y_buf[unit_rows(t), :] = bias
        for_range(0, units, init)
        for_range(0, units, lambda t: copy_in(t).wait())

    def ffn(row0, m):
        for h in range(max(1, m // MOE_TILE)):
            mh = min(m, MOE_TILE)
            rows = pl.ds(pl.multiple_of(row0 + h * mh, pad), mh)
            xb = x_buf[rows, :]
            glu = jnp.minimum(_dot(xb, wg_bf[...]) + bg_ref[...], SWIGLU_LIMIT)
            lin = jnp.clip(_dot(xb, wl_bf[...]) + bl_ref[...], -SWIGLU_LIMIT, SWIGLU_LIMIT)
            act = glu * jax.nn.sigmoid(SWIGLU_ALPHA * glu) * (lin + 1.0)
            y_buf[rows, :] += _dot(act.astype(BF16), wd_bf[...])

    @pl.when(units > 0)
    def _compute():
        wg_bf[...] = wg_ref[...].astype(BF16)
        wl_bf[...] = wl_ref[...].astype(BF16)
        wd_bf[...] = wd_ref[...].astype(BF16)
        big = MOE_BODY_ROWS
        n_big = units // (big // pad)
        for_range(0, n_big, lambda t: ffn(t * big, big))
        base = n_big * big
        rem = units * pad - base
        m = big // 2
        while m >= pad:
            take = (rem & m) != 0

            @pl.when(take)
            def _(base=base, m=m):
                ffn(base, m)

            base = base + jnp.where(take, m, 0)
            m //= 2

    @pl.when(jnp.logical_and(j == MOE_NCH - 1, units > 0))
    def _store():
        for_range(0, units, lambda t: copy_out(t).start())
        for_range(0, units, lambda t: copy_out(t).wait())

    @pl.when(jnp.logical_and(s == pl.num_programs(0) - 1, j == MOE_NCH - 1))
    def _fill_tail():
        y_buf[pl.ds(0, pad), :] = jnp.zeros((pad, D_MODEL), F32)

        def tail_copy(t):
            dst = y_hbm.at[pl.ds(pl.multiple_of(t * pad, pad), pad), :]
            return pltpu.make_async_copy(y_buf.at[pl.ds(0, pad), :], dst, sem_out)

        n_units = y_hbm.shape[0] // pad
        for_range(meta[1], n_units, lambda t: tail_copy(t).start())
        for_range(meta[1], n_units, lambda t: tail_copy(t).wait())


def _experts(item_e, item_start, item_units, meta, xs, w_gu, b_gu, w_dn, b_dn):
    n_slots = item_e.shape[0]
    p_rows = xs.shape[0]
    c = MOE_CHUNK

    def chunk(s, j, ie, ist, iu, m):
        return jnp.where(s < m[0], j, MOE_NCH - 1)

    grid_spec = pltpu.PrefetchScalarGridSpec(
        num_scalar_prefetch=4,
        grid=(n_slots, MOE_NCH),
        in_specs=[
            pl.BlockSpec(memory_space=pl.ANY),
            pl.BlockSpec((None, D_MODEL, c), lambda s, j, ie, *a: (ie[s], 0, chunk(s, j, ie, *a))),
            pl.BlockSpec((None, D_MODEL, c), lambda s, j, ie, *a: (ie[s], 0, MOE_NCH + chunk(s, j, ie, *a))),
            pl.BlockSpec((None, c, D_MODEL), lambda s, j, ie, *a: (ie[s], chunk(s, j, ie, *a), 0)),
            pl.BlockSpec((None, 1, c), lambda s, j, ie, *a: (ie[s], 0, chunk(s, j, ie, *a))),
            pl.BlockSpec((None, 1, c), lambda s, j, ie, *a: (ie[s], 0, MOE_NCH + chunk(s, j, ie, *a))),
            pl.BlockSpec((None, 1, D_MODEL), lambda s, j, ie, *a: (ie[s], 0, 0)),
        ],
        out_specs=pl.BlockSpec(memory_space=pl.ANY),
        scratch_shapes=[
            pltpu.VMEM((MOE_ITEM_ROWS, D_MODEL), BF16),
            pltpu.VMEM((MOE_ITEM_ROWS, D_MODEL), F32),
            pltpu.VMEM((D_MODEL, c), BF16),
            pltpu.VMEM((D_MODEL, c), BF16),
            pltpu.VMEM((c, D_MODEL), BF16),
            pltpu.SemaphoreType.DMA,
            pltpu.SemaphoreType.DMA,
        ],
    )
    return pl.pallas_call(
        _expert_kernel,
        grid_spec=grid_spec,
        out_shape=jax.ShapeDtypeStruct((p_rows, D_MODEL), F32),
        compiler_params=_params(2),
        name="moe_experts",
    )(item_e, item_start, item_units, meta, xs, w_gu, w_gu, w_dn,
      b_gu.reshape(N_EXPERTS, 1, 2 * D_EXPERT), b_gu.reshape(N_EXPERTS, 1, 2 * D_EXPERT),
      b_dn.reshape(N_EXPERTS, 1, D_MODEL))


def _combine_kernel(dest_ref, x_ref, gate_ref, gf_ref, y_hbm, o_ref, buf, sem, *, final_norm):
    tc = x_ref.shape[0]

    def issue(r, carry):
        for k in range(TOP_K):
            d = dest_ref[0, 0, r * TOP_K + k]
            pltpu.make_async_copy(y_hbm.at[pl.ds(d, 1), :], buf.at[pl.ds(k * tc + r, 1), :], sem).start()
        return carry

    lax.fori_loop(0, tc, issue, 0, unroll=4)
    pltpu.make_async_copy(y_hbm.at[pl.ds(0, TOP_K * tc), :], buf, sem).wait()
    gate = gate_ref[...]
    acc = x_ref[...]
    for k in range(TOP_K):
        acc = acc + gate[:, k:k + 1] * buf[k * tc:(k + 1) * tc, :]
    o_ref[...] = _rms(acc, gf_ref[...]) if final_norm else acc


def _combine(dest, x2, gate, g_final, y, final_norm):
    tc = COMBINE_TILE
    n = x2.shape[0]
    return pl.pallas_call(
        functools.partial(_combine_kernel, final_norm=final_norm),
        grid=(n // tc,),
        in_specs=[pl.BlockSpec((1, 1, tc * TOP_K), lambda i: (i, 0, 0), memory_space=pltpu.SMEM),
                  pl.BlockSpec((tc, D_MODEL), lambda i: (i, 0)),
                  pl.BlockSpec((tc, LANES), lambda i: (i, 0)),
                  _const_spec((1, D_MODEL)),
                  pl.BlockSpec(memory_space=pl.ANY)],
        out_specs=pl.BlockSpec((tc, D_MODEL), lambda i: (i, 0)),
        out_shape=jax.ShapeDtypeStruct((n, D_MODEL), F32),
        scratch_shapes=[pltpu.VMEM((TOP_K * tc, D_MODEL), F32), pltpu.SemaphoreType.DMA],
        compiler_params=_params(1),
        name="moe_combine",
    )(dest.reshape(n // tc, 1, tc * TOP_K), x2, gate, g_final, y)


def _route(top_idx):
    n_tok = top_idx.shape[0]
    n_assign = n_tok * TOP_K
    pad, item_rows = MOE_PAD, MOE_ITEM_ROWS
    p_rows = -(-(n_assign + N_EXPERTS * (pad - 1)) // GATHER_TILE) * GATHER_TILE
    n_slots = N_EXPERTS + -(-p_rows // item_rows)
    experts = jnp.arange(N_EXPERTS, dtype=jnp.int32)

    flat_e = top_idx.reshape(-1)
    onehot = (flat_e[:, None] == experts[None, :]).astype(jnp.int32)
    running = jnp.cumsum(onehot, axis=0)
    rank = jnp.sum(running * onehot, axis=1) - 1
    counts = running[-1]
    padded = (counts + pad - 1) // pad * pad
    pad_end = jnp.cumsum(padded)
    pad_start = pad_end - padded
    dest = pad_start[flat_e] + rank
    row_token = jnp.zeros((p_rows,), jnp.int32).at[dest].set(
        jnp.arange(n_assign, dtype=jnp.int32) // TOP_K)

    items_per_e = (padded + item_rows - 1) // item_rows
    item_end = jnp.cumsum(items_per_e)
    n_items = item_end[-1]
    slot = jnp.arange(n_slots, dtype=jnp.int32)
    live = slot < n_items
    s_eff = jnp.minimum(slot, n_items - 1)
    e = jnp.minimum(jnp.searchsorted(item_end, s_eff, side="right"), N_EXPERTS - 1).astype(jnp.int32)
    local = s_eff - (item_end[e] - items_per_e[e])
    start = pad_start[e] + local * item_rows
    n_rows = jnp.clip(padded[e] - local * item_rows, 0, item_rows)
    units = jnp.where(live, n_rows // pad, 0).astype(jnp.int32)
    meta = jnp.stack([n_items, pad_end[-1] // pad]).astype(jnp.int32)
    return dest.astype(jnp.int32), row_token, e, start.astype(jnp.int32), units, meta


def _group_front(x, sb_fn, mk, mv, lw, tm_proj, tm_post, rows_per_batch, blk):
    u, va, q, k, vb = _proj(x, lw["g_mix"], lw["w_in"], tm_proj)
    b_out = sb_fn(q, k, vb)
    x2, h2, idx, gate = _post(x, u, va, b_out, mk, mv, lw, tm_post, rows_per_batch, blk)
    return x2, h2, idx, gate, k, vb, va


def kernel(x_prompt, x_sample, cache_sb_k, cache_sb_v, cache_mem_k, cache_mem_v, mem_prompt, g_mix, w_in, w_sp, b_sp, g_a, g_b, w_out, g_cross, g_mem, w_cq, w_ck, w_cv, w_co, g_moe, w_router, b_router, w_gu, b_gu, w_dn, b_dn, g_final):
    depth = g_mix.shape[0]
    bsz, seq, _ = x_prompt.shape
    dbsz, dseq, _ = x_sample.shape
    n_mem = mem_prompt.shape[1]
    n_p, n_s = bsz * seq, dbsz * dseq

    xp = x_prompt.reshape(n_p, D_MODEL)
    xs = x_sample.reshape(n_s, D_MODEL)
    mem = mem_prompt.reshape(bsz * n_mem, D_MODEL)
    outs = [[] for _ in range(7)]
    for l in range(depth):
        lw = {
            "g_mix": g_mix[l][None], "w_in": w_in[l].astype(BF16),
            "w_sp": w_sp[l], "b_sp": b_sp[l], "g_a": g_a[l][None], "g_b": g_b[l][None],
            "w_out": w_out[l].astype(BF16), "g_cross": g_cross[l][None],
            "w_cq": w_cq[l].astype(BF16), "w_co": w_co[l].astype(BF16), "g_moe": g_moe[l][None],
            "w_router": jnp.pad(w_router[l], ((0, 0), (0, LANES - N_EXPERTS))),
            "b_router": jnp.pad(b_router[l], (0, LANES - N_EXPERTS))[None],
        }
        mk, mv = _memkv(mem, g_mem[l][None], w_ck[l].astype(BF16), w_cv[l].astype(BF16))

        def sb_p(q, k, v):
            shp = (bsz, seq, SB_WIDTH)
            return _sb_prompt(q.reshape(shp), k.reshape(shp), v.reshape(shp)).reshape(n_p, SB_WIDTH)

        def sb_s(q, k, v):
            shp = (dbsz, dseq, SB_WIDTH)
            return _sb_sample(q.reshape(shp), k.reshape(shp), v.reshape(shp),
                              cache_sb_k, cache_sb_v, l).reshape(n_s, SB_WIDTH)

        xp2, hp, idx_p, gate_p, k_p, v_p, _ = _group_front(
            xp, sb_p, mk, mv, lw, 256, 256, seq, A_CHUNK)
        xs2, hs, idx_s, gate_s, k_s, v_s, va_s = _group_front(
            xs, sb_s, cache_mem_k[l].reshape(dbsz * n_mem, CA_WIDTH),
            cache_mem_v[l].reshape(dbsz * n_mem, CA_WIDTH), lw, n_s, dseq, dseq, min(dseq, A_CHUNK))

        top_idx = jnp.concatenate([idx_p[:, :TOP_K], idx_s[:, :TOP_K]], axis=0)
        dest, row_token, item_e, item_start, item_units, meta = _route(top_idx)
        x_sorted = _gather(meta, row_token, jnp.concatenate([hp, hs], axis=0))
        y = _experts(item_e, item_start, item_units, meta, x_sorted, w_gu[l], b_gu[l], w_dn[l], b_dn[l])
        last = l == depth - 1
        dest = dest.reshape(n_p + n_s, TOP_K)
        xp = _combine(dest[:n_p], xp2, gate_p, g_final[None], y, last)
        xs = _combine(dest[n_p:], xs2, gate_s, g_final[None], y, last)

        outs[0].append(k_p.reshape(bsz, seq, SB_HEADS, SB_DH))
        outs[1].append(v_p.reshape(bsz, seq, SB_HEADS, SB_DH))
        outs[2].append(mk.reshape(bsz, n_mem, CA_HEADS, CA_DH))
        outs[3].append(mv.reshape(bsz, n_mem, CA_HEADS, CA_DH))
        outs[4].append(k_s.reshape(dbsz, dseq, SB_HEADS, SB_DH))
        outs[5].append(v_s.reshape(dbsz, dseq, SB_HEADS, SB_DH))
        outs[6].append(va_s.reshape(dbsz, dseq, A_GROUPS, A_DG))

    stacked = [jnp.stack(o, axis=0) for o in outs]
    return (xp.reshape(bsz, seq, D_MODEL), xs.reshape(dbsz, dseq, D_MODEL), *stacked)
```

```python
import functools

import jax
import jax.numpy as jnp
from jax import lax
from jax.experimental import pallas as pl
from jax.experimental.pallas import tpu as pltpu

F32 = jnp.float32
BF16 = jnp.bfloat16

D_MODEL = 2048
A_GROUPS = 8
A_DG = 128
A_WIDTH = A_GROUPS * A_DG
A_CHUNK = 128
SB_HEADS = 8
SB_DH = 128
SB_WIDTH = SB_HEADS * SB_DH
IN_WIDTH = 2 * A_WIDTH + 3 * SB_WIDTH
N_SEG = IN_WIDTH // 1024
CA_HEADS = 4
CA_DH = 128
CA_WIDTH = CA_HEADS * CA_DH
N_EXPERTS = 32
TOP_K = 4
D_EXPERT = D_MODEL
SWIGLU_LIMIT = 7.0
SWIGLU_ALPHA = 1.702
RMS_EPS = 1e-6

LANES = 128
SB_TILE = 256
SB_HEADS_PER_STEP = 4
VMEM_LIMIT = 56 * 1024 * 1024

MOE_PAD = 128
MOE_TILE = 256
MOE_BODY_ROWS = 512
MOE_ITEM_ROWS = 1536
MOE_CHUNK = 512
MOE_NCH = D_EXPERT // MOE_CHUNK
GATHER_TILE = 256
COMBINE_TILE = 128


def _rms(x, g):
    return x * lax.rsqrt(jnp.mean(x * x, axis=-1, keepdims=True) + RMS_EPS) * g


def _dot(a, b):
    return jnp.dot(a, b, preferred_element_type=F32)


def _dot_nt(a, b):
    return lax.dot_general(a, b, (((1,), (1,)), ((), ())), preferred_element_type=F32)


def _split_bf16(x):
    hi = x.astype(BF16)
    lo = (x - hi.astype(F32)).astype(BF16)
    return hi, lo


def _params(n_axes):
    return pltpu.CompilerParams(dimension_semantics=("arbitrary",) * n_axes,
                                vmem_limit_bytes=VMEM_LIMIT)


def _const_spec(shape):
    return pl.BlockSpec(shape, lambda *_: (0,) * len(shape), pipeline_mode=pl.Buffered(1))


def _proj_kernel(x_ref, g_ref, w_ref, *out_refs):
    h = _rms(x_ref[...], g_ref[...]).astype(BF16)
    for s, o_ref in enumerate(out_refs):
        o_ref[...] = _dot(h, w_ref[:, s * 1024:(s + 1) * 1024])


def _proj(x, g, w, tm):
    n = x.shape[0]
    seg = jax.ShapeDtypeStruct((n, 1024), F32)
    return pl.pallas_call(
        _proj_kernel,
        grid=(n // tm,),
        in_specs=[pl.BlockSpec((tm, D_MODEL), lambda i: (i, 0)),
                  _const_spec((1, D_MODEL)),
                  _const_spec((D_MODEL, IN_WIDTH))],
        out_specs=[pl.BlockSpec((tm, 1024), lambda i: (i, 0))] * N_SEG,
        out_shape=[seg] * N_SEG,
        compiler_params=_params(1),
        name="proj",
    )(x, g, w)


def _log_sigmoid_neg(z):
    return -(jnp.maximum(z, 0.0) + jnp.log(1.0 + jnp.exp(-jnp.abs(z))))


def _sb_blocks(qbs, kbs, vbs, carries, later, mask):
    n = range(len(qbs))
    zs = [_dot_nt(qbs[i], kbs[i]) * (SB_DH ** -0.5) for i in n]
    ls = [_log_sigmoid_neg(z) for z in zs]
    if mask is not None:
        ls = [jnp.where(mask, l, 0.0) for l in ls]
    splits = [_split_bf16(l) for l in ls]
    css = [_dot(hi, later) + _dot(lo, later) for hi, lo in splits]
    aas = [jnp.exp(ls[i] + zs[i] + css[i] + carries[i]) for i in n]
    if mask is not None:
        aas = [jnp.where(mask, a, 0.0) for a in aas]
    contribs = [_dot(aas[i].astype(BF16), vbs[i]) for i in n]
    new = [carries[i] + css[i][:, 0:1] + ls[i][:, 0:1] for i in n]
    return contribs, new


def _sb_block(qb, kb, vb, carry, later, mask):
    contribs, new = _sb_blocks([qb], [kb], [vb], [carry], later, mask)
    return contribs[0], new[0]


def _later_matrix(n):
    r = lax.broadcasted_iota(jnp.int32, (n, n), 0)
    c = lax.broadcasted_iota(jnp.int32, (n, n), 1)
    return (r > c).astype(BF16)


def _sb_prompt_kernel(q_ref, k_ref, v_ref, o_ref, later_ref, carry_ref):
    qi = pl.program_id(2)
    t = SB_TILE
    later_ref[...] = _later_matrix(t)
    r = lax.broadcasted_iota(jnp.int32, (t, t), 0)
    c = lax.broadcasted_iota(jnp.int32, (t, t), 1)
    heads = [slice(h * SB_DH, (h + 1) * SB_DH) for h in range(SB_HEADS_PER_STEP)]

    def sweep(j, first):
        off = pl.multiple_of(j * t, t)
        rows = pl.ds(off, t)
        carries = [jnp.zeros((t, 1), F32) if first else carry_ref[h] for h in range(len(heads))]
        contribs, carries = _sb_blocks(
            [q_ref[:, sl].astype(BF16) for sl in heads],
            [k_ref[rows, sl].astype(BF16) for sl in heads],
            [v_ref[rows, sl].astype(BF16) for sl in heads],
            carries, later_ref[...], (c < r) if first else None)
        for h, sl in enumerate(heads):
            carry_ref[h] = carries[h]
            if first:
                o_ref[:, sl] = contribs[h]
            else:
                o_ref[:, sl] += contribs[h]

    sweep(qi, True)

    def body(step, carry):
        sweep(qi - 1 - step, False)
        return carry

    lax.fori_loop(0, qi, body, 0)


def _sb_prompt(q, k, v):
    bsz, t_len, _ = q.shape
    t = SB_TILE
    w = SB_HEADS_PER_STEP * SB_DH
    kv_spec = pl.BlockSpec((None, t_len, w), lambda b, h, i: (b, 0, h))
    return pl.pallas_call(
        _sb_prompt_kernel,
        grid=(bsz, SB_HEADS // SB_HEADS_PER_STEP, t_len // t),
        in_specs=[pl.BlockSpec((None, t, w), lambda b, h, i: (b, i, h)), kv_spec, kv_spec],
        out_specs=pl.BlockSpec((None, t, w), lambda b, h, i: (b, i, h)),
        out_shape=jax.ShapeDtypeStruct(q.shape, F32),
        scratch_shapes=[pltpu.VMEM((t, t), BF16), pltpu.VMEM((SB_HEADS_PER_STEP, t, 1), F32)],
        compiler_params=_params(3),
        name="sb_prompt",
    )(q, k, v)


def _sb_sample_kernel(q_ref, kn_ref, vn_ref, ck_ref, cv_ref, o_ref, later_ref):
    t = SB_TILE
    n_new = q_ref.shape[0]
    later_ref[...] = _later_matrix(t)
    r = lax.broadcasted_iota(jnp.int32, (n_new, t), 0)
    c = lax.broadcasted_iota(jnp.int32, (n_new, t), 1)
    heads = range(SB_HEADS)
    sls = [slice(h * SB_DH, (h + 1) * SB_DH) for h in heads]
    qbs = [q_ref[:, sl].astype(BF16) for sl in sls]
    accs, carries = _sb_blocks(qbs, [kn_ref[:, sl].astype(BF16) for sl in sls],
                               [vn_ref[:, sl].astype(BF16) for sl in sls],
                               [jnp.zeros((n_new, 1), F32)] * SB_HEADS, later_ref[...], c < r)
    for j in reversed(range(ck_ref.shape[0] // (t * SB_HEADS))):
        rows = [pl.ds(j * t * SB_HEADS + h, t, stride=SB_HEADS) for h in heads]
        contribs, carries = _sb_blocks(qbs, [ck_ref[rows[h], :].astype(BF16) for h in heads],
                                       [cv_ref[rows[h], :].astype(BF16) for h in heads],
                                       carries, later_ref[...], None)
        accs = [a + cb for a, cb in zip(accs, contribs)]
    for sl, acc in zip(sls, accs):
        o_ref[:, sl] = acc


def _sb_sample(q, k_new, v_new, cache_k, cache_v, layer):
    bsz, n_new, _ = q.shape
    past = cache_k.shape[2]
    t = SB_TILE
    assert n_new <= t and past % t == 0
    pad = ((0, 0), (0, t - n_new), (0, 0))
    kn = jnp.pad(k_new, pad)
    vn = jnp.pad(v_new, pad)
    rows = lambda n: pl.BlockSpec((None, n, SB_WIDTH), lambda b: (b, 0, 0))
    depth = cache_k.shape[0]
    flat = (depth, bsz, past * SB_HEADS, SB_DH)
    cache = pl.BlockSpec((None, None, past * SB_HEADS, SB_DH), lambda b: (layer, b, 0, 0))
    return pl.pallas_call(
        _sb_sample_kernel,
        grid=(bsz,),
        in_specs=[rows(n_new), rows(t), rows(t), cache, cache],
        out_specs=rows(n_new),
        out_shape=jax.ShapeDtypeStruct(q.shape, F32),
        scratch_shapes=[pltpu.VMEM((t, t), BF16)],
        compiler_params=_params(1),
        name="sb_sample",
    )(q, kn, vn, cache_k.reshape(flat), cache_v.reshape(flat))


def _memkv_kernel(m_ref, g_ref, wk_ref, wv_ref, k_ref, v_ref):
    h = _rms(m_ref[...], g_ref[...]).astype(BF16)
    k_ref[...] = _dot(h, wk_ref[...])
    v_ref[...] = _dot(h, wv_ref[...])


def _memkv(mem, g, wk, wv):
    n = mem.shape[0]
    tm = 256
    out = jax.ShapeDtypeStruct((n, CA_WIDTH), F32)
    return pl.pallas_call(
        _memkv_kernel,
        grid=(n // tm,),
        in_specs=[pl.BlockSpec((tm, D_MODEL), lambda i: (i, 0)),
                  _const_spec((1, D_MODEL)),
                  _const_spec((D_MODEL, CA_WIDTH)),
                  _const_spec((D_MODEL, CA_WIDTH))],
        out_specs=[pl.BlockSpec((tm, CA_WIDTH), lambda i: (i, 0))] * 2,
        out_shape=[out, out],
        compiler_params=_params(1),
        name="memkv",
    )(mem, g, wk, wv)


def _post_kernel(x_ref, u_ref, va_ref, b_ref, wsp_ref, bsp_ref, ga_ref, gb_ref, wout_ref,
                 gc_ref, wcq_ref, mk_ref, mv_ref, wco_ref, gm_ref, wr_ref, br_ref,
                 x2_ref, h2_ref, idx_ref, gate_ref):
    tm = x_ref.shape[0]
    blk = wsp_ref.shape[1]

    r = lax.broadcasted_iota(jnp.int32, (blk, blk), 0)
    c = lax.broadcasted_iota(jnp.int32, (blk, blk), 1)
    tri = (c <= r).astype(F32)
    cols = []
    for g in range(A_GROUPS):
        w = (wsp_ref[g] * tri).astype(BF16)
        bias = bsp_ref[:, g:g + 1]
        sl = slice(g * A_DG, (g + 1) * A_DG)
        rows = []
        for ch in range(tm // blk):
            rs = slice(ch * blk, (ch + 1) * blk)
            mixed = _dot(w, va_ref[rs, sl].astype(BF16)) + bias
            rows.append(u_ref[rs, sl] * mixed)
        cols.append(rows[0] if len(rows) == 1 else jnp.concatenate(rows, axis=0))
    a_out = jnp.concatenate(cols, axis=1)

    a_n = _rms(a_out, ga_ref[...]).astype(BF16)
    b_n = _rms(b_ref[...], gb_ref[...]).astype(BF16)
    x1 = x_ref[...] + _dot(a_n, wout_ref[0:A_WIDTH, :]) + _dot(b_n, wout_ref[A_WIDTH:, :])

    hc = _rms(x1, gc_ref[...]).astype(BF16)
    q = _dot(hc, wcq_ref[...])
    heads = []
    for h in range(CA_HEADS):
        sl = slice(h * CA_DH, (h + 1) * CA_DH)
        s = _dot_nt(q[:, sl].astype(BF16), mk_ref[:, sl].astype(BF16)) * (CA_DH ** -0.5)
        p = jnp.exp(s - jnp.max(s, axis=-1, keepdims=True))
        p = p / jnp.sum(p, axis=-1, keepdims=True)
        heads.append(_dot(p.astype(BF16), mv_ref[:, sl].astype(BF16)))
    o = jnp.concatenate(heads, axis=1).astype(BF16)
    x2 = x1 + _dot(o, wco_ref[...])
    x2_ref[...] = x2

    h2 = _rms(x2, gm_ref[...])
    h2_ref[...] = h2
    hh, hl = _split_bf16(h2)
    wh, wl = _split_bf16(wr_ref[...])
    logits = _dot(hh, wh) + _dot(hh, wl) + _dot(hl, wh) + br_ref[...]
    lane = lax.broadcasted_iota(jnp.int32, (tm, LANES), 1).astype(F32)
    cur = jnp.where(lane < N_EXPERTS, logits, -jnp.inf)
    vals, idxs = [], []
    for _ in range(TOP_K):
        m = jnp.max(cur, axis=-1, keepdims=True)
        i = jnp.min(jnp.where(cur == m, lane, float(LANES)), axis=-1, keepdims=True)
        vals.append(m)
        idxs.append(i)
        cur = jnp.where(lane == i, -jnp.inf, cur)
    es = [jnp.exp(v - vals[0]) for v in vals]
    denom = es[0] + es[1] + es[2] + es[3]
    idx_out = jnp.zeros((tm, LANES), F32)
    gate_out = jnp.zeros((tm, LANES), F32)
    for k in range(TOP_K):
        idx_out = jnp.where(lane == k, idxs[k], idx_out)
        gate_out = jnp.where(lane == k, es[k] / denom, gate_out)
    idx_ref[...] = idx_out.astype(jnp.int32)
    gate_ref[...] = gate_out


def _post(x, u, va, b_out, mk, mv, lw, tm, rows_per_batch, blk):
    n = x.shape[0]
    tiles_per_batch = rows_per_batch // tm
    n_mem = mk.shape[0] // (n // rows_per_batch)
    row = lambda w: pl.BlockSpec((tm, w), lambda i: (i, 0))
    mem = pl.BlockSpec((n_mem, CA_WIDTH), lambda i: (i // tiles_per_batch, 0))
    wsp = lw["w_sp"][:, :blk, :blk]
    bsp = lw["b_sp"][:, :blk].T
    return pl.pallas_call(
        _post_kernel,
        grid=(n // tm,),
        in_specs=[row(D_MODEL), row(A_WIDTH), row(A_WIDTH), row(SB_WIDTH),
                  _const_spec((A_GROUPS, blk, blk)), _const_spec((blk, A_GROUPS)),
                  _const_spec((1, A_WIDTH)), _const_spec((1, SB_WIDTH)),
                  _const_spec((2 * A_WIDTH, D_MODEL)),
                  _const_spec((1, D_MODEL)), _const_spec((D_MODEL, CA_WIDTH)),
                  mem, mem, _const_spec((CA_WIDTH, D_MODEL)),
                  _const_spec((1, D_MODEL)), _const_spec((D_MODEL, LANES)), _const_spec((1, LANES))],
        out_specs=[row(D_MODEL), row(D_MODEL), row(LANES), row(LANES)],
        out_shape=[jax.ShapeDtypeStruct((n, D_MODEL), F32), jax.ShapeDtypeStruct((n, D_MODEL), F32),
                   jax.ShapeDtypeStruct((n, LANES), jnp.int32), jax.ShapeDtypeStruct((n, LANES), F32)],
        compiler_params=_params(1),
        name="post",
    )(x, u, va, b_out, wsp, bsp, lw["g_a"], lw["g_b"], lw["w_out"], lw["g_cross"], lw["w_cq"],
      mk, mv, lw["w_co"], lw["g_moe"], lw["w_router"], lw["b_router"])


def _gather_kernel(meta, tok_ref, next_tok_ref, h_hbm, xs_ref, buf, sem):
    i = pl.program_id(0)
    tg = xs_ref.shape[0]
    n_live = (meta[1] * MOE_PAD + tg - 1) // tg
    slot = i % 2

    def issue(ref, slot):
        def body(r, carry):
            pltpu.make_async_copy(h_hbm.at[pl.ds(ref[0, 0, r], 1), :],
                                  buf.at[slot, pl.ds(r, 1), :], sem.at[slot]).start()
            return carry
        lax.fori_loop(0, tg, body, 0, unroll=8)

    @pl.when(i == 0)
    def _():
        issue(tok_ref, 0)

    @pl.when(i + 1 < n_live)
    def _():
        issue(next_tok_ref, 1 - slot)

    @pl.when(i < n_live)
    def _():
        pltpu.make_async_copy(h_hbm.at[pl.ds(0, tg), :], buf.at[slot], sem.at[slot]).wait()
        xs_ref[...] = buf[slot].astype(BF16)

    @pl.when(i >= n_live)
    def _():
        xs_ref[...] = jnp.zeros(xs_ref.shape, BF16)


def _gather(meta, row_token, h_all):
    tg = GATHER_TILE
    p_rows = row_token.shape[0]
    n_tiles = p_rows // tg
    tok = row_token.reshape(n_tiles, 1, tg)
    grid_spec = pltpu.PrefetchScalarGridSpec(
        num_scalar_prefetch=1,
        grid=(n_tiles,),
        in_specs=[pl.BlockSpec((1, 1, tg), lambda i, m: (i, 0, 0), memory_space=pltpu.SMEM),
                  pl.BlockSpec((1, 1, tg), lambda i, m: (jnp.minimum(i + 1, n_tiles - 1), 0, 0),
                               memory_space=pltpu.SMEM),
                  pl.BlockSpec(memory_space=pl.ANY)],
        out_specs=pl.BlockSpec((tg, D_MODEL), lambda i, m: (i, 0)),
        scratch_shapes=[pltpu.VMEM((2, tg, D_MODEL), F32), pltpu.SemaphoreType.DMA((2,))],
    )
    return pl.pallas_call(
        _gather_kernel,
        grid_spec=grid_spec,
        out_shape=jax.ShapeDtypeStruct((p_rows, D_MODEL), BF16),
        compiler_params=_params(1),
        name="moe_gather",
    )(meta, tok, tok, h_all)


def _expert_kernel(item_e, item_start, item_units, meta,
                   xs_hbm, wg_ref, wl_ref, wd_ref, bg_ref, bl_ref, bd_ref,
                   y_hbm, x_buf, y_buf, wg_bf, wl_bf, wd_bf, sem_in, sem_out):
    s = pl.program_id(0)
    j = pl.program_id(1)
    units = item_units[s]
    start = item_start[s]
    pad = MOE_PAD

    def unit_rows(t):
        return pl.ds(pl.multiple_of(t * pad, pad), pad)

    def hbm_rows(t):
        return pl.ds(pl.multiple_of(start + t * pad, pad), pad)

    def copy_in(t):
        return pltpu.make_async_copy(xs_hbm.at[hbm_rows(t), :], x_buf.at[unit_rows(t), :], sem_in)

    def copy_out(t):
        return pltpu.make_async_copy(y_buf.at[unit_rows(t), :], y_hbm.at[hbm_rows(t), :], sem_out)

    def for_range(lo, hi, fn):
        def body(t, carry):
            fn(t)
            return carry
        lax.fori_loop(lo, hi, body, 0)

    @pl.when(jnp.logical_and(j == 0, units > 0))
    def _load():
        for_range(0, units, lambda t: copy_in(t).start())
        bias = jnp.broadcast_to(bd_ref[...], (pad, D_MODEL))

        def init(t):
            y_buf[unit_rows(t), :] = bias
        for_range(0, units, init)
        for_range(0, units, lambda t: copy_in(t).wait())

    def ffn(row0, m):
        for h in range(max(1, m // MOE_TILE)):
            mh = min(m, MOE_TILE)
            rows = pl.ds(pl.multiple_of(row0 + h * mh, pad), mh)
            xb = x_buf[rows, :]
            glu = jnp.minimum(_dot(xb, wg_bf[...]) + bg_ref[...], SWIGLU_LIMIT)
            lin = jnp.clip(_dot(xb, wl_bf[...]) + bl_ref[...], -SWIGLU_LIMIT, SWIGLU_LIMIT)
            act = glu * jax.nn.sigmoid(SWIGLU_ALPHA * glu) * (lin + 1.0)
            y_buf[rows, :] += _dot(act.astype(BF16), wd_bf[...])

    @pl.when(units > 0)
    def _compute():
        wg_bf[...] = wg_ref[...].astype(BF16)
        wl_bf[...] = wl_ref[...].astype(BF16)
        wd_bf[...] = wd_ref[...].astype(BF16)
        big = MOE_BODY_ROWS
        n_big = units // (big // pad)
        for_range(0, n_big, lambda t: ffn(t * big, big))
        base = n_big * big
        rem = units * pad - base
        m = big // 2
        while m >= pad:
            take = (rem & m) != 0

            @pl.when(take)
            def _(base=base, m=m):
                ffn(base, m)

            base = base + jnp.where(take, m, 0)
            m //= 2

    @pl.when(jnp.logical_and(j == MOE_NCH - 1, units > 0))
    def _store():
        for_range(0, units, lambda t: copy_out(t).start())
        for_range(0, units, lambda t: copy_out(t).wait())

    @pl.when(jnp.logical_and(s == pl.num_programs(0) - 1, j == MOE_NCH - 1))
    def _fill_tail():
        y_buf[pl.ds(0, pad), :] = jnp.zeros((pad, D_MODEL), F32)

        def tail_copy(t):
            dst = y_hbm.at[pl.ds(pl.multiple_of(t * pad, pad), pad), :]
            return pltpu.make_async_copy(y_buf.at[pl.ds(0, pad), :], dst, sem_out)

        n_units = y_hbm.shape[0] // pad
        for_range(meta[1], n_units, lambda t: tail_copy(t).start())
        for_range(meta[1], n_units, lambda t: tail_copy(t).wait())


def _experts(item_e, item_start, item_units, meta, xs, w_gu, b_gu, w_dn, b_dn):
    n_slots = item_e.shape[0]
    p_rows = xs.shape[0]
    c = MOE_CHUNK

    def chunk(s, j, ie, ist, iu, m):
        return jnp.where(s < m[0], j, MOE_NCH - 1)

    grid_spec = pltpu.PrefetchScalarGridSpec(
        num_scalar_prefetch=4,
        grid=(n_slots, MOE_NCH),
        in_specs=[
            pl.BlockSpec(memory_space=pl.ANY),
            pl.BlockSpec((None, D_MODEL, c), lambda s, j, ie, *a: (ie[s], 0, chunk(s, j, ie, *a))),
            pl.BlockSpec((None, D_MODEL, c), lambda s, j, ie, *a: (ie[s], 0, MOE_NCH + chunk(s, j, ie, *a))),
            pl.BlockSpec((None, c, D_MODEL), lambda s, j, ie, *a: (ie[s], chunk(s, j, ie, *a), 0)),
            pl.BlockSpec((None, 1, c), lambda s, j, ie, *a: (ie[s], 0, chunk(s, j, ie, *a))),
            pl.BlockSpec((None, 1, c), lambda s, j, ie, *a: (ie[s], 0, MOE_NCH + chunk(s, j, ie, *a))),
            pl.BlockSpec((None, 1, D_MODEL), lambda s, j, ie, *a: (ie[s], 0, 0)),
        ],
        out_specs=pl.BlockSpec(memory_space=pl.ANY),
        scratch_shapes=[
            pltpu.VMEM((MOE_ITEM_ROWS, D_MODEL), BF16),
            pltpu.VMEM((MOE_ITEM_ROWS, D_MODEL), F32),
            pltpu.VMEM((D_MODEL, c), BF16),
            pltpu.VMEM((D_MODEL, c), BF16),
            pltpu.VMEM((c, D_MODEL), BF16),
            pltpu.SemaphoreType.DMA,
            pltpu.SemaphoreType.DMA,
        ],
    )
    return pl.pallas_call(
        _expert_kernel,
        grid_spec=grid_spec,
        out_shape=jax.ShapeDtypeStruct((p_rows, D_MODEL), F32),
        compiler_params=_params(2),
        name="moe_experts",
    )(item_e, item_start, item_units, meta, xs, w_gu, w_gu, w_dn,
      b_gu.reshape(N_EXPERTS, 1, 2 * D_EXPERT), b_gu.reshape(N_EXPERTS, 1, 2 * D_EXPERT),
      b_dn.reshape(N_EXPERTS, 1, D_MODEL))


def _combine_kernel(dest_ref, next_dest_ref, x_ref, gate_ref, gf_ref, y_hbm, o_ref, buf, sem, *,
                    final_norm):
    i = pl.program_id(0)
    tc = x_ref.shape[0]
    slot = i % 2

    def issue(ref, slot):
        def body(r, carry):
            for k in range(TOP_K):
                pltpu.make_async_copy(y_hbm.at[pl.ds(ref[0, 0, r * TOP_K + k], 1), :],
                                      buf.at[slot, pl.ds(k * tc + r, 1), :], sem.at[slot]).start()
            return carry
        lax.fori_loop(0, tc, body, 0, unroll=4)

    @pl.when(i == 0)
    def _():
        issue(dest_ref, 0)

    @pl.when(i + 1 < pl.num_programs(0))
    def _():
        issue(next_dest_ref, 1 - slot)

    pltpu.make_async_copy(y_hbm.at[pl.ds(0, TOP_K * tc), :], buf.at[slot], sem.at[slot]).wait()
    gate = gate_ref[...]
    acc = x_ref[...]
    for k in range(TOP_K):
        acc = acc + gate[:, k:k + 1] * buf[slot, k * tc:(k + 1) * tc, :]
    o_ref[...] = _rms(acc, gf_ref[...]) if final_norm else acc


def _combine(dest, x2, gate, g_final, y, final_norm):
    tc = COMBINE_TILE
    n = x2.shape[0]
    n_tiles = n // tc
    dest = dest.reshape(n_tiles, 1, tc * TOP_K)
    return pl.pallas_call(
        functools.partial(_combine_kernel, final_norm=final_norm),
        grid=(n_tiles,),
        in_specs=[pl.BlockSpec((1, 1, tc * TOP_K), lambda i: (i, 0, 0), memory_space=pltpu.SMEM),
                  pl.BlockSpec((1, 1, tc * TOP_K), lambda i: (jnp.minimum(i + 1, n_tiles - 1), 0, 0),
                               memory_space=pltpu.SMEM),
                  pl.BlockSpec((tc, D_MODEL), lambda i: (i, 0)),
                  pl.BlockSpec((tc, LANES), lambda i: (i, 0)),
                  _const_spec((1, D_MODEL)),
                  pl.BlockSpec(memory_space=pl.ANY)],
        out_specs=pl.BlockSpec((tc, D_MODEL), lambda i: (i, 0)),
        out_shape=jax.ShapeDtypeStruct((n, D_MODEL), F32),
        scratch_shapes=[pltpu.VMEM((2, TOP_K * tc, D_MODEL), F32), pltpu.SemaphoreType.DMA((2,))],
        compiler_params=_params(1),
        name="moe_combine",
    )(dest, dest, x2, gate, g_final, y)


def _route(top_idx):
    n_tok = top_idx.shape[0]
    n_assign = n_tok * TOP_K
    pad, item_rows = MOE_PAD, MOE_ITEM_ROWS
    p_rows = -(-(n_assign + N_EXPERTS * (pad - 1)) // GATHER_TILE) * GATHER_TILE
    n_slots = N_EXPERTS + -(-p_rows // item_rows)
    experts = jnp.arange(N_EXPERTS, dtype=jnp.int32)

    flat_e = top_idx.reshape(-1)
    onehot = (flat_e[:, None] == experts[None, :]).astype(jnp.int32)
    running = jnp.cumsum(onehot, axis=0)
    rank = jnp.sum(running * onehot, axis=1) - 1
    counts = running[-1]
    padded = (counts + pad - 1) // pad * pad
    pad_end = jnp.cumsum(padded)
    pad_start = pad_end - padded
    dest = pad_start[flat_e] + rank
    row_token = jnp.zeros((p_rows,), jnp.int32).at[dest].set(
        jnp.arange(n_assign, dtype=jnp.int32) // TOP_K)

    items_per_e = (padded + item_rows - 1) // item_rows
    item_end = jnp.cumsum(items_per_e)
    n_items = item_end[-1]
    slot = jnp.arange(n_slots, dtype=jnp.int32)
    live = slot < n_items
    s_eff = jnp.minimum(slot, n_items - 1)
    e = jnp.minimum(jnp.searchsorted(item_end, s_eff, side="right"), N_EXPERTS - 1).astype(jnp.int32)
    local = s_eff - (item_end[e] - items_per_e[e])
    start = pad_start[e] + local * item_rows
    n_rows = jnp.clip(padded[e] - local * item_rows, 0, item_rows)
    units = jnp.where(live, n_rows // pad, 0).astype(jnp.int32)
    meta = jnp.stack([n_items, pad_end[-1] // pad]).astype(jnp.int32)
    return dest.astype(jnp.int32), row_token, e, start.astype(jnp.int32), units, meta


def _group_front(x, sb_fn, mk, mv, lw, tm_proj, tm_post, rows_per_batch, blk):
    u, va, q, k, vb = _proj(x, lw["g_mix"], lw["w_in"], tm_proj)
    b_out = sb_fn(q, k, vb)
    x2, h2, idx, gate = _post(x, u, va, b_out, mk, mv, lw, tm_post, rows_per_batch, blk)
    return x2, h2, idx, gate, k, vb, va


def kernel(x_prompt, x_sample, cache_sb_k, cache_sb_v, cache_mem_k, cache_mem_v, mem_prompt, g_mix, w_in, w_sp, b_sp, g_a, g_b, w_out, g_cross, g_mem, w_cq, w_ck, w_cv, w_co, g_moe, w_router, b_router, w_gu, b_gu, w_dn, b_dn, g_final):
    depth = g_mix.shape[0]
    bsz, seq, _ = x_prompt.shape
    dbsz, dseq, _ = x_sample.shape
    n_mem = mem_prompt.shape[1]
    n_p, n_s = bsz * seq, dbsz * dseq

    xp = x_prompt.reshape(n_p, D_MODEL)
    xs = x_sample.reshape(n_s, D_MODEL)
    mem = mem_prompt.reshape(bsz * n_mem, D_MODEL)
    outs = [[] for _ in range(7)]
    for l in range(depth):
        lw = {
            "g_mix": g_mix[l][None], "w_in": w_in[l].astype(BF16),
            "w_sp": w_sp[l], "b_sp": b_sp[l], "g_a": g_a[l][None], "g_b": g_b[l][None],
            "w_out": w_out[l].astype(BF16), "g_cross": g_cross[l][None],
            "w_cq": w_cq[l].astype(BF16), "w_co": w_co[l].astype(BF16), "g_moe": g_moe[l][None],
            "w_router": jnp.pad(w_router[l], ((0, 0), (0, LANES - N_EXPERTS))),
            "b_router": jnp.pad(b_router[l], (0, LANES - N_EXPERTS))[None],
        }
        mk, mv = _memkv(mem, g_mem[l][None], w_ck[l].astype(BF16), w_cv[l].astype(BF16))

        def sb_p(q, k, v):
            shp = (bsz, seq, SB_WIDTH)
            return _sb_prompt(q.reshape(shp), k.reshape(shp), v.reshape(shp)).reshape(n_p, SB_WIDTH)

        def sb_s(q, k, v):
            shp = (dbsz, dseq, SB_WIDTH)
            return _sb_sample(q.reshape(shp), k.reshape(shp), v.reshape(shp),
                              cache_sb_k, cache_sb_v, l).reshape(n_s, SB_WIDTH)

        xp2, hp, idx_p, gate_p, k_p, v_p, _ = _group_front(
            xp, sb_p, mk, mv, lw, 256, 256, seq, A_CHUNK)
        xs2, hs, idx_s, gate_s, k_s, v_s, va_s = _group_front(
            xs, sb_s, cache_mem_k[l].reshape(dbsz * n_mem, CA_WIDTH),
            cache_mem_v[l].reshape(dbsz * n_mem, CA_WIDTH), lw, n_s, dseq, dseq, min(dseq, A_CHUNK))

        top_idx = jnp.concatenate([idx_p[:, :TOP_K], idx_s[:, :TOP_K]], axis=0)
        dest, row_token, item_e, item_start, item_units, meta = _route(top_idx)
        x_sorted = _gather(meta, row_token, jnp.concatenate([hp, hs], axis=0))
        y = _experts(item_e, item_start, item_units, meta, x_sorted, w_gu[l], b_gu[l], w_dn[l], b_dn[l])
        last = l == depth - 1
        dest = dest.reshape(n_p + n_s, TOP_K)
        xp = _combine(dest[:n_p], xp2, gate_p, g_final[None], y, last)
        xs = _combine(dest[n_p:], xs2, gate_s, g_final[None], y, last)

        outs[0].append(k_p.reshape(bsz, seq, SB_HEADS, SB_DH))
        outs[1].append(v_p.reshape(bsz, seq, SB_HEADS, SB_DH))
        outs[2].append(mk.reshape(bsz, n_mem, CA_HEADS, CA_DH))
        outs[3].append(mv.reshape(bsz, n_mem, CA_HEADS, CA_DH))
        outs[4].append(k_s.reshape(dbsz, dseq, SB_HEADS, SB_DH))
        outs[5].append(v_s.reshape(dbsz, dseq, SB_HEADS, SB_DH))
        outs[6].append(va_s.reshape(dbsz, dseq, A_GROUPS, A_DG))

    stacked = [jnp.stack(o, axis=0) for o in outs]
    return (xp.reshape(bsz, seq, D_MODEL), xs.reshape(dbsz, dseq, D_MODEL), *stacked)
```

```python
import functools

import jax
import jax.numpy as jnp
from jax import lax
from jax.experimental import pallas as pl
from jax.experimental.pallas import tpu as pltpu

F32 = jnp.float32
BF16 = jnp.bfloat16

D_MODEL = 2048
A_GROUPS = 8
A_DG = 128
A_WIDTH = A_GROUPS * A_DG
A_CHUNK = 128
SB_HEADS = 8
SB_DH = 128
SB_WIDTH = SB_HEADS * SB_DH
IN_WIDTH = 2 * A_WIDTH + 3 * SB_WIDTH
N_SEG = IN_WIDTH // 1024
CA_HEADS = 4
CA_DH = 128
CA_WIDTH = CA_HEADS * CA_DH
N_EXPERTS = 32
TOP_K = 4
D_EXPERT = D_MODEL
SWIGLU_LIMIT = 7.0
SWIGLU_ALPHA = 1.702
RMS_EPS = 1e-6

LANES = 128
SB_TILE = 256
SB_HEADS_PER_STEP = 4
VMEM_LIMIT = 56 * 1024 * 1024

MOE_PAD = 128
MOE_TILE = 256
MOE_BODY_ROWS = 512
MOE_ITEM_ROWS = 1536
MOE_CHUNK = 512
MOE_NCH = D_EXPERT // MOE_CHUNK
GATHER_TILE = 256
COMBINE_TILE = 128


def _rms(x, g):
    return x * lax.rsqrt(jnp.mean(x * x, axis=-1, keepdims=True) + RMS_EPS) * g


def _dot(a, b):
    return jnp.dot(a, b, preferred_element_type=F32)


def _dot_nt(a, b):
    return lax.dot_general(a, b, (((1,), (1,)), ((), ())), preferred_element_type=F32)


def _split_bf16(x):
    hi = x.astype(BF16)
    lo = (x - hi.astype(F32)).astype(BF16)
    return hi, lo


def _params(n_axes):
    return pltpu.CompilerParams(dimension_semantics=("arbitrary",) * n_axes,
                                vmem_limit_bytes=VMEM_LIMIT)


def _const_spec(shape):
    return pl.BlockSpec(shape, lambda *_: (0,) * len(shape), pipeline_mode=pl.Buffered(1))


def _proj_kernel(x_ref, g_ref, w_ref, *out_refs):
    h = _rms(x_ref[...], g_ref[...]).astype(BF16)
    for s, o_ref in enumerate(out_refs):
        o_ref[...] = _dot(h, w_ref[:, s * 1024:(s + 1) * 1024])


def _proj(x, g, w, tm):
    n = x.shape[0]
    seg = jax.ShapeDtypeStruct((n, 1024), F32)
    return pl.pallas_call(
        _proj_kernel,
        grid=(n // tm,),
        in_specs=[pl.BlockSpec((tm, D_MODEL), lambda i: (i, 0)),
                  _const_spec((1, D_MODEL)),
                  _const_spec((D_MODEL, IN_WIDTH))],
        out_specs=[pl.BlockSpec((tm, 1024), lambda i: (i, 0))] * N_SEG,
        out_shape=[seg] * N_SEG,
        compiler_params=_params(1),
        name="proj",
    )(x, g, w)


def _log_sigmoid_neg(z):
    return -(jnp.maximum(z, 0.0) + jnp.log(1.0 + jnp.exp(-jnp.abs(z))))


def _sb_blocks(qbs, kbs, vbs, carries, later, mask):
    n = range(len(qbs))
    zs = [_dot_nt(qbs[i], kbs[i]) * (SB_DH ** -0.5) for i in n]
    ls = [_log_sigmoid_neg(z) for z in zs]
    if mask is not None:
        ls = [jnp.where(mask, l, 0.0) for l in ls]
    splits = [_split_bf16(l) for l in ls]
    css = [_dot(hi, later) + _dot(lo, later) for hi, lo in splits]
    aas = [jnp.exp(ls[i] + zs[i] + css[i] + carries[i]) for i in n]
    if mask is not None:
        aas = [jnp.where(mask, a, 0.0) for a in aas]
    contribs = [_dot(aas[i].astype(BF16), vbs[i]) for i in n]
    new = [carries[i] + css[i][:, 0:1] + ls[i][:, 0:1] for i in n]
    return contribs, new


def _sb_block(qb, kb, vb, carry, later, mask):
    contribs, new = _sb_blocks([qb], [kb], [vb], [carry], later, mask)
    return contribs[0], new[0]


def _later_matrix(n):
    r = lax.broadcasted_iota(jnp.int32, (n, n), 0)
    c = lax.broadcasted_iota(jnp.int32, (n, n), 1)
    return (r > c).astype(BF16)


def _sb_prompt_kernel(q_ref, k_ref, v_ref, o_ref, later_ref, carry_ref):
    qi = pl.program_id(2)
    t = SB_TILE
    later_ref[...] = _later_matrix(t)
    r = lax.broadcasted_iota(jnp.int32, (t, t), 0)
    c = lax.broadcasted_iota(jnp.int32, (t, t), 1)
    heads = [slice(h * SB_DH, (h + 1) * SB_DH) for h in range(SB_HEADS_PER_STEP)]

    def sweep(j, first):
        off = pl.multiple_of(j * t, t)
        rows = pl.ds(off, t)
        carries = [jnp.zeros((t, 1), F32) if first else carry_ref[h] for h in range(len(heads))]
        contribs, carries = _sb_blocks(
            [q_ref[:, sl].astype(BF16) for sl in heads],
            [k_ref[rows, sl].astype(BF16) for sl in heads],
            [v_ref[rows, sl].astype(BF16) for sl in heads],
            carries, later_ref[...], (c < r) if first else None)
        for h, sl in enumerate(heads):
            carry_ref[h] = carries[h]
            if first:
                o_ref[:, sl] = contribs[h]
            else:
                o_ref[:, sl] += contribs[h]

    sweep(qi, True)

    def body(step, carry):
        sweep(qi - 1 - step, False)
        return carry

    lax.fori_loop(0, qi, body, 0)


def _sb_prompt(q, k, v):
    bsz, t_len, _ = q.shape
    t = SB_TILE
    w = SB_HEADS_PER_STEP * SB_DH
    kv_spec = pl.BlockSpec((None, t_len, w), lambda b, h, i: (b, 0, h))
    return pl.pallas_call(
        _sb_prompt_kernel,
        grid=(bsz, SB_HEADS // SB_HEADS_PER_STEP, t_len // t),
        in_specs=[pl.BlockSpec((None, t, w), lambda b, h, i: (b, i, h)), kv_spec, kv_spec],
        out_specs=pl.BlockSpec((None, t, w), lambda b, h, i: (b, i, h)),
        out_shape=jax.ShapeDtypeStruct(q.shape, F32),
        scratch_shapes=[pltpu.VMEM((t, t), BF16), pltpu.VMEM((SB_HEADS_PER_STEP, t, 1), F32)],
        compiler_params=_params(3),
        name="sb_prompt",
    )(q, k, v)


def _sb_sample_kernel(q_ref, kn_ref, vn_ref, ck_ref, cv_ref, o_ref, later_ref):
    t = SB_TILE
    n_new = q_ref.shape[0]
    later_ref[...] = _later_matrix(t)
    r = lax.broadcasted_iota(jnp.int32, (n_new, t), 0)
    c = lax.broadcasted_iota(jnp.int32, (n_new, t), 1)
    heads = range(SB_HEADS)
    sls = [slice(h * SB_DH, (h + 1) * SB_DH) for h in heads]
    qbs = [q_ref[:, sl].astype(BF16) for sl in sls]
    accs, carries = _sb_blocks(qbs, [kn_ref[:, sl].astype(BF16) for sl in sls],
                               [vn_ref[:, sl].astype(BF16) for sl in sls],
                               [jnp.zeros((n_new, 1), F32)] * SB_HEADS, later_ref[...], c < r)
    for j in reversed(range(ck_ref.shape[0] // (t * SB_HEADS))):
        rows = [pl.ds(j * t * SB_HEADS + h, t, stride=SB_HEADS) for h in heads]
        contribs, carries = _sb_blocks(qbs, [ck_ref[rows[h], :].astype(BF16) for h in heads],
                                       [cv_ref[rows[h], :].astype(BF16) for h in heads],
                                       carries, later_ref[...], None)
        accs = [a + cb for a, cb in zip(accs, contribs)]
    for sl, acc in zip(sls, accs):
        o_ref[:, sl] = acc


def _sb_sample(q, k_new, v_new, cache_k, cache_v, layer):
    bsz, n_new, _ = q.shape
    past = cache_k.shape[2]
    t = SB_TILE
    assert n_new <= t and past % t == 0
    pad = ((0, 0), (0, t - n_new), (0, 0))
    kn = jnp.pad(k_new, pad)
    vn = jnp.pad(v_new, pad)
    rows = lambda n: pl.BlockSpec((None, n, SB_WIDTH), lambda b: (b, 0, 0))
    depth = cache_k.shape[0]
    flat = (depth, bsz, past * SB_HEADS, SB_DH)
    cache = pl.BlockSpec((None, None, past * SB_HEADS, SB_DH), lambda b: (layer, b, 0, 0))
    return pl.pallas_call(
        _sb_sample_kernel,
        grid=(bsz,),
        in_specs=[rows(n_new), rows(t), rows(t), cache, cache],
        out_specs=rows(n_new),
        out_shape=jax.ShapeDtypeStruct(q.shape, F32),
        scratch_shapes=[pltpu.VMEM((t, t), BF16)],
        compiler_params=_params(1),
        name="sb_sample",
    )(q, kn, vn, cache_k.reshape(flat), cache_v.reshape(flat))


def _memkv_kernel(m_ref, g_ref, wk_ref, wv_ref, k_ref, v_ref):
    h = _rms(m_ref[...], g_ref[...]).astype(BF16)
    k_ref[...] = _dot(h, wk_ref[...])
    v_ref[...] = _dot(h, wv_ref[...])


def _memkv(mem, g, wk, wv):
    n = mem.shape[0]
    tm = 256
    out = jax.ShapeDtypeStruct((n, CA_WIDTH), F32)
    return pl.pallas_call(
        _memkv_kernel,
        grid=(n // tm,),
        in_specs=[pl.BlockSpec((tm, D_MODEL), lambda i: (i, 0)),
                  _const_spec((1, D_MODEL)),
                  _const_spec((D_MODEL, CA_WIDTH)),
                  _const_spec((D_MODEL, CA_WIDTH))],
        out_specs=[pl.BlockSpec((tm, CA_WIDTH), lambda i: (i, 0))] * 2,
        out_shape=[out, out],
        compiler_params=_params(1),
        name="memkv",
    )(mem, g, wk, wv)


def _post_kernel(x_ref, u_ref, va_ref, b_ref, wsp_ref, bsp_ref, ga_ref, gb_ref, wout_ref,
                 gc_ref, wcq_ref, mk_ref, mv_ref, wco_ref, gm_ref, wr_ref, br_ref,
                 x2_ref, h2_ref, idx_ref, gate_ref):
    tm = x_ref.shape[0]
    blk = wsp_ref.shape[1]

    r = lax.broadcasted_iota(jnp.int32, (blk, blk), 0)
    c = lax.broadcasted_iota(jnp.int32, (blk, blk), 1)
    tri = (c <= r).astype(F32)
    cols = []
    for g in range(A_GROUPS):
        w = (wsp_ref[g] * tri).astype(BF16)
        bias = bsp_ref[:, g:g + 1]
        sl = slice(g * A_DG, (g + 1) * A_DG)
        rows = []
        for ch in range(tm // blk):
            rs = slice(ch * blk, (ch + 1) * blk)
            mixed = _dot(w, va_ref[rs, sl].astype(BF16)) + bias
            rows.append(u_ref[rs, sl] * mixed)
        cols.append(rows[0] if len(rows) == 1 else jnp.concatenate(rows, axis=0))
    a_out = jnp.concatenate(cols, axis=1)

    a_n = _rms(a_out, ga_ref[...]).astype(BF16)
    b_n = _rms(b_ref[...], gb_ref[...]).astype(BF16)
    x1 = x_ref[...] + _dot(a_n, wout_ref[0:A_WIDTH, :]) + _dot(b_n, wout_ref[A_WIDTH:, :])

    hc = _rms(x1, gc_ref[...]).astype(BF16)
    q = _dot(hc, wcq_ref[...])
    heads = []
    for h in range(CA_HEADS):
        sl = slice(h * CA_DH, (h + 1) * CA_DH)
        s = _dot_nt(q[:, sl].astype(BF16), mk_ref[:, sl].astype(BF16)) * (CA_DH ** -0.5)
        p = jnp.exp(s - jnp.max(s, axis=-1, keepdims=True))
        p = p / jnp.sum(p, axis=-1, keepdims=True)
        heads.append(_dot(p.astype(BF16), mv_ref[:, sl].astype(BF16)))
    o = jnp.concatenate(heads, axis=1).astype(BF16)
    x2 = x1 + _dot(o, wco_ref[...])
    x2_ref[...] = x2

    h2 = _rms(x2, gm_ref[...])
    h2_ref[...] = h2
    hh, hl = _split_bf16(h2)
    wh, wl = _split_bf16(wr_ref[...])
    logits = _dot(hh, wh) + _dot(hh, wl) + _dot(hl, wh) + br_ref[...]
    lane = lax.broadcasted_iota(jnp.int32, (tm, LANES), 1).astype(F32)
    cur = jnp.where(lane < N_EXPERTS, logits, -jnp.inf)
    vals, idxs = [], []
    for _ in range(TOP_K):
        m = jnp.max(cur, axis=-1, keepdims=True)
        i = jnp.min(jnp.where(cur == m, lane, float(LANES)), axis=-1, keepdims=True)
        vals.append(m)
        idxs.append(i)
        cur = jnp.where(lane == i, -jnp.inf, cur)
    es = [jnp.exp(v - vals[0]) for v in vals]
    denom = es[0] + es[1] + es[2] + es[3]
    idx_out = jnp.zeros((tm, LANES), F32)
    gate_out = jnp.zeros((tm, LANES), F32)
    for k in range(TOP_K):
        idx_out = jnp.where(lane == k, idxs[k], idx_out)
        gate_out = jnp.where(lane == k, es[k] / denom, gate_out)
    idx_ref[...] = idx_out.astype(jnp.int32)
    gate_ref[...] = gate_out


def _post(x, u, va, b_out, mk, mv, lw, tm, rows_per_batch, blk):
    n = x.shape[0]
    tiles_per_batch = rows_per_batch // tm
    n_mem = mk.shape[0] // (n // rows_per_batch)
    row = lambda w: pl.BlockSpec((tm, w), lambda i: (i, 0))
    mem = pl.BlockSpec((n_mem, CA_WIDTH), lambda i: (i // tiles_per_batch, 0))
    wsp = lw["w_sp"][:, :blk, :blk]
    bsp = lw["b_sp"][:, :blk].T
    return pl.pallas_call(
        _post_kernel,
        grid=(n // tm,),
        in_specs=[row(D_MODEL), row(A_WIDTH), row(A_WIDTH), row(SB_WIDTH),
                  _const_spec((A_GROUPS, blk, blk)), _const_spec((blk, A_GROUPS)),
                  _const_spec((1, A_WIDTH)), _const_spec((1, SB_WIDTH)),
                  _const_spec((2 * A_WIDTH, D_MODEL)),
                  _const_spec((1, D_MODEL)), _const_spec((D_MODEL, CA_WIDTH)),
                  mem, mem, _const_spec((CA_WIDTH, D_MODEL)),
                  _const_spec((1, D_MODEL)), _const_spec((D_MODEL, LANES)), _const_spec((1, LANES))],
        out_specs=[row(D_MODEL), row(D_MODEL), row(LANES), row(LANES)],
        out_shape=[jax.ShapeDtypeStruct((n, D_MODEL), F32), jax.ShapeDtypeStruct((n, D_MODEL), F32),
                   jax.ShapeDtypeStruct((n, LANES), jnp.int32), jax.ShapeDtypeStruct((n, LANES), F32)],
        compiler_params=_params(1),
        name="post",
    )(x, u, va, b_out, wsp, bsp, lw["g_a"], lw["g_b"], lw["w_out"], lw["g_cross"], lw["w_cq"],
      mk, mv, lw["w_co"], lw["g_moe"], lw["w_router"], lw["b_router"])


def _gather_kernel(meta, tok_ref, next_tok_ref, h_hbm, xs_ref, buf, sem):
    i = pl.program_id(0)
    tg = xs_ref.shape[0]
    n_live = (meta[1] * MOE_PAD + tg - 1) // tg
    slot = i % 2

    def issue(ref, slot):
        def body(r, carry):
            pltpu.make_async_copy(h_hbm.at[pl.ds(ref[0, 0, r], 1), :],
                                  buf.at[slot, pl.ds(r, 1), :], sem.at[slot]).start()
            return carry
        lax.fori_loop(0, tg, body, 0, unroll=8)

    @pl.when(i == 0)
    def _():
        issue(tok_ref, 0)

    @pl.when(i + 1 < n_live)
    def _():
        issue(next_tok_ref, 1 - slot)

    @pl.when(i < n_live)
    def _():
        pltpu.make_async_copy(h_hbm.at[pl.ds(0, tg), :], buf.at[slot], sem.at[slot]).wait()
        xs_ref[...] = buf[slot].astype(BF16)

    @pl.when(i >= n_live)
    def _():
        xs_ref[...] = jnp.zeros(xs_ref.shape, BF16)


def _gather(meta, row_token, h_all):
    tg = GATHER_TILE
    p_rows = row_token.shape[0]
    n_tiles = p_rows // tg
    tok = row_token.reshape(n_tiles, 1, tg)
    grid_spec = pltpu.PrefetchScalarGridSpec(
        num_scalar_prefetch=1,
        grid=(n_tiles,),
        in_specs=[pl.BlockSpec((1, 1, tg), lambda i, m: (i, 0, 0), memory_space=pltpu.SMEM),
                  pl.BlockSpec((1, 1, tg), lambda i, m: (jnp.minimum(i + 1, n_tiles - 1), 0, 0),
                               memory_space=pltpu.SMEM),
                  pl.BlockSpec(memory_space=pl.ANY)],
        out_specs=pl.BlockSpec((tg, D_MODEL), lambda i, m: (i, 0)),
        scratch_shapes=[pltpu.VMEM((2, tg, D_MODEL), F32), pltpu.SemaphoreType.DMA((2,))],
    )
    return pl.pallas_call(
        _gather_kernel,
        grid_spec=grid_spec,
        out_shape=jax.ShapeDtypeStruct((p_rows, D_MODEL), BF16),
        compiler_params=_params(1),
        name="moe_gather",
    )(meta, tok, tok, h_all)


def _expert_kernel(item_e, item_start, item_units, meta,
                   xs_hbm, wg_ref, wl_ref, wd_ref, bg_ref, bl_ref, bd_ref,
                   y_hbm, x_buf, y_buf, sem_in, sem_out):
    s = pl.program_id(0)
    j = pl.program_id(1)
    units = item_units[s]
    start = item_start[s]
    pad = MOE_PAD

    def unit_rows(t):
        return pl.ds(pl.multiple_of(t * pad, pad), pad)

    def hbm_rows(t):
        return pl.ds(pl.multiple_of(start + t * pad, pad), pad)

    def copy_in(t):
        return pltpu.make_async_copy(xs_hbm.at[hbm_rows(t), :], x_buf.at[unit_rows(t), :], sem_in)

    def copy_out(t):
        return pltpu.make_async_copy(y_buf.at[unit_rows(t), :], y_hbm.at[hbm_rows(t), :], sem_out)

    def for_range(lo, hi, fn):
        def body(t, carry):
            fn(t)
            return carry
        lax.fori_loop(lo, hi, body, 0)

    @pl.when(jnp.logical_and(j == 0, units > 0))
    def _load():
        for_range(0, units, lambda t: copy_in(t).start())
        bias = jnp.broadcast_to(bd_ref[...], (pad, D_MODEL))

        def init(t):
            y_buf[unit_rows(t), :] = bias
        for_range(0, units, init)
        for_range(0, units, lambda t: copy_in(t).wait())

    def ffn(row0, m):
        wg = wg_ref[...].astype(BF16)
        wl = wl_ref[...].astype(BF16)
        wd = wd_ref[...].astype(BF16)
        for h in range(max(1, m // MOE_TILE)):
            mh = min(m, MOE_TILE)
            rows = pl.ds(pl.multiple_of(row0 + h * mh, pad), mh)
            xb = x_buf[rows, :]
            glu = jnp.minimum(_dot(xb, wg) + bg_ref[...], SWIGLU_LIMIT)
            lin = jnp.clip(_dot(xb, wl) + bl_ref[...], -SWIGLU_LIMIT, SWIGLU_LIMIT)
            act = glu * jax.nn.sigmoid(SWIGLU_ALPHA * glu) * (lin + 1.0)
            y_buf[rows, :] += _dot(act.astype(BF16), wd)

        @pl.when(j == MOE_NCH - 1)
        def _():
            for u in range(m // pad):
                copy_out(row0 // pad + u).start()

    @pl.when(units > 0)
    def _compute():
        big = MOE_BODY_ROWS
        n_big = units // (big // pad)
        for_range(0, n_big, lambda t: ffn(t * big, big))
        base = n_big * big
        rem = units * pad - base
        m = big // 2
        while m >= pad:
            take = (rem & m) != 0

            @pl.when(take)
            def _(base=base, m=m):
                ffn(base, m)

            base = base + jnp.where(take, m, 0)
            m //= 2

    @pl.when(jnp.logical_and(j == MOE_NCH - 1, units > 0))
    def _store():
        for_range(0, units, lambda t: copy_out(t).wait())

    @pl.when(jnp.logical_and(s == pl.num_programs(0) - 1, j == MOE_NCH - 1))
    def _fill_tail():
        y_buf[pl.ds(0, pad), :] = jnp.zeros((pad, D_MODEL), F32)

        def tail_copy(t):
            dst = y_hbm.at[pl.ds(pl.multiple_of(t * pad, pad), pad), :]
            return pltpu.make_async_copy(y_buf.at[pl.ds(0, pad), :], dst, sem_out)

        n_units = y_hbm.shape[0] // pad
        for_range(meta[1], n_units, lambda t: tail_copy(t).start())
        for_range(meta[1], n_units, lambda t: tail_copy(t).wait())


def _experts(item_e, item_start, item_units, meta, xs, w_gu, b_gu, w_dn, b_dn):
    n_slots = item_e.shape[0]
    p_rows = xs.shape[0]
    c = MOE_CHUNK

    def chunk(s, j, ie, ist, iu, m):
        return jnp.where(s < m[0], j, MOE_NCH - 1)

    grid_spec = pltpu.PrefetchScalarGridSpec(
        num_scalar_prefetch=4,
        grid=(n_slots, MOE_NCH),
        in_specs=[
            pl.BlockSpec(memory_space=pl.ANY),
            pl.BlockSpec((None, D_MODEL, c), lambda s, j, ie, *a: (ie[s], 0, chunk(s, j, ie, *a))),
            pl.BlockSpec((None, D_MODEL, c), lambda s, j, ie, *a: (ie[s], 0, MOE_NCH + chunk(s, j, ie, *a))),
            pl.BlockSpec((None, c, D_MODEL), lambda s, j, ie, *a: (ie[s], chunk(s, j, ie, *a), 0)),
            pl.BlockSpec((None, 1, c), lambda s, j, ie, *a: (ie[s], 0, chunk(s, j, ie, *a))),
            pl.BlockSpec((None, 1, c), lambda s, j, ie, *a: (ie[s], 0, MOE_NCH + chunk(s, j, ie, *a))),
            pl.BlockSpec((None, 1, D_MODEL), lambda s, j, ie, *a: (ie[s], 0, 0)),
        ],
        out_specs=pl.BlockSpec(memory_space=pl.ANY),
        scratch_shapes=[
            pltpu.VMEM((MOE_ITEM_ROWS, D_MODEL), BF16),
            pltpu.VMEM((MOE_ITEM_ROWS, D_MODEL), F32),
            pltpu.SemaphoreType.DMA,
            pltpu.SemaphoreType.DMA,
        ],
    )
    return pl.pallas_call(
        _expert_kernel,
        grid_spec=grid_spec,
        out_shape=jax.ShapeDtypeStruct((p_rows, D_MODEL), F32),
        compiler_params=_params(2),
        name="moe_experts",
    )(item_e, item_start, item_units, meta, xs, w_gu, w_gu, w_dn,
      b_gu.reshape(N_EXPERTS, 1, 2 * D_EXPERT), b_gu.reshape(N_EXPERTS, 1, 2 * D_EXPERT),
      b_dn.reshape(N_EXPERTS, 1, D_MODEL))


def _combine_kernel(dest_ref, next_dest_ref, x_ref, gate_ref, gf_ref, y_hbm, o_ref, buf, sem, *,
                    final_norm):
    i = pl.program_id(0)
    tc = x_ref.shape[0]
    slot = i % 2

    def issue(ref, slot):
        def body(r, carry):
            for k in range(TOP_K):
                pltpu.make_async_copy(y_hbm.at[pl.ds(ref[0, 0, r * TOP_K + k], 1), :],
                                      buf.at[slot, pl.ds(k * tc + r, 1), :], sem.at[slot]).start()
            return carry
        lax.fori_loop(0, tc, body, 0, unroll=4)

    @pl.when(i == 0)
    def _():
        issue(dest_ref, 0)

    @pl.when(i + 1 < pl.num_programs(0))
    def _():
        issue(next_dest_ref, 1 - slot)

    pltpu.make_async_copy(y_hbm.at[pl.ds(0, TOP_K * tc), :], buf.at[slot], sem.at[slot]).wait()
    gate = gate_ref[...]
    acc = x_ref[...]
    for k in range(TOP_K):
        acc = acc + gate[:, k:k + 1] * buf[slot, k * tc:(k + 1) * tc, :]
    o_ref[...] = _rms(acc, gf_ref[...]) if final_norm else acc


def _combine(dest, x2, gate, g_final, y, final_norm):
    tc = COMBINE_TILE
    n = x2.shape[0]
    n_tiles = n // tc
    dest = dest.reshape(n_tiles, 1, tc * TOP_K)
    return pl.pallas_call(
        functools.partial(_combine_kernel, final_norm=final_norm),
        grid=(n_tiles,),
        in_specs=[pl.BlockSpec((1, 1, tc * TOP_K), lambda i: (i, 0, 0), memory_space=pltpu.SMEM),
                  pl.BlockSpec((1, 1, tc * TOP_K), lambda i: (jnp.minimum(i + 1, n_tiles - 1), 0, 0),
                               memory_space=pltpu.SMEM),
                  pl.BlockSpec((tc, D_MODEL), lambda i: (i, 0)),
                  pl.BlockSpec((tc, LANES), lambda i: (i, 0)),
                  _const_spec((1, D_MODEL)),
                  pl.BlockSpec(memory_space=pl.ANY)],
        out_specs=pl.BlockSpec((tc, D_MODEL), lambda i: (i, 0)),
        out_shape=jax.ShapeDtypeStruct((n, D_MODEL), F32),
        scratch_shapes=[pltpu.VMEM((2, TOP_K * tc, D_MODEL), F32), pltpu.SemaphoreType.DMA((2,))],
        compiler_params=_params(1),
        name="moe_combine",
    )(dest, dest, x2, gate, g_final, y)


def _route(top_idx):
    n_tok = top_idx.shape[0]
    n_assign = n_tok * TOP_K
    pad, item_rows = MOE_PAD, MOE_ITEM_ROWS
    p_rows = -(-(n_assign + N_EXPERTS * (pad - 1)) // GATHER_TILE) * GATHER_TILE
    n_slots = N_EXPERTS + -(-p_rows // item_rows)
    experts = jnp.arange(N_EXPERTS, dtype=jnp.int32)

    flat_e = top_idx.reshape(-1)
    onehot = (flat_e[:, None] == experts[None, :]).astype(jnp.int32)
    running = jnp.cumsum(onehot, axis=0)
    rank = jnp.sum(running * onehot, axis=1) - 1
    counts = running[-1]
    padded = (counts + pad - 1) // pad * pad
    pad_end = jnp.cumsum(padded)
    pad_start = pad_end - padded
    dest = pad_start[flat_e] + rank
    row_token = jnp.zeros((p_rows,), jnp.int32).at[dest].set(
        jnp.arange(n_assign, dtype=jnp.int32) // TOP_K)

    items_per_e = (padded + item_rows - 1) // item_rows
    item_end = jnp.cumsum(items_per_e)
    n_items = item_end[-1]
    slot = jnp.arange(n_slots, dtype=jnp.int32)
    live = slot < n_items
    s_eff = jnp.minimum(slot, n_items - 1)
    e = jnp.minimum(jnp.searchsorted(item_end, s_eff, side="right"), N_EXPERTS - 1).astype(jnp.int32)
    local = s_eff - (item_end[e] - items_per_e[e])
    start = pad_start[e] + local * item_rows
    n_rows = jnp.clip(padded[e] - local * item_rows, 0, item_rows)
    units = jnp.where(live, n_rows // pad, 0).astype(jnp.int32)
    meta = jnp.stack([n_items, pad_end[-1] // pad]).astype(jnp.int32)
    return dest.astype(jnp.int32), row_token, e, start.astype(jnp.int32), units, meta


def _group_front(x, sb_fn, mk, mv, lw, tm_proj, tm_post, rows_per_batch, blk):
    u, va, q, k, vb = _proj(x, lw["g_mix"], lw["w_in"], tm_proj)
    b_out = sb_fn(q, k, vb)
    x2, h2, idx, gate = _post(x, u, va, b_out, mk, mv, lw, tm_post, rows_per_batch, blk)
    return x2, h2, idx, gate, k, vb, va


def kernel(x_prompt, x_sample, cache_sb_k, cache_sb_v, cache_mem_k, cache_mem_v, mem_prompt, g_mix, w_in, w_sp, b_sp, g_a, g_b, w_out, g_cross, g_mem, w_cq, w_ck, w_cv, w_co, g_moe, w_router, b_router, w_gu, b_gu, w_dn, b_dn, g_final):
    depth = g_mix.shape[0]
    bsz, seq, _ = x_prompt.shape
    dbsz, dseq, _ = x_sample.shape
    n_mem = mem_prompt.shape[1]
    n_p, n_s = bsz * seq, dbsz * dseq

    xp = x_prompt.reshape(n_p, D_MODEL)
    xs = x_sample.reshape(n_s, D_MODEL)
    mem = mem_prompt.reshape(bsz * n_mem, D_MODEL)
    outs = [[] for _ in range(7)]
    for l in range(depth):
        lw = {
            "g_mix": g_mix[l][None], "w_in": w_in[l].astype(BF16),
            "w_sp": w_sp[l], "b_sp": b_sp[l], "g_a": g_a[l][None], "g_b": g_b[l][None],
            "w_out": w_out[l].astype(BF16), "g_cross": g_cross[l][None],
            "w_cq": w_cq[l].astype(BF16), "w_co": w_co[l].astype(BF16), "g_moe": g_moe[l][None],
            "w_router": jnp.pad(w_router[l], ((0, 0), (0, LANES - N_EXPERTS))),
            "b_router": jnp.pad(b_router[l], (0, LANES - N_EXPERTS))[None],
        }
        mk, mv = _memkv(mem, g_mem[l][None], w_ck[l].astype(BF16), w_cv[l].astype(BF16))

        def sb_p(q, k, v):
            shp = (bsz, seq, SB_WIDTH)
            return _sb_prompt(q.reshape(shp), k.reshape(shp), v.reshape(shp)).reshape(n_p, SB_WIDTH)

        def sb_s(q, k, v):
            shp = (dbsz, dseq, SB_WIDTH)
            return _sb_sample(q.reshape(shp), k.reshape(shp), v.reshape(shp),
                              cache_sb_k, cache_sb_v, l).reshape(n_s, SB_WIDTH)

        xp2, hp, idx_p, gate_p, k_p, v_p, _ = _group_front(
            xp, sb_p, mk, mv, lw, 256, 256, seq, A_CHUNK)
        xs2, hs, idx_s, gate_s, k_s, v_s, va_s = _group_front(
            xs, sb_s, cache_mem_k[l].reshape(dbsz * n_mem, CA_WIDTH),
            cache_mem_v[l].reshape(dbsz * n_mem, CA_WIDTH), lw, n_s, dseq, dseq, min(dseq, A_CHUNK))

        top_idx = jnp.concatenate([idx_p[:, :TOP_K], idx_s[:, :TOP_K]], axis=0)
        dest, row_token, item_e, item_start, item_units, meta = _route(top_idx)
        x_sorted = _gather(meta, row_token, jnp.concatenate([hp, hs], axis=0))
        y = _experts(item_e, item_start, item_units, meta, x_sorted, w_gu[l], b_gu[l], w_dn[l], b_dn[l])
        last = l == depth - 1
        dest = dest.reshape(n_p + n_s, TOP_K)
        xp = _combine(dest[:n_p], xp2, gate_p, g_final[None], y, last)
        xs = _combine(dest[n_p:], xs2, gate_s, g_final[None], y, last)

        outs[0].append(k_p.reshape(bsz, seq, SB_HEADS, SB_DH))
        outs[1].append(v_p.reshape(bsz, seq, SB_HEADS, SB_DH))
        outs[2].append(mk.reshape(bsz, n_mem, CA_HEADS, CA_DH))
        outs[3].append(mv.reshape(bsz, n_mem, CA_HEADS, CA_DH))
        outs[4].append(k_s.reshape(dbsz, dseq, SB_HEADS, SB_DH))
        outs[5].append(v_s.reshape(dbsz, dseq, SB_HEADS, SB_DH))
        outs[6].append(va_s.reshape(dbsz, dseq, A_GROUPS, A_DG))

    stacked = [jnp.stack(o, axis=0) for o in outs]
    return (xp.reshape(bsz, seq, D_MODEL), xs.reshape(dbsz, dseq, D_MODEL), *stacked)
```

```python
import functools

import jax
import jax.numpy as jnp
from jax import lax
from jax.experimental import pallas as pl
from jax.experimental.pallas import tpu as pltpu

F32 = jnp.float32
BF16 = jnp.bfloat16

D_MODEL = 2048
A_GROUPS = 8
A_DG = 128
A_WIDTH = A_GROUPS * A_DG
A_CHUNK = 128
SB_HEADS = 8
SB_DH = 128
SB_WIDTH = SB_HEADS * SB_DH
IN_WIDTH = 2 * A_WIDTH + 3 * SB_WIDTH
N_SEG = IN_WIDTH // 1024
CA_HEADS = 4
CA_DH = 128
CA_WIDTH = CA_HEADS * CA_DH
N_EXPERTS = 32
TOP_K = 4
D_EXPERT = D_MODEL
SWIGLU_LIMIT = 7.0
SWIGLU_ALPHA = 1.702
RMS_EPS = 1e-6

LANES = 128
SB_TILE = 256
SB_HEADS_PER_STEP = 4
VMEM_LIMIT = 56 * 1024 * 1024

MOE_PAD = 128
MOE_TILE = 256
MOE_BODY_ROWS = 512
MOE_ITEM_ROWS = 1536
MOE_CHUNK = 512
MOE_NCH = D_EXPERT // MOE_CHUNK
DISPATCH_TILE = 128
COMBINE_TILE = 128


def _rms(x, g):
    return x * lax.rsqrt(jnp.mean(x * x, axis=-1, keepdims=True) + RMS_EPS) * g


def _dot(a, b):
    return jnp.dot(a, b, preferred_element_type=F32)


def _dot_nt(a, b):
    return lax.dot_general(a, b, (((1,), (1,)), ((), ())), preferred_element_type=F32)


def _split_bf16(x):
    hi = x.astype(BF16)
    lo = (x - hi.astype(F32)).astype(BF16)
    return hi, lo


def _pack_bf16_pairs(x):
    c = x.shape[1] // 2
    r = x.astype(BF16).astype(F32)
    lo = lax.bitcast_convert_type(r[:, :c], jnp.uint32) >> 16
    hi = lax.bitcast_convert_type(r[:, c:], jnp.uint32)
    return lax.bitcast_convert_type(hi | lo, F32)


def _unpack_bf16_pairs(w):
    bits = lax.bitcast_convert_type(w, jnp.uint32)
    lo = lax.bitcast_convert_type(bits << 16, F32).astype(BF16)
    hi = lax.bitcast_convert_type(bits & jnp.uint32(0xFFFF0000), F32).astype(BF16)
    return lo, hi


def _params(n_axes):
    return pltpu.CompilerParams(dimension_semantics=("arbitrary",) * n_axes,
                                vmem_limit_bytes=VMEM_LIMIT)


def _const_spec(shape):
    return pl.BlockSpec(shape, lambda *_: (0,) * len(shape), pipeline_mode=pl.Buffered(1))


def _proj_kernel(x_ref, g_ref, w_ref, *out_refs):
    h = _rms(x_ref[...], g_ref[...]).astype(BF16)
    for s, o_ref in enumerate(out_refs):
        o_ref[...] = _dot(h, w_ref[:, s * 1024:(s + 1) * 1024])


def _proj(x, g, w, tm):
    n = x.shape[0]
    seg = jax.ShapeDtypeStruct((n, 1024), F32)
    return pl.pallas_call(
        _proj_kernel,
        grid=(n // tm,),
        in_specs=[pl.BlockSpec((tm, D_MODEL), lambda i: (i, 0)),
                  _const_spec((1, D_MODEL)),
                  _const_spec((D_MODEL, IN_WIDTH))],
        out_specs=[pl.BlockSpec((tm, 1024), lambda i: (i, 0))] * N_SEG,
        out_shape=[seg] * N_SEG,
        compiler_params=_params(1),
        name="proj",
    )(x, g, w)


def _log_sigmoid_neg(z):
    return -(jnp.maximum(z, 0.0) + jnp.log(1.0 + jnp.exp(-jnp.abs(z))))


def _sb_blocks(qbs, kbs, vbs, carries, later, mask):
    n = range(len(qbs))
    zs = [_dot_nt(qbs[i], kbs[i]) * (SB_DH ** -0.5) for i in n]
    ls = [_log_sigmoid_neg(z) for z in zs]
    if mask is not None:
        ls = [jnp.where(mask, l, 0.0) for l in ls]
    splits = [_split_bf16(l) for l in ls]
    css = [_dot(hi, later) + _dot(lo, later) for hi, lo in splits]
    aas = [jnp.exp(ls[i] + zs[i] + css[i] + carries[i]) for i in n]
    if mask is not None:
        aas = [jnp.where(mask, a, 0.0) for a in aas]
    contribs = [_dot(aas[i].astype(BF16), vbs[i]) for i in n]
    new = [carries[i] + css[i][:, 0:1] + ls[i][:, 0:1] for i in n]
    return contribs, new


def _sb_block(qb, kb, vb, carry, later, mask):
    contribs, new = _sb_blocks([qb], [kb], [vb], [carry], later, mask)
    return contribs[0], new[0]


def _later_matrix(n):
    r = lax.broadcasted_iota(jnp.int32, (n, n), 0)
    c = lax.broadcasted_iota(jnp.int32, (n, n), 1)
    return (r > c).astype(BF16)


def _sb_prompt_kernel(q_ref, k_ref, v_ref, o_ref, later_ref, carry_ref):
    qi = pl.program_id(2)
    t = SB_TILE
    later_ref[...] = _later_matrix(t)
    r = lax.broadcasted_iota(jnp.int32, (t, t), 0)
    c = lax.broadcasted_iota(jnp.int32, (t, t), 1)
    heads = [slice(h * SB_DH, (h + 1) * SB_DH) for h in range(SB_HEADS_PER_STEP)]

    def sweep(j, first):
        off = pl.multiple_of(j * t, t)
        rows = pl.ds(off, t)
        carries = [jnp.zeros((t, 1), F32) if first else carry_ref[h] for h in range(len(heads))]
        contribs, carries = _sb_blocks(
            [q_ref[:, sl].astype(BF16) for sl in heads],
            [k_ref[rows, sl].astype(BF16) for sl in heads],
            [v_ref[rows, sl].astype(BF16) for sl in heads],
            carries, later_ref[...], (c < r) if first else None)
        for h, sl in enumerate(heads):
            carry_ref[h] = carries[h]
            if first:
                o_ref[:, sl] = contribs[h]
            else:
                o_ref[:, sl] += contribs[h]

    sweep(qi, True)

    def body(step, carry):
        sweep(qi - 1 - step, False)
        return carry

    lax.fori_loop(0, qi, body, 0)


def _sb_prompt(q, k, v):
    bsz, t_len, _ = q.shape
    t = SB_TILE
    w = SB_HEADS_PER_STEP * SB_DH
    kv_spec = pl.BlockSpec((None, t_len, w), lambda b, h, i: (b, 0, h))
    return pl.pallas_call(
        _sb_prompt_kernel,
        grid=(bsz, SB_HEADS // SB_HEADS_PER_STEP, t_len // t),
        in_specs=[pl.BlockSpec((None, t, w), lambda b, h, i: (b, i, h)), kv_spec, kv_spec],
        out_specs=pl.BlockSpec((None, t, w), lambda b, h, i: (b, i, h)),
        out_shape=jax.ShapeDtypeStruct(q.shape, F32),
        scratch_shapes=[pltpu.VMEM((t, t), BF16), pltpu.VMEM((SB_HEADS_PER_STEP, t, 1), F32)],
        compiler_params=_params(3),
        name="sb_prompt",
    )(q, k, v)


def _sb_sample_kernel(q_ref, kn_ref, vn_ref, ck_ref, cv_ref, o_ref, later_ref):
    t = SB_TILE
    n_new = q_ref.shape[0]
    later_ref[...] = _later_matrix(t)
    r = lax.broadcasted_iota(jnp.int32, (n_new, t), 0)
    c = lax.broadcasted_iota(jnp.int32, (n_new, t), 1)
    heads = range(SB_HEADS)
    sls = [slice(h * SB_DH, (h + 1) * SB_DH) for h in heads]
    qbs = [q_ref[:, sl].astype(BF16) for sl in sls]
    accs, carries = _sb_blocks(qbs, [kn_ref[:, sl].astype(BF16) for sl in sls],
                               [vn_ref[:, sl].astype(BF16) for sl in sls],
                               [jnp.zeros((n_new, 1), F32)] * SB_HEADS, later_ref[...], c < r)
    for j in reversed(range(ck_ref.shape[0] // (t * SB_HEADS))):
        rows = [pl.ds(j * t * SB_HEADS + h, t, stride=SB_HEADS) for h in heads]
        contribs, carries = _sb_blocks(qbs, [ck_ref[rows[h], :].astype(BF16) for h in heads],
                                       [cv_ref[rows[h], :].astype(BF16) for h in heads],
                                       carries, later_ref[...], None)
        accs = [a + cb for a, cb in zip(accs, contribs)]
    for sl, acc in zip(sls, accs):
        o_ref[:, sl] = acc


def _sb_sample(q, k_new, v_new, cache_k, cache_v, layer):
    bsz, n_new, _ = q.shape
    past = cache_k.shape[2]
    t = SB_TILE
    assert n_new <= t and past % t == 0
    pad = ((0, 0), (0, t - n_new), (0, 0))
    kn = jnp.pad(k_new, pad)
    vn = jnp.pad(v_new, pad)
    rows = lambda n: pl.BlockSpec((None, n, SB_WIDTH), lambda b: (b, 0, 0))
    depth = cache_k.shape[0]
    flat = (depth, bsz, past * SB_HEADS, SB_DH)
    cache = pl.BlockSpec((None, None, past * SB_HEADS, SB_DH), lambda b: (layer, b, 0, 0))
    return pl.pallas_call(
        _sb_sample_kernel,
        grid=(bsz,),
        in_specs=[rows(n_new), rows(t), rows(t), cache, cache],
        out_specs=rows(n_new),
        out_shape=jax.ShapeDtypeStruct(q.shape, F32),
        scratch_shapes=[pltpu.VMEM((t, t), BF16)],
        compiler_params=_params(1),
        name="sb_sample",
    )(q, kn, vn, cache_k.reshape(flat), cache_v.reshape(flat))


def _memkv_kernel(m_ref, g_ref, wk_ref, wv_ref, k_ref, v_ref):
    h = _rms(m_ref[...], g_ref[...]).astype(BF16)
    k_ref[...] = _dot(h, wk_ref[...])
    v_ref[...] = _dot(h, wv_ref[...])


def _memkv(mem, g, wk, wv):
    n = mem.shape[0]
    tm = 256
    out = jax.ShapeDtypeStruct((n, CA_WIDTH), F32)
    return pl.pallas_call(
        _memkv_kernel,
        grid=(n // tm,),
        in_specs=[pl.BlockSpec((tm, D_MODEL), lambda i: (i, 0)),
                  _const_spec((1, D_MODEL)),
                  _const_spec((D_MODEL, CA_WIDTH)),
                  _const_spec((D_MODEL, CA_WIDTH))],
        out_specs=[pl.BlockSpec((tm, CA_WIDTH), lambda i: (i, 0))] * 2,
        out_shape=[out, out],
        compiler_params=_params(1),
        name="memkv",
    )(mem, g, wk, wv)


def _post_kernel(x_ref, u_ref, va_ref, b_ref, wsp_ref, bsp_ref, ga_ref, gb_ref, wout_ref,
                 gc_ref, wcq_ref, mk_ref, mv_ref, wco_ref, gm_ref, wr_ref, br_ref,
                 x2_ref, h2_ref, idx_ref, gate_ref):
    tm = x_ref.shape[0]
    blk = wsp_ref.shape[1]

    r = lax.broadcasted_iota(jnp.int32, (blk, blk), 0)
    c = lax.broadcasted_iota(jnp.int32, (blk, blk), 1)
    tri = (c <= r).astype(F32)
    cols = []
    for g in range(A_GROUPS):
        w = (wsp_ref[g] * tri).astype(BF16)
        bias = bsp_ref[:, g:g + 1]
        sl = slice(g * A_DG, (g + 1) * A_DG)
        rows = []
        for ch in range(tm // blk):
            rs = slice(ch * blk, (ch + 1) * blk)
            mixed = _dot(w, va_ref[rs, sl].astype(BF16)) + bias
            rows.append(u_ref[rs, sl] * mixed)
        cols.append(rows[0] if len(rows) == 1 else jnp.concatenate(rows, axis=0))
    a_out = jnp.concatenate(cols, axis=1)

    a_n = _rms(a_out, ga_ref[...]).astype(BF16)
    b_n = _rms(b_ref[...], gb_ref[...]).astype(BF16)
    x1 = x_ref[...] + _dot(a_n, wout_ref[0:A_WIDTH, :]) + _dot(b_n, wout_ref[A_WIDTH:, :])

    hc = _rms(x1, gc_ref[...]).astype(BF16)
    q = _dot(hc, wcq_ref[...])
    heads = []
    for h in range(CA_HEADS):
        sl = slice(h * CA_DH, (h + 1) * CA_DH)
        s = _dot_nt(q[:, sl].astype(BF16), mk_ref[:, sl].astype(BF16)) * (CA_DH ** -0.5)
        p = jnp.exp(s - jnp.max(s, axis=-1, keepdims=True))
        p = p / jnp.sum(p, axis=-1, keepdims=True)
        heads.append(_dot(p.astype(BF16), mv_ref[:, sl].astype(BF16)))
    o = jnp.concatenate(heads, axis=1).astype(BF16)
    x2 = x1 + _dot(o, wco_ref[...])
    x2_ref[...] = x2

    h2 = _rms(x2, gm_ref[...])
    h2_ref[...] = _pack_bf16_pairs(h2)
    hh, hl = _split_bf16(h2)
    wh, wl = _split_bf16(wr_ref[...])
    logits = _dot(hh, wh) + _dot(hh, wl) + _dot(hl, wh) + br_ref[...]
    lane = lax.broadcasted_iota(jnp.int32, (tm, LANES), 1).astype(F32)
    cur = jnp.where(lane < N_EXPERTS, logits, -jnp.inf)
    vals, idxs = [], []
    for _ in range(TOP_K):
        m = jnp.max(cur, axis=-1, keepdims=True)
        i = jnp.min(jnp.where(cur == m, lane, float(LANES)), axis=-1, keepdims=True)
        vals.append(m)
        idxs.append(i)
        cur = jnp.where(lane == i, -jnp.inf, cur)
    es = [jnp.exp(v - vals[0]) for v in vals]
    denom = es[0] + es[1] + es[2] + es[3]
    idx_out = jnp.zeros((tm, LANES), F32)
    gate_out = jnp.zeros((tm, LANES), F32)
    for k in range(TOP_K):
        idx_out = jnp.where(lane == k, idxs[k], idx_out)
        gate_out = jnp.where(lane == k, es[k] / denom, gate_out)
    idx_ref[...] = idx_out.astype(jnp.int32)
    gate_ref[...] = gate_out


def _post(x, u, va, b_out, mk, mv, lw, tm, rows_per_batch, blk):
    n = x.shape[0]
    tiles_per_batch = rows_per_batch // tm
    n_mem = mk.shape[0] // (n // rows_per_batch)
    row = lambda w: pl.BlockSpec((tm, w), lambda i: (i, 0))
    mem = pl.BlockSpec((n_mem, CA_WIDTH), lambda i: (i // tiles_per_batch, 0))
    wsp = lw["w_sp"][:, :blk, :blk]
    bsp = lw["b_sp"][:, :blk].T
    return pl.pallas_call(
        _post_kernel,
        grid=(n // tm,),
        in_specs=[row(D_MODEL), row(A_WIDTH), row(A_WIDTH), row(SB_WIDTH),
                  _const_spec((A_GROUPS, blk, blk)), _const_spec((blk, A_GROUPS)),
                  _const_spec((1, A_WIDTH)), _const_spec((1, SB_WIDTH)),
                  _const_spec((2 * A_WIDTH, D_MODEL)),
                  _const_spec((1, D_MODEL)), _const_spec((D_MODEL, CA_WIDTH)),
                  mem, mem, _const_spec((CA_WIDTH, D_MODEL)),
                  _const_spec((1, D_MODEL)), _const_spec((D_MODEL, LANES)), _const_spec((1, LANES))],
        out_specs=[row(D_MODEL), row(D_MODEL // 2), row(LANES), row(LANES)],
        out_shape=[jax.ShapeDtypeStruct((n, D_MODEL), F32), jax.ShapeDtypeStruct((n, D_MODEL // 2), F32),
                   jax.ShapeDtypeStruct((n, LANES), jnp.int32), jax.ShapeDtypeStruct((n, LANES), F32)],
        compiler_params=_params(1),
        name="post",
    )(x, u, va, b_out, wsp, bsp, lw["g_a"], lw["g_b"], lw["w_out"], lw["g_cross"], lw["w_cq"],
      mk, mv, lw["w_co"], lw["g_moe"], lw["w_router"], lw["b_router"])


def _dispatch_kernel(last_unit, meta, dest_ref, hp_ref, hs_ref, xs_hbm, stage, zero, sem, sem_zero, *,
                     n_prompt_tiles, n_tiles):
    i = pl.program_id(0)
    td = stage.shape[1]
    pad = MOE_PAD
    slot = i % 2

    def zero_copy(row0):
        return pltpu.make_async_copy(zero, xs_hbm.at[pl.ds(pl.multiple_of(row0, pad), pad), :], sem_zero)

    @pl.when(i == 0)
    def _():
        zero[...] = jnp.zeros(zero.shape, F32)

        def fill(wait):
            def act(copy):
                if wait:
                    copy.wait()
                else:
                    copy.start()

            def expert_body(e, carry):
                @pl.when(last_unit[e] >= 0)
                def _():
                    act(zero_copy(last_unit[e]))
                return carry

            def tail_body(t, carry):
                act(zero_copy(t * pad))
                return carry

            lax.fori_loop(0, N_EXPERTS, expert_body, 0)
            lax.fori_loop(meta[1], xs_hbm.shape[0] // pad, tail_body, 0)

        fill(False)
        fill(True)

    def wait_slot(slot):
        for _ in range(TOP_K):
            pltpu.make_async_copy(stage.at[slot], xs_hbm.at[pl.ds(0, td), :], sem.at[slot]).wait()

    @pl.when(i >= 2)
    def _():
        wait_slot(slot)

    @pl.when(i < n_prompt_tiles)
    def _():
        stage[slot] = hp_ref[...]

    @pl.when(i >= n_prompt_tiles)
    def _():
        stage[slot] = hs_ref[...]

    def issue(r, carry):
        for k in range(TOP_K):
            pltpu.make_async_copy(stage.at[slot, pl.ds(r, 1), :],
                                  xs_hbm.at[pl.ds(dest_ref[0, 0, r * TOP_K + k], 1), :],
                                  sem.at[slot]).start()
        return carry

    lax.fori_loop(0, td, issue, 0, unroll=4)

    @pl.when(i == n_tiles - 1)
    def _():
        wait_slot(slot)
        if n_tiles > 1:
            wait_slot(1 - slot)


def _dispatch(last_unit, meta, dest, h_prompt, h_sample, p_rows):
    td = DISPATCH_TILE
    n_p, width = h_prompt.shape
    n_s = h_sample.shape[0]
    assert n_p % td == 0 and n_s % td == 0
    n_prompt_tiles, n_tiles = n_p // td, (n_p + n_s) // td
    grid_spec = pltpu.PrefetchScalarGridSpec(
        num_scalar_prefetch=2,
        grid=(n_tiles,),
        in_specs=[pl.BlockSpec((1, 1, td * TOP_K), lambda i, *_: (i, 0, 0), memory_space=pltpu.SMEM),
                  pl.BlockSpec((td, width), lambda i, *_: (jnp.minimum(i, n_prompt_tiles - 1), 0)),
                  pl.BlockSpec((td, width), lambda i, *_: (jnp.maximum(i - n_prompt_tiles, 0), 0))],
        out_specs=pl.BlockSpec(memory_space=pl.ANY),
        scratch_shapes=[pltpu.VMEM((2, td, width), F32), pltpu.VMEM((MOE_PAD, width), F32),
                        pltpu.SemaphoreType.DMA((2,)), pltpu.SemaphoreType.DMA],
    )
    return pl.pallas_call(
        functools.partial(_dispatch_kernel, n_prompt_tiles=n_prompt_tiles, n_tiles=n_tiles),
        grid_spec=grid_spec,
        out_shape=jax.ShapeDtypeStruct((p_rows, width), F32),
        compiler_params=_params(1),
        name="moe_dispatch",
    )(last_unit, meta, dest.reshape(n_tiles, 1, td * TOP_K), h_prompt, h_sample)


def _expert_kernel(item_e, item_start, item_units, meta,
                   xs_hbm, wg_ref, wl_ref, wd_ref, bg_ref, bl_ref, bd_ref,
                   y_hbm, x_buf, y_buf, sem_in, sem_out):
    s = pl.program_id(0)
    j = pl.program_id(1)
    units = item_units[s]
    start = item_start[s]
    pad = MOE_PAD

    def unit_rows(t):
        return pl.ds(pl.multiple_of(t * pad, pad), pad)

    def hbm_rows(t):
        return pl.ds(pl.multiple_of(start + t * pad, pad), pad)

    half = D_MODEL // 2

    def copy_in(t):
        return pltpu.make_async_copy(xs_hbm.at[hbm_rows(t), :], y_buf.at[unit_rows(t), pl.ds(0, half)], sem_in)

    def copy_out(t):
        return pltpu.make_async_copy(y_buf.at[unit_rows(t), :], y_hbm.at[hbm_rows(t), :], sem_out)

    def for_range(lo, hi, fn):
        def body(t, carry):
            fn(t)
            return carry
        lax.fori_loop(lo, hi, body, 0)

    @pl.when(jnp.logical_and(j == 0, units > 0))
    def _load():
        for_range(0, units, lambda t: copy_in(t).start())
        for_range(0, units, lambda t: copy_in(t).wait())
        bias = jnp.broadcast_to(bd_ref[...], (pad, D_MODEL))

        def unpack_and_init(t):
            lo, hi = _unpack_bf16_pairs(y_buf[unit_rows(t), pl.ds(0, half)])
            x_buf[unit_rows(t), pl.ds(0, half)] = lo
            x_buf[unit_rows(t), pl.ds(half, half)] = hi
            y_buf[unit_rows(t), :] = bias
        for_range(0, units, unpack_and_init)

    def ffn(row0, m):
        wg = wg_ref[...].astype(BF16)
        wl = wl_ref[...].astype(BF16)
        wd = wd_ref[...].astype(BF16)
        for h in range(max(1, m // MOE_TILE)):
            mh = min(m, MOE_TILE)
            rows = pl.ds(pl.multiple_of(row0 + h * mh, pad), mh)
            xb = x_buf[rows, :]
            glu = jnp.minimum(_dot(xb, wg) + bg_ref[...], SWIGLU_LIMIT)
            lin = jnp.clip(_dot(xb, wl) + bl_ref[...], -SWIGLU_LIMIT, SWIGLU_LIMIT)
            act = glu * jax.nn.sigmoid(SWIGLU_ALPHA * glu) * (lin + 1.0)
            y_buf[rows, :] += _dot(act.astype(BF16), wd)

        @pl.when(j == MOE_NCH - 1)
        def _():
            for u in range(m // pad):
                copy_out(row0 // pad + u).start()

    @pl.when(units > 0)
    def _compute():
        big = MOE_BODY_ROWS
        n_big = units // (big // pad)
        for_range(0, n_big, lambda t: ffn(t * big, big))
        base = n_big * big
        rem = units * pad - base
        m = big // 2
        while m >= pad:
            take = (rem & m) != 0

            @pl.when(take)
            def _(base=base, m=m):
                ffn(base, m)

            base = base + jnp.where(take, m, 0)
            m //= 2

    @pl.when(jnp.logical_and(j == MOE_NCH - 1, units > 0))
    def _store():
        for_range(0, units, lambda t: copy_out(t).wait())

    @pl.when(jnp.logical_and(s == pl.num_programs(0) - 1, j == MOE_NCH - 1))
    def _fill_tail():
        y_buf[pl.ds(0, pad), :] = jnp.zeros((pad, D_MODEL), F32)

        def tail_copy(t):
            dst = y_hbm.at[pl.ds(pl.multiple_of(t * pad, pad), pad), :]
            return pltpu.make_async_copy(y_buf.at[pl.ds(0, pad), :], dst, sem_out)

        n_units = y_hbm.shape[0] // pad
        for_range(meta[1], n_units, lambda t: tail_copy(t).start())
        for_range(meta[1], n_units, lambda t: tail_copy(t).wait())


def _experts(item_e, item_start, item_units, meta, xs, w_gu, b_gu, w_dn, b_dn):
    n_slots = item_e.shape[0]
    p_rows = xs.shape[0]
    c = MOE_CHUNK

    def chunk(s, j, ie, ist, iu, m):
        return jnp.where(s < m[0], j, MOE_NCH - 1)

    grid_spec = pltpu.PrefetchScalarGridSpec(
        num_scalar_prefetch=4,
        grid=(n_slots, MOE_NCH),
        in_specs=[
            pl.BlockSpec(memory_space=pl.ANY),
            pl.BlockSpec((None, D_MODEL, c), lambda s, j, ie, *a: (ie[s], 0, chunk(s, j, ie, *a))),
            pl.BlockSpec((None, D_MODEL, c), lambda s, j, ie, *a: (ie[s], 0, MOE_NCH + chunk(s, j, ie, *a))),
            pl.BlockSpec((None, c, D_MODEL), lambda s, j, ie, *a: (ie[s], chunk(s, j, ie, *a), 0)),
            pl.BlockSpec((None, 1, c), lambda s, j, ie, *a: (ie[s], 0, chunk(s, j, ie, *a))),
            pl.BlockSpec((None, 1, c), lambda s, j, ie, *a: (ie[s], 0, MOE_NCH + chunk(s, j, ie, *a))),
            pl.BlockSpec((None, 1, D_MODEL), lambda s, j, ie, *a: (ie[s], 0, 0)),
        ],
        out_specs=pl.BlockSpec(memory_space=pl.ANY),
        scratch_shapes=[
            pltpu.VMEM((MOE_ITEM_ROWS, D_MODEL), BF16),
            pltpu.VMEM((MOE_ITEM_ROWS, D_MODEL), F32),
            pltpu.SemaphoreType.DMA,
            pltpu.SemaphoreType.DMA,
        ],
    )
    return pl.pallas_call(
        _expert_kernel,
        grid_spec=grid_spec,
        out_shape=jax.ShapeDtypeStruct((p_rows, D_MODEL), F32),
        compiler_params=_params(2),
        name="moe_experts",
    )(item_e, item_start, item_units, meta, xs, w_gu, w_gu, w_dn,
      b_gu.reshape(N_EXPERTS, 1, 2 * D_EXPERT), b_gu.reshape(N_EXPERTS, 1, 2 * D_EXPERT),
      b_dn.reshape(N_EXPERTS, 1, D_MODEL))


def _combine_kernel(dest_ref, next_dest_ref, x_ref, gate_ref, gf_ref, y_hbm, o_ref, buf, sem, *,
                    final_norm):
    i = pl.program_id(0)
    tc = x_ref.shape[0]
    slot = i % 2

    def issue(ref, slot):
        def body(r, carry):
            for k in range(TOP_K):
                pltpu.make_async_copy(y_hbm.at[pl.ds(ref[0, 0, r * TOP_K + k], 1), :],
                                      buf.at[slot, pl.ds(k * tc + r, 1), :], sem.at[slot]).start()
            return carry
        lax.fori_loop(0, tc, body, 0, unroll=4)

    @pl.when(i == 0)
    def _():
        issue(dest_ref, 0)

    @pl.when(i + 1 < pl.num_programs(0))
    def _():
        issue(next_dest_ref, 1 - slot)

    pltpu.make_async_copy(y_hbm.at[pl.ds(0, TOP_K * tc), :], buf.at[slot], sem.at[slot]).wait()
    gate = gate_ref[...]
    acc = x_ref[...]
    for k in range(TOP_K):
        acc = acc + gate[:, k:k + 1] * buf[slot, k * tc:(k + 1) * tc, :]
    o_ref[...] = _rms(acc, gf_ref[...]) if final_norm else acc


def _combine(dest, x2, gate, g_final, y, final_norm):
    tc = COMBINE_TILE
    n = x2.shape[0]
    n_tiles = n // tc
    dest = dest.reshape(n_tiles, 1, tc * TOP_K)
    return pl.pallas_call(
        functools.partial(_combine_kernel, final_norm=final_norm),
        grid=(n_tiles,),
        in_specs=[pl.BlockSpec((1, 1, tc * TOP_K), lambda i: (i, 0, 0), memory_space=pltpu.SMEM),
                  pl.BlockSpec((1, 1, tc * TOP_K), lambda i: (jnp.minimum(i + 1, n_tiles - 1), 0, 0),
                               memory_space=pltpu.SMEM),
                  pl.BlockSpec((tc, D_MODEL), lambda i: (i, 0)),
                  pl.BlockSpec((tc, LANES), lambda i: (i, 0)),
                  _const_spec((1, D_MODEL)),
                  pl.BlockSpec(memory_space=pl.ANY)],
        out_specs=pl.BlockSpec((tc, D_MODEL), lambda i: (i, 0)),
        out_shape=jax.ShapeDtypeStruct((n, D_MODEL), F32),
        scratch_shapes=[pltpu.VMEM((2, TOP_K * tc, D_MODEL), F32), pltpu.SemaphoreType.DMA((2,))],
        compiler_params=_params(1),
        name="moe_combine",
    )(dest, dest, x2, gate, g_final, y)


def _route(top_idx):
    n_tok = top_idx.shape[0]
    n_assign = n_tok * TOP_K
    pad, item_rows = MOE_PAD, MOE_ITEM_ROWS
    p_rows = -(-(n_assign + N_EXPERTS * (pad - 1)) // pad) * pad
    n_slots = N_EXPERTS + -(-p_rows // item_rows)
    experts = jnp.arange(N_EXPERTS, dtype=jnp.int32)

    flat_e = top_idx.reshape(-1)
    onehot = (flat_e[:, None] == experts[None, :]).astype(jnp.int32)
    running = jnp.cumsum(onehot, axis=0)
    rank = jnp.sum(running * onehot, axis=1) - 1
    counts = running[-1]
    padded = (counts + pad - 1) // pad * pad
    pad_end = jnp.cumsum(padded)
    pad_start = pad_end - padded
    dest = pad_start[flat_e] + rank
    last_unit = jnp.where(padded > 0, pad_end - pad, -1).astype(jnp.int32)

    items_per_e = (padded + item_rows - 1) // item_rows
    item_end = jnp.cumsum(items_per_e)
    n_items = item_end[-1]
    slot = jnp.arange(n_slots, dtype=jnp.int32)
    live = slot < n_items
    s_eff = jnp.minimum(slot, n_items - 1)
    e = jnp.minimum(jnp.searchsorted(item_end, s_eff, side="right"), N_EXPERTS - 1).astype(jnp.int32)
    local = s_eff - (item_end[e] - items_per_e[e])
    start = pad_start[e] + local * item_rows
    n_rows = jnp.clip(padded[e] - local * item_rows, 0, item_rows)
    units = jnp.where(live, n_rows // pad, 0).astype(jnp.int32)
    meta = jnp.stack([n_items, pad_end[-1] // pad]).astype(jnp.int32)
    return dest.astype(jnp.int32), last_unit, p_rows, e, start.astype(jnp.int32), units, meta


def _group_front(x, sb_fn, mk, mv, lw, tm_proj, tm_post, rows_per_batch, blk):
    u, va, q, k, vb = _proj(x, lw["g_mix"], lw["w_in"], tm_proj)
    b_out = sb_fn(q, k, vb)
    x2, h2, idx, gate = _post(x, u, va, b_out, mk, mv, lw, tm_post, rows_per_batch, blk)
    return x2, h2, idx, gate, k, vb, va


def kernel(x_prompt, x_sample, cache_sb_k, cache_sb_v, cache_mem_k, cache_mem_v, mem_prompt, g_mix, w_in, w_sp, b_sp, g_a, g_b, w_out, g_cross, g_mem, w_cq, w_ck, w_cv, w_co, g_moe, w_router, b_router, w_gu, b_gu, w_dn, b_dn, g_final):
    depth = g_mix.shape[0]
    bsz, seq, _ = x_prompt.shape
    dbsz, dseq, _ = x_sample.shape
    n_mem = mem_prompt.shape[1]
    n_p, n_s = bsz * seq, dbsz * dseq

    xp = x_prompt.reshape(n_p, D_MODEL)
    xs = x_sample.reshape(n_s, D_MODEL)
    mem = mem_prompt.reshape(bsz * n_mem, D_MODEL)
    outs = [[] for _ in range(7)]
    for l in range(depth):
        lw = {
            "g_mix": g_mix[l][None], "w_in": w_in[l].astype(BF16),
            "w_sp": w_sp[l], "b_sp": b_sp[l], "g_a": g_a[l][None], "g_b": g_b[l][None],
            "w_out": w_out[l].astype(BF16), "g_cross": g_cross[l][None],
            "w_cq": w_cq[l].astype(BF16), "w_co": w_co[l].astype(BF16), "g_moe": g_moe[l][None],
            "w_router": jnp.pad(w_router[l], ((0, 0), (0, LANES - N_EXPERTS))),
            "b_router": jnp.pad(b_router[l], (0, LANES - N_EXPERTS))[None],
        }
        mk, mv = _memkv(mem, g_mem[l][None], w_ck[l].astype(BF16), w_cv[l].astype(BF16))

        def sb_p(q, k, v):
            shp = (bsz, seq, SB_WIDTH)
            return _sb_prompt(q.reshape(shp), k.reshape(shp), v.reshape(shp)).reshape(n_p, SB_WIDTH)

        def sb_s(q, k, v):
            shp = (dbsz, dseq, SB_WIDTH)
            return _sb_sample(q.reshape(shp), k.reshape(shp), v.reshape(shp),
                              cache_sb_k, cache_sb_v, l).reshape(n_s, SB_WIDTH)

        xp2, hp, idx_p, gate_p, k_p, v_p, _ = _group_front(
            xp, sb_p, mk, mv, lw, 256, 256, seq, A_CHUNK)
        xs2, hs, idx_s, gate_s, k_s, v_s, va_s = _group_front(
            xs, sb_s, cache_mem_k[l].reshape(dbsz * n_mem, CA_WIDTH),
            cache_mem_v[l].reshape(dbsz * n_mem, CA_WIDTH), lw, n_s, dseq, dseq, min(dseq, A_CHUNK))

        top_idx = jnp.concatenate([idx_p[:, :TOP_K], idx_s[:, :TOP_K]], axis=0)
        dest, last_unit, p_rows, item_e, item_start, item_units, meta = _route(top_idx)
        x_sorted = _dispatch(last_unit, meta, dest, hp, hs, p_rows)
        y = _experts(item_e, item_start, item_units, meta, x_sorted, w_gu[l], b_gu[l], w_dn[l], b_dn[l])
        last = l == depth - 1
        dest = dest.reshape(n_p + n_s, TOP_K)
        xp = _combine(dest[:n_p], xp2, gate_p, g_final[None], y, last)
        xs = _combine(dest[n_p:], xs2, gate_s, g_final[None], y, last)

        outs[0].append(k_p.reshape(bsz, seq, SB_HEADS, SB_DH))
        outs[1].append(v_p.reshape(bsz, seq, SB_HEADS, SB_DH))
        outs[2].append(mk.reshape(bsz, n_mem, CA_HEADS, CA_DH))
        outs[3].append(mv.reshape(bsz, n_mem, CA_HEADS, CA_DH))
        outs[4].append(k_s.reshape(dbsz, dseq, SB_HEADS, SB_DH))
        outs[5].append(v_s.reshape(dbsz, dseq, SB_HEADS, SB_DH))
        outs[6].append(va_s.reshape(dbsz, dseq, A_GROUPS, A_DG))

    stacked = [jnp.stack(o, axis=0) for o in outs]
    return (xp.reshape(bsz, seq, D_MODEL), xs.reshape(dbsz, dseq, D_MODEL), *stacked)
```

```python
import functools

import jax
import jax.numpy as jnp
from jax import lax
from jax.experimental import pallas as pl
from jax.experimental.pallas import tpu as pltpu

F32 = jnp.float32
BF16 = jnp.bfloat16

D_MODEL = 2048
A_GROUPS = 8
A_DG = 128
A_WIDTH = A_GROUPS * A_DG
A_CHUNK = 128
SB_HEADS = 8
SB_DH = 128
SB_WIDTH = SB_HEADS * SB_DH
IN_WIDTH = 2 * A_WIDTH + 3 * SB_WIDTH
N_SEG = IN_WIDTH // 1024
CA_HEADS = 4
CA_DH = 128
CA_WIDTH = CA_HEADS * CA_DH
N_EXPERTS = 32
TOP_K = 4
D_EXPERT = D_MODEL
SWIGLU_LIMIT = 7.0
SWIGLU_ALPHA = 1.702
RMS_EPS = 1e-6

LANES = 128
SB_TILE = 256
SB_HEADS_PER_STEP = 4
VMEM_LIMIT = 56 * 1024 * 1024

MOE_PAD = 128
MOE_TILE = 256
MOE_BODY_ROWS = 512
MOE_ITEM_ROWS = 1280
MOE_CHUNK = 512
MOE_NCH = D_EXPERT // MOE_CHUNK
DISPATCH_TILE = 128
COMBINE_TILE = 128


def _rms(x, g):
    return x * lax.rsqrt(jnp.mean(x * x, axis=-1, keepdims=True) + RMS_EPS) * g


def _dot(a, b):
    return jnp.dot(a, b, preferred_element_type=F32)


def _dot_nt(a, b):
    return lax.dot_general(a, b, (((1,), (1,)), ((), ())), preferred_element_type=F32)


def _split_bf16(x):
    hi = x.astype(BF16)
    lo = (x - hi.astype(F32)).astype(BF16)
    return hi, lo


def _pack_bf16_pairs(x):
    c = x.shape[1] // 2
    r = x.astype(BF16).astype(F32)
    lo = lax.bitcast_convert_type(r[:, :c], jnp.uint32) >> 16
    hi = lax.bitcast_convert_type(r[:, c:], jnp.uint32)
    return lax.bitcast_convert_type(hi | lo, F32)


def _unpack_bf16_pairs(w):
    bits = lax.bitcast_convert_type(w, jnp.uint32)
    lo = lax.bitcast_convert_type(bits << 16, F32).astype(BF16)
    hi = lax.bitcast_convert_type(bits & jnp.uint32(0xFFFF0000), F32).astype(BF16)
    return lo, hi


def _params(n_axes):
    return pltpu.CompilerParams(dimension_semantics=("arbitrary",) * n_axes,
                                vmem_limit_bytes=VMEM_LIMIT)


def _const_spec(shape):
    return pl.BlockSpec(shape, lambda *_: (0,) * len(shape), pipeline_mode=pl.Buffered(1))


def _proj_kernel(x_ref, g_ref, w_ref, *out_refs):
    h = _rms(x_ref[...], g_ref[...]).astype(BF16)
    for s, o_ref in enumerate(out_refs):
        o_ref[...] = _dot(h, w_ref[:, s * 1024:(s + 1) * 1024])


def _proj(x, g, w, tm):
    n = x.shape[0]
    seg = jax.ShapeDtypeStruct((n, 1024), F32)
    return pl.pallas_call(
        _proj_kernel,
        grid=(n // tm,),
        in_specs=[pl.BlockSpec((tm, D_MODEL), lambda i: (i, 0)),
                  _const_spec((1, D_MODEL)),
                  _const_spec((D_MODEL, IN_WIDTH))],
        out_specs=[pl.BlockSpec((tm, 1024), lambda i: (i, 0))] * N_SEG,
        out_shape=[seg] * N_SEG,
        compiler_params=_params(1),
        name="proj",
    )(x, g, w)


def _log_sigmoid_neg(z):
    return -(jnp.maximum(z, 0.0) + jnp.log(1.0 + jnp.exp(-jnp.abs(z))))


def _sb_blocks(qbs, kbs, vbs, carries, later, mask):
    n = range(len(qbs))
    zs = [_dot_nt(qbs[i], kbs[i]) * (SB_DH ** -0.5) for i in n]
    ls = [_log_sigmoid_neg(z) for z in zs]
    if mask is not None:
        ls = [jnp.where(mask, l, 0.0) for l in ls]
    splits = [_split_bf16(l) for l in ls]
    css = [_dot(hi, later) + _dot(lo, later) for hi, lo in splits]
    aas = [jnp.exp(ls[i] + zs[i] + css[i] + carries[i]) for i in n]
    if mask is not None:
        aas = [jnp.where(mask, a, 0.0) for a in aas]
    contribs = [_dot(aas[i].astype(BF16), vbs[i]) for i in n]
    new = [carries[i] + css[i][:, 0:1] + ls[i][:, 0:1] for i in n]
    return contribs, new


def _sb_block(qb, kb, vb, carry, later, mask):
    contribs, new = _sb_blocks([qb], [kb], [vb], [carry], later, mask)
    return contribs[0], new[0]


def _later_matrix(n):
    r = lax.broadcasted_iota(jnp.int32, (n, n), 0)
    c = lax.broadcasted_iota(jnp.int32, (n, n), 1)
    return (r > c).astype(BF16)


def _sb_prompt_kernel(q_ref, k_ref, v_ref, o_ref, later_ref, carry_ref):
    qi = pl.program_id(2)
    t = SB_TILE
    later_ref[...] = _later_matrix(t)
    r = lax.broadcasted_iota(jnp.int32, (t, t), 0)
    c = lax.broadcasted_iota(jnp.int32, (t, t), 1)
    heads = [slice(h * SB_DH, (h + 1) * SB_DH) for h in range(SB_HEADS_PER_STEP)]

    def sweep(j, first):
        off = pl.multiple_of(j * t, t)
        rows = pl.ds(off, t)
        carries = [jnp.zeros((t, 1), F32) if first else carry_ref[h] for h in range(len(heads))]
        contribs, carries = _sb_blocks(
            [q_ref[:, sl].astype(BF16) for sl in heads],
            [k_ref[rows, sl].astype(BF16) for sl in heads],
            [v_ref[rows, sl].astype(BF16) for sl in heads],
            carries, later_ref[...], (c < r) if first else None)
        for h, sl in enumerate(heads):
            carry_ref[h] = carries[h]
            if first:
                o_ref[:, sl] = contribs[h]
            else:
                o_ref[:, sl] += contribs[h]

    sweep(qi, True)

    def body(step, carry):
        sweep(qi - 1 - step, False)
        return carry

    lax.fori_loop(0, qi, body, 0)


def _sb_prompt(q, k, v):
    bsz, t_len, _ = q.shape
    t = SB_TILE
    w = SB_HEADS_PER_STEP * SB_DH
    kv_spec = pl.BlockSpec((None, t_len, w), lambda b, h, i: (b, 0, h))
    return pl.pallas_call(
        _sb_prompt_kernel,
        grid=(bsz, SB_HEADS // SB_HEADS_PER_STEP, t_len // t),
        in_specs=[pl.BlockSpec((None, t, w), lambda b, h, i: (b, i, h)), kv_spec, kv_spec],
        out_specs=pl.BlockSpec((None, t, w), lambda b, h, i: (b, i, h)),
        out_shape=jax.ShapeDtypeStruct(q.shape, F32),
        scratch_shapes=[pltpu.VMEM((t, t), BF16), pltpu.VMEM((SB_HEADS_PER_STEP, t, 1), F32)],
        compiler_params=_params(3),
        name="sb_prompt",
    )(q, k, v)


def _sb_sample_kernel(q_ref, kn_ref, vn_ref, ck_ref, cv_ref, o_ref, later_ref):
    t = SB_TILE
    n_new = q_ref.shape[0]
    later_ref[...] = _later_matrix(t)
    r = lax.broadcasted_iota(jnp.int32, (n_new, t), 0)
    c = lax.broadcasted_iota(jnp.int32, (n_new, t), 1)
    heads = range(SB_HEADS)
    sls = [slice(h * SB_DH, (h + 1) * SB_DH) for h in heads]
    qbs = [q_ref[:, sl].astype(BF16) for sl in sls]
    accs, carries = _sb_blocks(qbs, [kn_ref[:, sl].astype(BF16) for sl in sls],
                               [vn_ref[:, sl].astype(BF16) for sl in sls],
                               [jnp.zeros((n_new, 1), F32)] * SB_HEADS, later_ref[...], c < r)
    for j in reversed(range(ck_ref.shape[0] // (t * SB_HEADS))):
        rows = [pl.ds(j * t * SB_HEADS + h, t, stride=SB_HEADS) for h in heads]
        contribs, carries = _sb_blocks(qbs, [ck_ref[rows[h], :].astype(BF16) for h in heads],
                                       [cv_ref[rows[h], :].astype(BF16) for h in heads],
                                       carries, later_ref[...], None)
        accs = [a + cb for a, cb in zip(accs, contribs)]
    for sl, acc in zip(sls, accs):
        o_ref[:, sl] = acc


def _sb_sample(q, k_new, v_new, cache_k, cache_v, layer):
    bsz, n_new, _ = q.shape
    past = cache_k.shape[2]
    t = SB_TILE
    assert n_new <= t and past % t == 0
    pad = ((0, 0), (0, t - n_new), (0, 0))
    kn = jnp.pad(k_new, pad)
    vn = jnp.pad(v_new, pad)
    rows = lambda n: pl.BlockSpec((None, n, SB_WIDTH), lambda b: (b, 0, 0))
    depth = cache_k.shape[0]
    flat = (depth, bsz, past * SB_HEADS, SB_DH)
    cache = pl.BlockSpec((None, None, past * SB_HEADS, SB_DH), lambda b: (layer, b, 0, 0))
    return pl.pallas_call(
        _sb_sample_kernel,
        grid=(bsz,),
        in_specs=[rows(n_new), rows(t), rows(t), cache, cache],
        out_specs=rows(n_new),
        out_shape=jax.ShapeDtypeStruct(q.shape, F32),
        scratch_shapes=[pltpu.VMEM((t, t), BF16)],
        compiler_params=_params(1),
        name="sb_sample",
    )(q, kn, vn, cache_k.reshape(flat), cache_v.reshape(flat))


def _memkv_kernel(m_ref, g_ref, wk_ref, wv_ref, k_ref, v_ref):
    h = _rms(m_ref[...], g_ref[...]).astype(BF16)
    k_ref[...] = _dot(h, wk_ref[...])
    v_ref[...] = _dot(h, wv_ref[...])


def _memkv(mem, g, wk, wv):
    n = mem.shape[0]
    tm = 256
    out = jax.ShapeDtypeStruct((n, CA_WIDTH), F32)
    return pl.pallas_call(
        _memkv_kernel,
        grid=(n // tm,),
        in_specs=[pl.BlockSpec((tm, D_MODEL), lambda i: (i, 0)),
                  _const_spec((1, D_MODEL)),
                  _const_spec((D_MODEL, CA_WIDTH)),
                  _const_spec((D_MODEL, CA_WIDTH))],
        out_specs=[pl.BlockSpec((tm, CA_WIDTH), lambda i: (i, 0))] * 2,
        out_shape=[out, out],
        compiler_params=_params(1),
        name="memkv",
    )(mem, g, wk, wv)


def _post_kernel(x_ref, u_ref, va_ref, b_ref, wsp_ref, bsp_ref, ga_ref, gb_ref, wout_ref,
                 gc_ref, wcq_ref, mk_ref, mv_ref, wco_ref, gm_ref, wr_ref, br_ref,
                 x2_ref, h2_ref, idx_ref, gate_ref):
    tm = x_ref.shape[0]
    blk = wsp_ref.shape[1]

    r = lax.broadcasted_iota(jnp.int32, (blk, blk), 0)
    c = lax.broadcasted_iota(jnp.int32, (blk, blk), 1)
    tri = (c <= r).astype(F32)
    cols = []
    for g in range(A_GROUPS):
        w = (wsp_ref[g] * tri).astype(BF16)
        bias = bsp_ref[:, g:g + 1]
        sl = slice(g * A_DG, (g + 1) * A_DG)
        rows = []
        for ch in range(tm // blk):
            rs = slice(ch * blk, (ch + 1) * blk)
            mixed = _dot(w, va_ref[rs, sl].astype(BF16)) + bias
            rows.append(u_ref[rs, sl] * mixed)
        cols.append(rows[0] if len(rows) == 1 else jnp.concatenate(rows, axis=0))
    a_out = jnp.concatenate(cols, axis=1)

    a_n = _rms(a_out, ga_ref[...]).astype(BF16)
    b_n = _rms(b_ref[...], gb_ref[...]).astype(BF16)
    x1 = x_ref[...] + _dot(a_n, wout_ref[0:A_WIDTH, :]) + _dot(b_n, wout_ref[A_WIDTH:, :])

    hc = _rms(x1, gc_ref[...]).astype(BF16)
    q = _dot(hc, wcq_ref[...])
    heads = []
    for h in range(CA_HEADS):
        sl = slice(h * CA_DH, (h + 1) * CA_DH)
        s = _dot_nt(q[:, sl].astype(BF16), mk_ref[:, sl].astype(BF16)) * (CA_DH ** -0.5)
        p = jnp.exp(s - jnp.max(s, axis=-1, keepdims=True))
        p = p / jnp.sum(p, axis=-1, keepdims=True)
        heads.append(_dot(p.astype(BF16), mv_ref[:, sl].astype(BF16)))
    o = jnp.concatenate(heads, axis=1).astype(BF16)
    x2 = x1 + _dot(o, wco_ref[...])
    x2_ref[...] = x2

    h2 = _rms(x2, gm_ref[...])
    h2_ref[...] = _pack_bf16_pairs(h2)
    hh, hl = _split_bf16(h2)
    wh, wl = _split_bf16(wr_ref[...])
    logits = _dot(hh, wh) + _dot(hh, wl) + _dot(hl, wh) + br_ref[...]
    lane = lax.broadcasted_iota(jnp.int32, (tm, LANES), 1).astype(F32)
    cur = jnp.where(lane < N_EXPERTS, logits, -jnp.inf)
    vals, idxs = [], []
    for _ in range(TOP_K):
        m = jnp.max(cur, axis=-1, keepdims=True)
        i = jnp.min(jnp.where(cur == m, lane, float(LANES)), axis=-1, keepdims=True)
        vals.append(m)
        idxs.append(i)
        cur = jnp.where(lane == i, -jnp.inf, cur)
    es = [jnp.exp(v - vals[0]) for v in vals]
    denom = es[0] + es[1] + es[2] + es[3]
    idx_out = jnp.zeros((tm, LANES), F32)
    gate_out = jnp.zeros((tm, LANES), F32)
    for k in range(TOP_K):
        idx_out = jnp.where(lane == k, idxs[k], idx_out)
        gate_out = jnp.where(lane == k, es[k] / denom, gate_out)
    idx_ref[...] = idx_out.astype(jnp.int32)
    gate_ref[...] = gate_out


def _post(x, u, va, b_out, mk, mv, lw, tm, rows_per_batch, blk):
    n = x.shape[0]
    tiles_per_batch = rows_per_batch // tm
    n_mem = mk.shape[0] // (n // rows_per_batch)
    row = lambda w: pl.BlockSpec((tm, w), lambda i: (i, 0))
    mem = pl.BlockSpec((n_mem, CA_WIDTH), lambda i: (i // tiles_per_batch, 0))
    wsp = lw["w_sp"][:, :blk, :blk]
    bsp = lw["b_sp"][:, :blk].T
    return pl.pallas_call(
        _post_kernel,
        grid=(n // tm,),
        in_specs=[row(D_MODEL), row(A_WIDTH), row(A_WIDTH), row(SB_WIDTH),
                  _const_spec((A_GROUPS, blk, blk)), _const_spec((blk, A_GROUPS)),
                  _const_spec((1, A_WIDTH)), _const_spec((1, SB_WIDTH)),
                  _const_spec((2 * A_WIDTH, D_MODEL)),
                  _const_spec((1, D_MODEL)), _const_spec((D_MODEL, CA_WIDTH)),
                  mem, mem, _const_spec((CA_WIDTH, D_MODEL)),
                  _const_spec((1, D_MODEL)), _const_spec((D_MODEL, LANES)), _const_spec((1, LANES))],
        out_specs=[row(D_MODEL), row(D_MODEL // 2), row(LANES), row(LANES)],
        out_shape=[jax.ShapeDtypeStruct((n, D_MODEL), F32), jax.ShapeDtypeStruct((n, D_MODEL // 2), F32),
                   jax.ShapeDtypeStruct((n, LANES), jnp.int32), jax.ShapeDtypeStruct((n, LANES), F32)],
        compiler_params=_params(1),
        name="post",
    )(x, u, va, b_out, wsp, bsp, lw["g_a"], lw["g_b"], lw["w_out"], lw["g_cross"], lw["w_cq"],
      mk, mv, lw["w_co"], lw["g_moe"], lw["w_router"], lw["b_router"])


def _dispatch_kernel(last_unit, meta, dest_ref, hp_ref, hs_ref, xs_hbm, stage, zero, sem, sem_zero, *,
                     n_prompt_tiles, n_tiles):
    i = pl.program_id(0)
    td = stage.shape[1]
    pad = MOE_PAD
    slot = i % 2

    def zero_copy(row0):
        return pltpu.make_async_copy(zero, xs_hbm.at[pl.ds(pl.multiple_of(row0, pad), pad), :], sem_zero)

    @pl.when(i == 0)
    def _():
        zero[...] = jnp.zeros(zero.shape, F32)

        def fill(wait):
            def act(copy):
                if wait:
                    copy.wait()
                else:
                    copy.start()

            def expert_body(e, carry):
                @pl.when(last_unit[e] >= 0)
                def _():
                    act(zero_copy(last_unit[e]))
                return carry

            def tail_body(t, carry):
                act(zero_copy(t * pad))
                return carry

            lax.fori_loop(0, N_EXPERTS, expert_body, 0)
            lax.fori_loop(meta[1], xs_hbm.shape[0] // pad, tail_body, 0)

        fill(False)
        fill(True)

    def wait_slot(slot):
        for _ in range(TOP_K):
            pltpu.make_async_copy(stage.at[slot], xs_hbm.at[pl.ds(0, td), :], sem.at[slot]).wait()

    @pl.when(i >= 2)
    def _():
        wait_slot(slot)

    @pl.when(i < n_prompt_tiles)
    def _():
        stage[slot] = hp_ref[...]

    @pl.when(i >= n_prompt_tiles)
    def _():
        stage[slot] = hs_ref[...]

    def issue(r, carry):
        for k in range(TOP_K):
            pltpu.make_async_copy(stage.at[slot, pl.ds(r, 1), :],
                                  xs_hbm.at[pl.ds(dest_ref[0, 0, r * TOP_K + k], 1), :],
                                  sem.at[slot]).start()
        return carry

    lax.fori_loop(0, td, issue, 0, unroll=4)

    @pl.when(i == n_tiles - 1)
    def _():
        wait_slot(slot)
        if n_tiles > 1:
            wait_slot(1 - slot)


def _dispatch(last_unit, meta, dest, h_prompt, h_sample, p_rows):
    td = DISPATCH_TILE
    n_p, width = h_prompt.shape
    n_s = h_sample.shape[0]
    assert n_p % td == 0 and n_s % td == 0
    n_prompt_tiles, n_tiles = n_p // td, (n_p + n_s) // td
    grid_spec = pltpu.PrefetchScalarGridSpec(
        num_scalar_prefetch=2,
        grid=(n_tiles,),
        in_specs=[pl.BlockSpec((1, 1, td * TOP_K), lambda i, *_: (i, 0, 0), memory_space=pltpu.SMEM),
                  pl.BlockSpec((td, width), lambda i, *_: (jnp.minimum(i, n_prompt_tiles - 1), 0)),
                  pl.BlockSpec((td, width), lambda i, *_: (jnp.maximum(i - n_prompt_tiles, 0), 0))],
        out_specs=pl.BlockSpec(memory_space=pl.ANY),
        scratch_shapes=[pltpu.VMEM((2, td, width), F32), pltpu.VMEM((MOE_PAD, width), F32),
                        pltpu.SemaphoreType.DMA((2,)), pltpu.SemaphoreType.DMA],
    )
    return pl.pallas_call(
        functools.partial(_dispatch_kernel, n_prompt_tiles=n_prompt_tiles, n_tiles=n_tiles),
        grid_spec=grid_spec,
        out_shape=jax.ShapeDtypeStruct((p_rows, width), F32),
        compiler_params=_params(1),
        name="moe_dispatch",
    )(last_unit, meta, dest.reshape(n_tiles, 1, td * TOP_K), h_prompt, h_sample)


def _expert_kernel(item_e, item_start, item_units, meta,
                   xs_hbm, wg_ref, wl_ref, wd_ref, bg_ref, bl_ref, bd_ref,
                   y_hbm, stage, x_buf, y_buf, sem_in, sem_out):
    s = pl.program_id(0)
    j = pl.program_id(1)
    n_slots = pl.num_programs(0)
    units = item_units[s]
    start = item_start[s]
    pad = MOE_PAD
    assert MOE_NCH >= 2

    def unit_rows(t):
        return pl.ds(pl.multiple_of(t * pad, pad), pad)

    def hbm_rows(first_row, t):
        return pl.ds(pl.multiple_of(first_row + t * pad, pad), pad)

    def copy_in(first_row, t):
        return pltpu.make_async_copy(xs_hbm.at[hbm_rows(first_row, t), :], stage.at[unit_rows(t), :], sem_in)

    def copy_out(t):
        return pltpu.make_async_copy(y_buf.at[unit_rows(t), :], y_hbm.at[hbm_rows(start, t), :], sem_out)

    def for_range(lo, hi, fn):
        def body(t, carry):
            fn(t)
            return carry
        lax.fori_loop(lo, hi, body, 0)

    @pl.when(jnp.logical_and(s == 0, j == 0))
    def _():
        for_range(0, units, lambda t: copy_in(start, t).start())

    @pl.when(jnp.logical_and(j == MOE_NCH - 1, s + 1 < n_slots))
    def _():
        nxt = jnp.minimum(s + 1, n_slots - 1)
        for_range(0, item_units[nxt], lambda t: copy_in(item_start[nxt], t).start())

    @pl.when(j == 0)
    def _():
        for_range(0, units, lambda t: copy_in(start, t).wait())

    def ffn(row0, m, first):
        wg = wg_ref[...].astype(BF16)
        wl = wl_ref[...].astype(BF16)
        wd = wd_ref[...].astype(BF16)
        for h in range(max(1, m // MOE_TILE)):
            mh = min(m, MOE_TILE)
            rows = pl.ds(pl.multiple_of(row0 + h * mh, pad), mh)
            if first:
                xb = jnp.concatenate(_unpack_bf16_pairs(stage[rows, :]), axis=1)
                x_buf[rows, :] = xb
            else:
                xb = x_buf[rows, :]
            glu = jnp.minimum(_dot(xb, wg) + bg_ref[...], SWIGLU_LIMIT)
            lin = jnp.clip(_dot(xb, wl) + bl_ref[...], -SWIGLU_LIMIT, SWIGLU_LIMIT)
            act = glu * jax.nn.sigmoid(SWIGLU_ALPHA * glu) * (lin + 1.0)
            down = _dot(act.astype(BF16), wd)
            if first:
                y_buf[rows, :] = bd_ref[...] + down
            else:
                y_buf[rows, :] += down

        @pl.when(j == MOE_NCH - 1)
        def _():
            for u in range(m // pad):
                copy_out(row0 // pad + u).start()

    def run(first):
        big = MOE_BODY_ROWS
        n_big = units // (big // pad)
        for_range(0, n_big, lambda t: ffn(t * big, big, first))
        base = n_big * big
        rem = units * pad - base
        m = big // 2
        while m >= pad:
            take = (rem & m) != 0

            @pl.when(take)
            def _(base=base, m=m):
                ffn(base, m, first)

            base = base + jnp.where(take, m, 0)
            m //= 2

    @pl.when(jnp.logical_and(units > 0, j == 0))
    def _():
        run(True)

    @pl.when(jnp.logical_and(units > 0, j > 0))
    def _():
        run(False)

    @pl.when(jnp.logical_and(j == MOE_NCH - 1, units > 0))
    def _store():
        for_range(0, units, lambda t: copy_out(t).wait())

    @pl.when(jnp.logical_and(s == pl.num_programs(0) - 1, j == MOE_NCH - 1))
    def _fill_tail():
        y_buf[pl.ds(0, pad), :] = jnp.zeros((pad, D_MODEL), F32)

        def tail_copy(t):
            dst = y_hbm.at[pl.ds(pl.multiple_of(t * pad, pad), pad), :]
            return pltpu.make_async_copy(y_buf.at[pl.ds(0, pad), :], dst, sem_out)

        n_units = y_hbm.shape[0] // pad
        for_range(meta[1], n_units, lambda t: tail_copy(t).start())
        for_range(meta[1], n_units, lambda t: tail_copy(t).wait())


def _experts(item_e, item_start, item_units, meta, xs, w_gu, b_gu, w_dn, b_dn):
    n_slots = item_e.shape[0]
    p_rows = xs.shape[0]
    c = MOE_CHUNK

    def chunk(s, j, ie, ist, iu, m):
        return jnp.where(s < m[0], j, MOE_NCH - 1)

    grid_spec = pltpu.PrefetchScalarGridSpec(
        num_scalar_prefetch=4,
        grid=(n_slots, MOE_NCH),
        in_specs=[
            pl.BlockSpec(memory_space=pl.ANY),
            pl.BlockSpec((None, D_MODEL, c), lambda s, j, ie, *a: (ie[s], 0, chunk(s, j, ie, *a))),
            pl.BlockSpec((None, D_MODEL, c), lambda s, j, ie, *a: (ie[s], 0, MOE_NCH + chunk(s, j, ie, *a))),
            pl.BlockSpec((None, c, D_MODEL), lambda s, j, ie, *a: (ie[s], chunk(s, j, ie, *a), 0)),
            pl.BlockSpec((None, 1, c), lambda s, j, ie, *a: (ie[s], 0, chunk(s, j, ie, *a))),
            pl.BlockSpec((None, 1, c), lambda s, j, ie, *a: (ie[s], 0, MOE_NCH + chunk(s, j, ie, *a))),
            pl.BlockSpec((None, 1, D_MODEL), lambda s, j, ie, *a: (ie[s], 0, 0)),
        ],
        out_specs=pl.BlockSpec(memory_space=pl.ANY),
        scratch_shapes=[
            pltpu.VMEM((MOE_ITEM_ROWS, D_MODEL // 2), F32),
            pltpu.VMEM((MOE_ITEM_ROWS, D_MODEL), BF16),
            pltpu.VMEM((MOE_ITEM_ROWS, D_MODEL), F32),
            pltpu.SemaphoreType.DMA,
            pltpu.SemaphoreType.DMA,
        ],
    )
    return pl.pallas_call(
        _expert_kernel,
        grid_spec=grid_spec,
        out_shape=jax.ShapeDtypeStruct((p_rows, D_MODEL), F32),
        compiler_params=_params(2),
        name="moe_experts",
    )(item_e, item_start, item_units, meta, xs, w_gu, w_gu, w_dn,
      b_gu.reshape(N_EXPERTS, 1, 2 * D_EXPERT), b_gu.reshape(N_EXPERTS, 1, 2 * D_EXPERT),
      b_dn.reshape(N_EXPERTS, 1, D_MODEL))


def _combine_kernel(dest_ref, next_dest_ref, x_ref, gate_ref, gf_ref, y_hbm, o_ref, buf, sem, *,
                    final_norm):
    i = pl.program_id(0)
    tc = x_ref.shape[0]
    slot = i % 2

    def issue(ref, slot):
        for r in range(tc):
            for k in range(TOP_K):
                pltpu.make_async_copy(y_hbm.at[pl.ds(ref[0, 0, r * TOP_K + k], 1), :],
                                      buf.at[slot, pl.ds(k * tc + r, 1), :], sem.at[slot]).start()

    @pl.when(i == 0)
    def _():
        issue(dest_ref, 0)

    @pl.when(i + 1 < pl.num_programs(0))
    def _():
        issue(next_dest_ref, 1 - slot)

    pltpu.make_async_copy(y_hbm.at[pl.ds(0, TOP_K * tc), :], buf.at[slot], sem.at[slot]).wait()
    gate = gate_ref[...]
    acc = x_ref[...]
    for k in range(TOP_K):
        acc = acc + gate[:, k:k + 1] * buf[slot, k * tc:(k + 1) * tc, :]
    o_ref[...] = _rms(acc, gf_ref[...]) if final_norm else acc


def _combine(dest, x2, gate, g_final, y, final_norm):
    tc = COMBINE_TILE
    n = x2.shape[0]
    n_tiles = n // tc
    dest = dest.reshape(n_tiles, 1, tc * TOP_K)
    return pl.pallas_call(
        functools.partial(_combine_kernel, final_norm=final_norm),
        grid=(n_tiles,),
        in_specs=[pl.BlockSpec((1, 1, tc * TOP_K), lambda i: (i, 0, 0), memory_space=pltpu.SMEM),
                  pl.BlockSpec((1, 1, tc * TOP_K), lambda i: (jnp.minimum(i + 1, n_tiles - 1), 0, 0),
                               memory_space=pltpu.SMEM),
                  pl.BlockSpec((tc, D_MODEL), lambda i: (i, 0)),
                  pl.BlockSpec((tc, LANES), lambda i: (i, 0)),
                  _const_spec((1, D_MODEL)),
                  pl.BlockSpec(memory_space=pl.ANY)],
        out_specs=pl.BlockSpec((tc, D_MODEL), lambda i: (i, 0)),
        out_shape=jax.ShapeDtypeStruct((n, D_MODEL), F32),
        scratch_shapes=[pltpu.VMEM((2, TOP_K * tc, D_MODEL), F32), pltpu.SemaphoreType.DMA((2,))],
        compiler_params=_params(1),
        name="moe_combine",
    )(dest, dest, x2, gate, g_final, y)


def _route(top_idx):
    n_tok = top_idx.shape[0]
    n_assign = n_tok * TOP_K
    pad, item_rows = MOE_PAD, MOE_ITEM_ROWS
    p_rows = -(-(n_assign + N_EXPERTS * (pad - 1)) // pad) * pad
    n_slots = N_EXPERTS + -(-p_rows // item_rows)
    experts = jnp.arange(N_EXPERTS, dtype=jnp.int32)

    flat_e = top_idx.reshape(-1)
    onehot = (flat_e[:, None] == experts[None, :]).astype(jnp.int32)
    running = jnp.cumsum(onehot, axis=0)
    rank = jnp.sum(running * onehot, axis=1) - 1
    counts = running[-1]
    padded = (counts + pad - 1) // pad * pad
    pad_end = jnp.cumsum(padded)
    pad_start = pad_end - padded
    dest = pad_start[flat_e] + rank
    last_unit = jnp.where(padded > 0, pad_end - pad, -1).astype(jnp.int32)

    items_per_e = (padded + item_rows - 1) // item_rows
    item_end = jnp.cumsum(items_per_e)
    n_items = item_end[-1]
    slot = jnp.arange(n_slots, dtype=jnp.int32)
    live = slot < n_items
    s_eff = jnp.minimum(slot, n_items - 1)
    e = jnp.minimum(jnp.searchsorted(item_end, s_eff, side="right"), N_EXPERTS - 1).astype(jnp.int32)
    local = s_eff - (item_end[e] - items_per_e[e])
    start = pad_start[e] + local * item_rows
    n_rows = jnp.clip(padded[e] - local * item_rows, 0, item_rows)
    units = jnp.where(live, n_rows // pad, 0).astype(jnp.int32)
    meta = jnp.stack([n_items, pad_end[-1] // pad]).astype(jnp.int32)
    return dest.astype(jnp.int32), last_unit, p_rows, e, start.astype(jnp.int32), units, meta


def _group_front(x, sb_fn, mk, mv, lw, tm_proj, tm_post, rows_per_batch, blk):
    u, va, q, k, vb = _proj(x, lw["g_mix"], lw["w_in"], tm_proj)
    b_out = sb_fn(q, k, vb)
    x2, h2, idx, gate = _post(x, u, va, b_out, mk, mv, lw, tm_post, rows_per_batch, blk)
    return x2, h2, idx, gate, k, vb, va


def kernel(x_prompt, x_sample, cache_sb_k, cache_sb_v, cache_mem_k, cache_mem_v, mem_prompt, g_mix, w_in, w_sp, b_sp, g_a, g_b, w_out, g_cross, g_mem, w_cq, w_ck, w_cv, w_co, g_moe, w_router, b_router, w_gu, b_gu, w_dn, b_dn, g_final):
    depth = g_mix.shape[0]
    bsz, seq, _ = x_prompt.shape
    dbsz, dseq, _ = x_sample.shape
    n_mem = mem_prompt.shape[1]
    n_p, n_s = bsz * seq, dbsz * dseq

    xp = x_prompt.reshape(n_p, D_MODEL)
    xs = x_sample.reshape(n_s, D_MODEL)
    mem = mem_prompt.reshape(bsz * n_mem, D_MODEL)
    outs = [[] for _ in range(7)]
    for l in range(depth):
        lw = {
            "g_mix": g_mix[l][None], "w_in": w_in[l].astype(BF16),
            "w_sp": w_sp[l], "b_sp": b_sp[l], "g_a": g_a[l][None], "g_b": g_b[l][None],
            "w_out": w_out[l].astype(BF16), "g_cross": g_cross[l][None],
            "w_cq": w_cq[l].astype(BF16), "w_co": w_co[l].astype(BF16), "g_moe": g_moe[l][None],
            "w_router": jnp.pad(w_router[l], ((0, 0), (0, LANES - N_EXPERTS))),
            "b_router": jnp.pad(b_router[l], (0, LANES - N_EXPERTS))[None],
        }
        mk, mv = _memkv(mem, g_mem[l][None], w_ck[l].astype(BF16), w_cv[l].astype(BF16))

        def sb_p(q, k, v):
            shp = (bsz, seq, SB_WIDTH)
            return _sb_prompt(q.reshape(shp), k.reshape(shp), v.reshape(shp)).reshape(n_p, SB_WIDTH)

        def sb_s(q, k, v):
            shp = (dbsz, dseq, SB_WIDTH)
            return _sb_sample(q.reshape(shp), k.reshape(shp), v.reshape(shp),
                              cache_sb_k, cache_sb_v, l).reshape(n_s, SB_WIDTH)

        xp2, hp, idx_p, gate_p, k_p, v_p, _ = _group_front(
            xp, sb_p, mk, mv, lw, 256, 256, seq, A_CHUNK)
        xs2, hs, idx_s, gate_s, k_s, v_s, va_s = _group_front(
            xs, sb_s, cache_mem_k[l].reshape(dbsz * n_mem, CA_WIDTH),
            cache_mem_v[l].reshape(dbsz * n_mem, CA_WIDTH), lw, n_s, dseq, dseq, min(dseq, A_CHUNK))

        top_idx = jnp.concatenate([idx_p[:, :TOP_K], idx_s[:, :TOP_K]], axis=0)
        dest, last_unit, p_rows, item_e, item_start, item_units, meta = _route(top_idx)
        x_sorted = _dispatch(last_unit, meta, dest, hp, hs, p_rows)
        y = _experts(item_e, item_start, item_units, meta, x_sorted, w_gu[l], b_gu[l], w_dn[l], b_dn[l])
        last = l == depth - 1
        dest = dest.reshape(n_p + n_s, TOP_K)
        xp = _combine(dest[:n_p], xp2, gate_p, g_final[None], y, last)
        xs = _combine(dest[n_p:], xs2, gate_s, g_final[None], y, last)

        outs[0].append(k_p.reshape(bsz, seq, SB_HEADS, SB_DH))
        outs[1].append(v_p.reshape(bsz, seq, SB_HEADS, SB_DH))
        outs[2].append(mk.reshape(bsz, n_mem, CA_HEADS, CA_DH))
        outs[3].append(mv.reshape(bsz, n_mem, CA_HEADS, CA_DH))
        outs[4].append(k_s.reshape(dbsz, dseq, SB_HEADS, SB_DH))
        outs[5].append(v_s.reshape(dbsz, dseq, SB_HEADS, SB_DH))
        outs[6].append(va_s.reshape(dbsz, dseq, A_GROUPS, A_DG))

    stacked = [jnp.stack(o, axis=0) for o in outs]
    return (xp.reshape(bsz, seq, D_MODEL), xs.reshape(dbsz, dseq, D_MODEL), *stacked)
```

```python
import functools

import jax
import jax.numpy as jnp
from jax import lax
from jax.experimental import pallas as pl
from jax.experimental.pallas import tpu as pltpu

F32 = jnp.float32
BF16 = jnp.bfloat16

D_MODEL = 2048
A_GROUPS = 8
A_DG = 128
A_WIDTH = A_GROUPS * A_DG
A_CHUNK = 128
SB_HEADS = 8
SB_DH = 128
SB_WIDTH = SB_HEADS * SB_DH
IN_WIDTH = 2 * A_WIDTH + 3 * SB_WIDTH
N_SEG = IN_WIDTH // 1024
CA_HEADS = 4
CA_DH = 128
CA_WIDTH = CA_HEADS * CA_DH
N_EXPERTS = 32
TOP_K = 4
D_EXPERT = D_MODEL
SWIGLU_LIMIT = 7.0
SWIGLU_ALPHA = 1.702
RMS_EPS = 1e-6

LANES = 128
SB_TILE = 256
SB_HEADS_PER_STEP = 4
VMEM_LIMIT = 56 * 1024 * 1024

MOE_PAD = 128
MOE_TILE = 256
MOE_BODY_ROWS = 512
MOE_ITEM_ROWS = 1280
MOE_CHUNK = 512
MOE_NCH = D_EXPERT // MOE_CHUNK
DISPATCH_TILE = 128
COMBINE_TILE = 128


def _rms(x, g):
    return x * lax.rsqrt(jnp.mean(x * x, axis=-1, keepdims=True) + RMS_EPS) * g


def _dot(a, b):
    return jnp.dot(a, b, preferred_element_type=F32)


def _dot_nt(a, b):
    return lax.dot_general(a, b, (((1,), (1,)), ((), ())), preferred_element_type=F32)


def _split_bf16(x):
    hi = x.astype(BF16)
    lo = (x - hi.astype(F32)).astype(BF16)
    return hi, lo


def _pack_bf16_pairs(x):
    c = x.shape[1] // 2
    r = x.astype(BF16).astype(F32)
    lo = lax.bitcast_convert_type(r[:, :c], jnp.uint32) >> 16
    hi = lax.bitcast_convert_type(r[:, c:], jnp.uint32)
    return lax.bitcast_convert_type(hi | lo, F32)


def _unpack_bf16_pairs(w):
    bits = lax.bitcast_convert_type(w, jnp.uint32)
    lo = lax.bitcast_convert_type(bits << 16, F32).astype(BF16)
    hi = lax.bitcast_convert_type(bits & jnp.uint32(0xFFFF0000), F32).astype(BF16)
    return lo, hi


def _params(n_axes):
    return pltpu.CompilerParams(dimension_semantics=("arbitrary",) * n_axes,
                                vmem_limit_bytes=VMEM_LIMIT)


def _const_spec(shape):
    return pl.BlockSpec(shape, lambda *_: (0,) * len(shape), pipeline_mode=pl.Buffered(1))


def _proj_kernel(x_ref, g_ref, w_ref, *out_refs):
    h = _rms(x_ref[...], g_ref[...]).astype(BF16)
    for s, o_ref in enumerate(out_refs):
        o_ref[...] = _dot(h, w_ref[:, s * 1024:(s + 1) * 1024])


def _proj(x, g, w, tm):
    n = x.shape[0]
    seg = jax.ShapeDtypeStruct((n, 1024), F32)
    return pl.pallas_call(
        _proj_kernel,
        grid=(n // tm,),
        in_specs=[pl.BlockSpec((tm, D_MODEL), lambda i: (i, 0)),
                  _const_spec((1, D_MODEL)),
                  _const_spec((D_MODEL, IN_WIDTH))],
        out_specs=[pl.BlockSpec((tm, 1024), lambda i: (i, 0))] * N_SEG,
        out_shape=[seg] * N_SEG,
        compiler_params=_params(1),
        name="proj",
    )(x, g, w)


def _log_sigmoid_neg(z):
    return -(jnp.maximum(z, 0.0) + jnp.log(1.0 + jnp.exp(-jnp.abs(z))))


def _sb_blocks(qbs, kbs, vbs, carries, later, mask):
    n = range(len(qbs))
    zs = [_dot_nt(qbs[i], kbs[i]) * (SB_DH ** -0.5) for i in n]
    ls = [_log_sigmoid_neg(z) for z in zs]
    if mask is not None:
        ls = [jnp.where(mask, l, 0.0) for l in ls]
    splits = [_split_bf16(l) for l in ls]
    css = [_dot(hi, later) + _dot(lo, later) for hi, lo in splits]
    aas = [jnp.exp(ls[i] + zs[i] + css[i] + carries[i]) for i in n]
    if mask is not None:
        aas = [jnp.where(mask, a, 0.0) for a in aas]
    contribs = [_dot(aas[i].astype(BF16), vbs[i]) for i in n]
    new = [carries[i] + css[i][:, 0:1] + ls[i][:, 0:1] for i in n]
    return contribs, new


def _sb_block(qb, kb, vb, carry, later, mask):
    contribs, new = _sb_blocks([qb], [kb], [vb], [carry], later, mask)
    return contribs[0], new[0]


def _later_matrix(n):
    r = lax.broadcasted_iota(jnp.int32, (n, n), 0)
    c = lax.broadcasted_iota(jnp.int32, (n, n), 1)
    return (r > c).astype(BF16)


def _sb_prompt_kernel(q_ref, k_ref, v_ref, o_ref, later_ref, carry_ref):
    qi = pl.program_id(2)
    t = SB_TILE
    later_ref[...] = _later_matrix(t)
    r = lax.broadcasted_iota(jnp.int32, (t, t), 0)
    c = lax.broadcasted_iota(jnp.int32, (t, t), 1)
    heads = [slice(h * SB_DH, (h + 1) * SB_DH) for h in range(SB_HEADS_PER_STEP)]

    def sweep(j, first):
        off = pl.multiple_of(j * t, t)
        rows = pl.ds(off, t)
        carries = [jnp.zeros((t, 1), F32) if first else carry_ref[h] for h in range(len(heads))]
        contribs, carries = _sb_blocks(
            [q_ref[:, sl].astype(BF16) for sl in heads],
            [k_ref[rows, sl].astype(BF16) for sl in heads],
            [v_ref[rows, sl].astype(BF16) for sl in heads],
            carries, later_ref[...], (c < r) if first else None)
        for h, sl in enumerate(heads):
            carry_ref[h] = carries[h]
            if first:
                o_ref[:, sl] = contribs[h]
            else:
                o_ref[:, sl] += contribs[h]

    sweep(qi, True)

    def body(step, carry):
        sweep(qi - 1 - step, False)
        return carry

    lax.fori_loop(0, qi, body, 0)


def _sb_prompt(q, k, v):
    bsz, t_len, _ = q.shape
    t = SB_TILE
    w = SB_HEADS_PER_STEP * SB_DH
    kv_spec = pl.BlockSpec((None, t_len, w), lambda b, h, i: (b, 0, h))
    return pl.pallas_call(
        _sb_prompt_kernel,
        grid=(bsz, SB_HEADS // SB_HEADS_PER_STEP, t_len // t),
        in_specs=[pl.BlockSpec((None, t, w), lambda b, h, i: (b, i, h)), kv_spec, kv_spec],
        out_specs=pl.BlockSpec((None, t, w), lambda b, h, i: (b, i, h)),
        out_shape=jax.ShapeDtypeStruct(q.shape, F32),
        scratch_shapes=[pltpu.VMEM((t, t), BF16), pltpu.VMEM((SB_HEADS_PER_STEP, t, 1), F32)],
        compiler_params=_params(3),
        name="sb_prompt",
    )(q, k, v)


def _sb_sample_kernel(q_ref, kn_ref, vn_ref, ck_ref, cv_ref, o_ref, later_ref):
    t = SB_TILE
    n_new = q_ref.shape[0]
    later_ref[...] = _later_matrix(t)
    r = lax.broadcasted_iota(jnp.int32, (n_new, t), 0)
    c = lax.broadcasted_iota(jnp.int32, (n_new, t), 1)
    heads = range(SB_HEADS)
    sls = [slice(h * SB_DH, (h + 1) * SB_DH) for h in heads]
    qbs = [q_ref[:, sl].astype(BF16) for sl in sls]
    accs, carries = _sb_blocks(qbs, [kn_ref[:, sl].astype(BF16) for sl in sls],
                               [vn_ref[:, sl].astype(BF16) for sl in sls],
                               [jnp.zeros((n_new, 1), F32)] * SB_HEADS, later_ref[...], c < r)
    for j in reversed(range(ck_ref.shape[0] // (t * SB_HEADS))):
        rows = [pl.ds(j * t * SB_HEADS + h, t, stride=SB_HEADS) for h in heads]
        contribs, carries = _sb_blocks(qbs, [ck_ref[rows[h], :].astype(BF16) for h in heads],
                                       [cv_ref[rows[h], :].astype(BF16) for h in heads],
                                       carries, later_ref[...], None)
        accs = [a + cb for a, cb in zip(accs, contribs)]
    for sl, acc in zip(sls, accs):
        o_ref[:, sl] = acc


def _sb_sample(q, k_new, v_new, cache_k, cache_v, layer):
    bsz, n_new, _ = q.shape
    past = cache_k.shape[2]
    t = SB_TILE
    assert n_new <= t and past % t == 0
    pad = ((0, 0), (0, t - n_new), (0, 0))
    kn = jnp.pad(k_new, pad)
    vn = jnp.pad(v_new, pad)
    rows = lambda n: pl.BlockSpec((None, n, SB_WIDTH), lambda b: (b, 0, 0))
    depth = cache_k.shape[0]
    flat = (depth, bsz, past * SB_HEADS, SB_DH)
    cache = pl.BlockSpec((None, None, past * SB_HEADS, SB_DH), lambda b: (layer, b, 0, 0))
    return pl.pallas_call(
        _sb_sample_kernel,
        grid=(bsz,),
        in_specs=[rows(n_new), rows(t), rows(t), cache, cache],
        out_specs=rows(n_new),
        out_shape=jax.ShapeDtypeStruct(q.shape, F32),
        scratch_shapes=[pltpu.VMEM((t, t), BF16)],
        compiler_params=_params(1),
        name="sb_sample",
    )(q, kn, vn, cache_k.reshape(flat), cache_v.reshape(flat))


def _memkv_kernel(m_ref, g_ref, wk_ref, wv_ref, k_ref, v_ref):
    h = _rms(m_ref[...], g_ref[...]).astype(BF16)
    k_ref[...] = _dot(h, wk_ref[...])
    v_ref[...] = _dot(h, wv_ref[...])


def _memkv(mem, g, wk, wv):
    n = mem.shape[0]
    tm = 256
    out = jax.ShapeDtypeStruct((n, CA_WIDTH), F32)
    return pl.pallas_call(
        _memkv_kernel,
        grid=(n // tm,),
        in_specs=[pl.BlockSpec((tm, D_MODEL), lambda i: (i, 0)),
                  _const_spec((1, D_MODEL)),
                  _const_spec((D_MODEL, CA_WIDTH)),
                  _const_spec((D_MODEL, CA_WIDTH))],
        out_specs=[pl.BlockSpec((tm, CA_WIDTH), lambda i: (i, 0))] * 2,
        out_shape=[out, out],
        compiler_params=_params(1),
        name="memkv",
    )(mem, g, wk, wv)


def _post_kernel(x_ref, u_ref, va_ref, b_ref, wsp_ref, bsp_ref, ga_ref, gb_ref, wout_ref,
                 gc_ref, wcq_ref, mk_ref, mv_ref, wco_ref, gm_ref, wr_ref, br_ref,
                 x2_ref, h2_ref, idx_ref, gate_ref):
    tm = x_ref.shape[0]
    blk = wsp_ref.shape[1]

    r = lax.broadcasted_iota(jnp.int32, (blk, blk), 0)
    c = lax.broadcasted_iota(jnp.int32, (blk, blk), 1)
    tri = (c <= r).astype(F32)
    cols = []
    for g in range(A_GROUPS):
        w = (wsp_ref[g] * tri).astype(BF16)
        bias = bsp_ref[:, g:g + 1]
        sl = slice(g * A_DG, (g + 1) * A_DG)
        chunks = [slice(ch * blk, (ch + 1) * blk) for ch in range(tm // blk)]
        rows = []
        for pair in [chunks[p:p + 2] for p in range(0, len(chunks), 2)]:
            v = [va_ref[rs, sl].astype(BF16) for rs in pair]
            mixed = _dot(w, v[0] if len(v) == 1 else jnp.concatenate(v, axis=1))
            for p, rs in enumerate(pair):
                rows.append(u_ref[rs, sl] * (mixed[:, p * A_DG:(p + 1) * A_DG] + bias))
        cols.append(rows[0] if len(rows) == 1 else jnp.concatenate(rows, axis=0))
    a_out = jnp.concatenate(cols, axis=1)

    a_n = _rms(a_out, ga_ref[...]).astype(BF16)
    b_n = _rms(b_ref[...], gb_ref[...]).astype(BF16)
    x1 = x_ref[...] + _dot(a_n, wout_ref[0:A_WIDTH, :]) + _dot(b_n, wout_ref[A_WIDTH:, :])

    hc = _rms(x1, gc_ref[...]).astype(BF16)
    q = _dot(hc, wcq_ref[...])
    heads = []
    for h in range(CA_HEADS):
        sl = slice(h * CA_DH, (h + 1) * CA_DH)
        s = _dot_nt(q[:, sl].astype(BF16), mk_ref[:, sl].astype(BF16)) * (CA_DH ** -0.5)
        p = jnp.exp(s - jnp.max(s, axis=-1, keepdims=True))
        p = p / jnp.sum(p, axis=-1, keepdims=True)
        heads.append(_dot(p.astype(BF16), mv_ref[:, sl].astype(BF16)))
    o = jnp.concatenate(heads, axis=1).astype(BF16)
    x2 = x1 + _dot(o, wco_ref[...])
    x2_ref[...] = x2

    h2 = _rms(x2, gm_ref[...])
    h2_ref[...] = _pack_bf16_pairs(h2)
    hh, hl = _split_bf16(h2)
    wh, wl = _split_bf16(wr_ref[...])
    both = _dot(hh, jnp.concatenate([wh, wl], axis=1))
    logits = both[:, :LANES] + both[:, LANES:] + _dot(hl, wh) + br_ref[...]
    lane = lax.broadcasted_iota(jnp.int32, (tm, LANES), 1).astype(F32)
    cur = jnp.where(lane < N_EXPERTS, logits, -jnp.inf)
    vals, idxs = [], []
    for _ in range(TOP_K):
        m = jnp.max(cur, axis=-1, keepdims=True)
        i = jnp.min(jnp.where(cur == m, lane, float(LANES)), axis=-1, keepdims=True)
        vals.append(m)
        idxs.append(i)
        cur = jnp.where(lane == i, -jnp.inf, cur)
    es = [jnp.exp(v - vals[0]) for v in vals]
    denom = es[0] + es[1] + es[2] + es[3]
    idx_out = jnp.zeros((tm, LANES), F32)
    gate_out = jnp.zeros((tm, LANES), F32)
    for k in range(TOP_K):
        idx_out = jnp.where(lane == k, idxs[k], idx_out)
        gate_out = jnp.where(lane == k, es[k] / denom, gate_out)
    idx_ref[...] = idx_out.astype(jnp.int32)
    gate_ref[...] = gate_out


def _post(x, u, va, b_out, mk, mv, lw, tm, rows_per_batch, blk):
    n = x.shape[0]
    tiles_per_batch = rows_per_batch // tm
    n_mem = mk.shape[0] // (n // rows_per_batch)
    row = lambda w: pl.BlockSpec((tm, w), lambda i: (i, 0))
    mem = pl.BlockSpec((n_mem, CA_WIDTH), lambda i: (i // tiles_per_batch, 0))
    wsp = lw["w_sp"][:, :blk, :blk]
    bsp = lw["b_sp"][:, :blk].T
    return pl.pallas_call(
        _post_kernel,
        grid=(n // tm,),
        in_specs=[row(D_MODEL), row(A_WIDTH), row(A_WIDTH), row(SB_WIDTH),
                  _const_spec((A_GROUPS, blk, blk)), _const_spec((blk, A_GROUPS)),
                  _const_spec((1, A_WIDTH)), _const_spec((1, SB_WIDTH)),
                  _const_spec((2 * A_WIDTH, D_MODEL)),
                  _const_spec((1, D_MODEL)), _const_spec((D_MODEL, CA_WIDTH)),
                  mem, mem, _const_spec((CA_WIDTH, D_MODEL)),
                  _const_spec((1, D_MODEL)), _const_spec((D_MODEL, LANES)), _const_spec((1, LANES))],
        out_specs=[row(D_MODEL), row(D_MODEL // 2), row(LANES), row(LANES)],
        out_shape=[jax.ShapeDtypeStruct((n, D_MODEL), F32), jax.ShapeDtypeStruct((n, D_MODEL // 2), F32),
                   jax.ShapeDtypeStruct((n, LANES), jnp.int32), jax.ShapeDtypeStruct((n, LANES), F32)],
        compiler_params=_params(1),
        name="post",
    )(x, u, va, b_out, wsp, bsp, lw["g_a"], lw["g_b"], lw["w_out"], lw["g_cross"], lw["w_cq"],
      mk, mv, lw["w_co"], lw["g_moe"], lw["w_router"], lw["b_router"])


def _dispatch_kernel(last_unit, meta, dest_ref, hp_ref, hs_ref, xs_hbm, stage, zero, sem, sem_zero, *,
                     n_prompt_tiles, n_tiles):
    i = pl.program_id(0)
    td = stage.shape[1]
    pad = MOE_PAD
    slot = i % 2

    def zero_copy(row0):
        return pltpu.make_async_copy(zero, xs_hbm.at[pl.ds(pl.multiple_of(row0, pad), pad), :], sem_zero)

    @pl.when(i == 0)
    def _():
        zero[...] = jnp.zeros(zero.shape, F32)

        def fill(wait):
            def act(copy):
                if wait:
                    copy.wait()
                else:
                    copy.start()

            def expert_body(e, carry):
                @pl.when(last_unit[e] >= 0)
                def _():
                    act(zero_copy(last_unit[e]))
                return carry

            def tail_body(t, carry):
                act(zero_copy(t * pad))
                return carry

            lax.fori_loop(0, N_EXPERTS, expert_body, 0)
            lax.fori_loop(meta[1], xs_hbm.shape[0] // pad, tail_body, 0)

        fill(False)
        fill(True)

    def wait_slot(slot):
        for _ in range(TOP_K):
            pltpu.make_async_copy(stage.at[slot], xs_hbm.at[pl.ds(0, td), :], sem.at[slot]).wait()

    @pl.when(i >= 2)
    def _():
        wait_slot(slot)

    @pl.when(i < n_prompt_tiles)
    def _():
        stage[slot] = hp_ref[...]

    @pl.when(i >= n_prompt_tiles)
    def _():
        stage[slot] = hs_ref[...]

    def issue(r, carry):
        for k in range(TOP_K):
            pltpu.make_async_copy(stage.at[slot, pl.ds(r, 1), :],
                                  xs_hbm.at[pl.ds(dest_ref[0, 0, r * TOP_K + k], 1), :],
                                  sem.at[slot]).start()
        return carry

    lax.fori_loop(0, td, issue, 0, unroll=4)

    @pl.when(i == n_tiles - 1)
    def _():
        wait_slot(slot)
        if n_tiles > 1:
            wait_slot(1 - slot)


def _dispatch(last_unit, meta, dest, h_prompt, h_sample, p_rows):
    td = DISPATCH_TILE
    n_p, width = h_prompt.shape
    n_s = h_sample.shape[0]
    assert n_p % td == 0 and n_s % td == 0
    n_prompt_tiles, n_tiles = n_p // td, (n_p + n_s) // td
    grid_spec = pltpu.PrefetchScalarGridSpec(
        num_scalar_prefetch=2,
        grid=(n_tiles,),
        in_specs=[pl.BlockSpec((1, 1, td * TOP_K), lambda i, *_: (i, 0, 0), memory_space=pltpu.SMEM),
                  pl.BlockSpec((td, width), lambda i, *_: (jnp.minimum(i, n_prompt_tiles - 1), 0)),
                  pl.BlockSpec((td, width), lambda i, *_: (jnp.maximum(i - n_prompt_tiles, 0), 0))],
        out_specs=pl.BlockSpec(memory_space=pl.ANY),
        scratch_shapes=[pltpu.VMEM((2, td, width), F32), pltpu.VMEM((MOE_PAD, width), F32),
                        pltpu.SemaphoreType.DMA((2,)), pltpu.SemaphoreType.DMA],
    )
    return pl.pallas_call(
        functools.partial(_dispatch_kernel, n_prompt_tiles=n_prompt_tiles, n_tiles=n_tiles),
        grid_spec=grid_spec,
        out_shape=jax.ShapeDtypeStruct((p_rows, width), F32),
        compiler_params=_params(1),
        name="moe_dispatch",
    )(last_unit, meta, dest.reshape(n_tiles, 1, td * TOP_K), h_prompt, h_sample)


def _expert_kernel(item_e, item_start, item_units, meta,
                   xs_hbm, wg_ref, wl_ref, wd_ref, bg_ref, bl_ref, bd_ref,
                   y_hbm, stage, x_buf, y_buf, sem_in, sem_out):
    s = pl.program_id(0)
    j = pl.program_id(1)
    n_slots = pl.num_programs(0)
    units = item_units[s]
    start = item_start[s]
    pad = MOE_PAD
    assert MOE_NCH >= 2

    def unit_rows(t):
        return pl.ds(pl.multiple_of(t * pad, pad), pad)

    def hbm_rows(first_row, t):
        return pl.ds(pl.multiple_of(first_row + t * pad, pad), pad)

    def copy_in(first_row, t):
        return pltpu.make_async_copy(xs_hbm.at[hbm_rows(first_row, t), :], stage.at[unit_rows(t), :], sem_in)

    def copy_out(t):
        return pltpu.make_async_copy(y_buf.at[unit_rows(t), :], y_hbm.at[hbm_rows(start, t), :], sem_out)

    def for_range(lo, hi, fn):
        def body(t, carry):
            fn(t)
            return carry
        lax.fori_loop(lo, hi, body, 0)

    @pl.when(jnp.logical_and(s == 0, j == 0))
    def _():
        for_range(0, units, lambda t: copy_in(start, t).start())

    @pl.when(jnp.logical_and(j == MOE_NCH - 1, s + 1 < n_slots))
    def _():
        nxt = jnp.minimum(s + 1, n_slots - 1)
        for_range(0, item_units[nxt], lambda t: copy_in(item_start[nxt], t).start())

    @pl.when(j == 0)
    def _():
        for_range(0, units, lambda t: copy_in(start, t).wait())

    def ffn(row0, m, first):
        wg = wg_ref[...].astype(BF16)
        wl = wl_ref[...].astype(BF16)
        wd = wd_ref[...].astype(BF16)
        for off in range(0, m, MOE_TILE):
            rows = pl.ds(pl.multiple_of(row0 + off, pad), min(MOE_TILE, m - off))
            if first:
                xb = jnp.concatenate(_unpack_bf16_pairs(stage[rows, :]), axis=1)
                x_buf[rows, :] = xb
            else:
                xb = x_buf[rows, :]
            glu = jnp.minimum(_dot(xb, wg) + bg_ref[...], SWIGLU_LIMIT)
            lin = jnp.clip(_dot(xb, wl) + bl_ref[...], -SWIGLU_LIMIT, SWIGLU_LIMIT)
            act = glu * jax.nn.sigmoid(SWIGLU_ALPHA * glu) * (lin + 1.0)
            down = _dot(act.astype(BF16), wd)
            if first:
                y_buf[rows, :] = bd_ref[...] + down
            else:
                y_buf[rows, :] += down

        @pl.when(j == MOE_NCH - 1)
        def _():
            for u in range(m // pad):
                copy_out(row0 // pad + u).start()

    def run(first):
        big = MOE_BODY_ROWS
        big_units = big // pad
        merge = jnp.logical_and(units % big_units == 1, units > big_units)
        n_big = units // big_units - jnp.where(merge, 1, 0)
        for_range(0, n_big, lambda t: ffn(t * big, big, first))
        base = n_big * big

        @pl.when(merge)
        def _():
            ffn(base, big + pad, first)

        rem = jnp.where(merge, 0, units * pad - base)
        m = big // 2
        while m >= pad:
            take = (rem & m) != 0

            @pl.when(take)
            def _(base=base, m=m):
                ffn(base, m, first)

            base = base + jnp.where(take, m, 0)
            m //= 2

    @pl.when(jnp.logical_and(units > 0, j == 0))
    def _():
        run(True)

    @pl.when(jnp.logical_and(units > 0, j > 0))
    def _():
        run(False)

    @pl.when(jnp.logical_and(j == MOE_NCH - 1, units > 0))
    def _store():
        for_range(0, units, lambda t: copy_out(t).wait())

    @pl.when(jnp.logical_and(s == pl.num_programs(0) - 1, j == MOE_NCH - 1))
    def _fill_tail():
        y_buf[pl.ds(0, pad), :] = jnp.zeros((pad, D_MODEL), F32)

        def tail_copy(t):
            dst = y_hbm.at[pl.ds(pl.multiple_of(t * pad, pad), pad), :]
            return pltpu.make_async_copy(y_buf.at[pl.ds(0, pad), :], dst, sem_out)

        n_units = y_hbm.shape[0] // pad
        for_range(meta[1], n_units, lambda t: tail_copy(t).start())
        for_range(meta[1], n_units, lambda t: tail_copy(t).wait())


def _experts(item_e, item_start, item_units, meta, xs, w_gu, b_gu, w_dn, b_dn):
    n_slots = item_e.shape[0]
    p_rows = xs.shape[0]
    c = MOE_CHUNK

    def chunk(s, j, ie, ist, iu, m):
        return jnp.where(s < m[0], j, MOE_NCH - 1)

    grid_spec = pltpu.PrefetchScalarGridSpec(
        num_scalar_prefetch=4,
        grid=(n_slots, MOE_NCH),
        in_specs=[
            pl.BlockSpec(memory_space=pl.ANY),
            pl.BlockSpec((None, D_MODEL, c), lambda s, j, ie, *a: (ie[s], 0, chunk(s, j, ie, *a))),
            pl.BlockSpec((None, D_MODEL, c), lambda s, j, ie, *a: (ie[s], 0, MOE_NCH + chunk(s, j, ie, *a))),
            pl.BlockSpec((None, c, D_MODEL), lambda s, j, ie, *a: (ie[s], chunk(s, j, ie, *a), 0)),
            pl.BlockSpec((None, 1, c), lambda s, j, ie, *a: (ie[s], 0, chunk(s, j, ie, *a))),
            pl.BlockSpec((None, 1, c), lambda s, j, ie, *a: (ie[s], 0, MOE_NCH + chunk(s, j, ie, *a))),
            pl.BlockSpec((None, 1, D_MODEL), lambda s, j, ie, *a: (ie[s], 0, 0)),
        ],
        out_specs=pl.BlockSpec(memory_space=pl.ANY),
        scratch_shapes=[
            pltpu.VMEM((MOE_ITEM_ROWS, D_MODEL // 2), F32),
            pltpu.VMEM((MOE_ITEM_ROWS, D_MODEL), BF16),
            pltpu.VMEM((MOE_ITEM_ROWS, D_MODEL), F32),
            pltpu.SemaphoreType.DMA,
            pltpu.SemaphoreType.DMA,
        ],
    )
    return pl.pallas_call(
        _expert_kernel,
        grid_spec=grid_spec,
        out_shape=jax.ShapeDtypeStruct((p_rows, D_MODEL), F32),
        compiler_params=_params(2),
        name="moe_experts",
    )(item_e, item_start, item_units, meta, xs, w_gu, w_gu, w_dn,
      b_gu.reshape(N_EXPERTS, 1, 2 * D_EXPERT), b_gu.reshape(N_EXPERTS, 1, 2 * D_EXPERT),
      b_dn.reshape(N_EXPERTS, 1, D_MODEL))


def _combine_kernel(dest_ref, next_dest_ref, x_ref, gate_ref, gf_ref, y_hbm, o_ref, buf, sem, *,
                    final_norm):
    i = pl.program_id(0)
    tc = x_ref.shape[0]
    slot = i % 2

    def issue(ref, slot):
        for r in range(tc):
            for k in range(TOP_K):
                pltpu.make_async_copy(y_hbm.at[pl.ds(ref[0, 0, r * TOP_K + k], 1), :],
                                      buf.at[slot, pl.ds(k * tc + r, 1), :], sem.at[slot]).start()

    @pl.when(i == 0)
    def _():
        issue(dest_ref, 0)

    @pl.when(i + 1 < pl.num_programs(0))
    def _():
        issue(next_dest_ref, 1 - slot)

    pltpu.make_async_copy(y_hbm.at[pl.ds(0, TOP_K * tc), :], buf.at[slot], sem.at[slot]).wait()
    gate = gate_ref[...]
    acc = x_ref[...]
    for k in range(TOP_K):
        acc = acc + gate[:, k:k + 1] * buf[slot, k * tc:(k + 1) * tc, :]
    o_ref[...] = _rms(acc, gf_ref[...]) if final_norm else acc


def _combine(dest, x2, gate, g_final, y, final_norm):
    tc = COMBINE_TILE
    n = x2.shape[0]
    n_tiles = n // tc
    dest = dest.reshape(n_tiles, 1, tc * TOP_K)
    return pl.pallas_call(
        functools.partial(_combine_kernel, final_norm=final_norm),
        grid=(n_tiles,),
        in_specs=[pl.BlockSpec((1, 1, tc * TOP_K), lambda i: (i, 0, 0), memory_space=pltpu.SMEM),
                  pl.BlockSpec((1, 1, tc * TOP_K), lambda i: (jnp.minimum(i + 1, n_tiles - 1), 0, 0),
                               memory_space=pltpu.SMEM),
                  pl.BlockSpec((tc, D_MODEL), lambda i: (i, 0)),
                  pl.BlockSpec((tc, LANES), lambda i: (i, 0)),
                  _const_spec((1, D_MODEL)),
                  pl.BlockSpec(memory_space=pl.ANY)],
        out_specs=pl.BlockSpec((tc, D_MODEL), lambda i: (i, 0)),
        out_shape=jax.ShapeDtypeStruct((n, D_MODEL), F32),
        scratch_shapes=[pltpu.VMEM((2, TOP_K * tc, D_MODEL), F32), pltpu.SemaphoreType.DMA((2,))],
        compiler_params=_params(1),
        name="moe_combine",
    )(dest, dest, x2, gate, g_final, y)


def _route(top_idx):
    n_tok = top_idx.shape[0]
    n_assign = n_tok * TOP_K
    pad, item_rows = MOE_PAD, MOE_ITEM_ROWS
    p_rows = -(-(n_assign + N_EXPERTS * (pad - 1)) // pad) * pad
    n_slots = N_EXPERTS + -(-p_rows // item_rows)
    experts = jnp.arange(N_EXPERTS, dtype=jnp.int32)

    flat_e = top_idx.reshape(-1)
    onehot = (flat_e[:, None] == experts[None, :]).astype(jnp.int32)
    running = jnp.cumsum(onehot, axis=0)
    rank = jnp.sum(running * onehot, axis=1) - 1
    counts = running[-1]
    padded = (counts + pad - 1) // pad * pad
    pad_end = jnp.cumsum(padded)
    pad_start = pad_end - padded
    dest = pad_start[flat_e] + rank
    last_unit = jnp.where(padded > 0, pad_end - pad, -1).astype(jnp.int32)

    items_per_e = (padded + item_rows - 1) // item_rows
    item_end = jnp.cumsum(items_per_e)
    n_items = item_end[-1]
    slot = jnp.arange(n_slots, dtype=jnp.int32)
    live = slot < n_items
    s_eff = jnp.minimum(slot, n_items - 1)
    e = jnp.minimum(jnp.searchsorted(item_end, s_eff, side="right"), N_EXPERTS - 1).astype(jnp.int32)
    local = s_eff - (item_end[e] - items_per_e[e])
    start = pad_start[e] + local * item_rows
    n_rows = jnp.clip(padded[e] - local * item_rows, 0, item_rows)
    units = jnp.where(live, n_rows // pad, 0).astype(jnp.int32)
    meta = jnp.stack([n_items, pad_end[-1] // pad]).astype(jnp.int32)
    return dest.astype(jnp.int32), last_unit, p_rows, e, start.astype(jnp.int32), units, meta


def _group_front(x, sb_fn, mk, mv, lw, tm_proj, tm_post, rows_per_batch, blk):
    u, va, q, k, vb = _proj(x, lw["g_mix"], lw["w_in"], tm_proj)
    b_out = sb_fn(q, k, vb)
    x2, h2, idx, gate = _post(x, u, va, b_out, mk, mv, lw, tm_post, rows_per_batch, blk)
    return x2, h2, idx, gate, k, vb, va


def kernel(x_prompt, x_sample, cache_sb_k, cache_sb_v, cache_mem_k, cache_mem_v, mem_prompt, g_mix, w_in, w_sp, b_sp, g_a, g_b, w_out, g_cross, g_mem, w_cq, w_ck, w_cv, w_co, g_moe, w_router, b_router, w_gu, b_gu, w_dn, b_dn, g_final):
    depth = g_mix.shape[0]
    bsz, seq, _ = x_prompt.shape
    dbsz, dseq, _ = x_sample.shape
    n_mem = mem_prompt.shape[1]
    n_p, n_s = bsz * seq, dbsz * dseq

    xp = x_prompt.reshape(n_p, D_MODEL)
    xs = x_sample.reshape(n_s, D_MODEL)
    mem = mem_prompt.reshape(bsz * n_mem, D_MODEL)
    outs = [[] for _ in range(7)]
    for l in range(depth):
        lw = {
            "g_mix": g_mix[l][None], "w_in": w_in[l].astype(BF16),
            "w_sp": w_sp[l], "b_sp": b_sp[l], "g_a": g_a[l][None], "g_b": g_b[l][None],
            "w_out": w_out[l].astype(BF16), "g_cross": g_cross[l][None],
            "w_cq": w_cq[l].astype(BF16), "w_co": w_co[l].astype(BF16), "g_moe": g_moe[l][None],
            "w_router": jnp.pad(w_router[l], ((0, 0), (0, LANES - N_EXPERTS))),
            "b_router": jnp.pad(b_router[l], (0, LANES - N_EXPERTS))[None],
        }
        mk, mv = _memkv(mem, g_mem[l][None], w_ck[l].astype(BF16), w_cv[l].astype(BF16))

        def sb_p(q, k, v):
            shp = (bsz, seq, SB_WIDTH)
            return _sb_prompt(q.reshape(shp), k.reshape(shp), v.reshape(shp)).reshape(n_p, SB_WIDTH)

        def sb_s(q, k, v):
            shp = (dbsz, dseq, SB_WIDTH)
            return _sb_sample(q.reshape(shp), k.reshape(shp), v.reshape(shp),
                              cache_sb_k, cache_sb_v, l).reshape(n_s, SB_WIDTH)

        xp2, hp, idx_p, gate_p, k_p, v_p, _ = _group_front(
            xp, sb_p, mk, mv, lw, 256, 256, seq, A_CHUNK)
        xs2, hs, idx_s, gate_s, k_s, v_s, va_s = _group_front(
            xs, sb_s, cache_mem_k[l].reshape(dbsz * n_mem, CA_WIDTH),
            cache_mem_v[l].reshape(dbsz * n_mem, CA_WIDTH), lw, n_s, dseq, dseq, min(dseq, A_CHUNK))

        top_idx = jnp.concatenate([idx_p[:, :TOP_K], idx_s[:, :TOP_K]], axis=0)
        dest, last_unit, p_rows, item_e, item_start, item_units, meta = _route(top_idx)
        x_sorted = _dispatch(last_unit, meta, dest, hp, hs, p_rows)
        y = _experts(item_e, item_start, item_units, meta, x_sorted, w_gu[l], b_gu[l], w_dn[l], b_dn[l])
        last = l == depth - 1
        dest = dest.reshape(n_p + n_s, TOP_K)
        xp = _combine(dest[:n_p], xp2, gate_p, g_final[None], y, last)
        xs = _combine(dest[n_p:], xs2, gate_s, g_final[None], y, last)

        outs[0].append(k_p.reshape(bsz, seq, SB_HEADS, SB_DH))
        outs[1].append(v_p.reshape(bsz, seq, SB_HEADS, SB_DH))
        outs[2].append(mk.reshape(bsz, n_mem, CA_HEADS, CA_DH))
        outs[3].append(mv.reshape(bsz, n_mem, CA_HEADS, CA_DH))
        outs[4].append(k_s.reshape(dbsz, dseq, SB_HEADS, SB_DH))
        outs[5].append(v_s.reshape(dbsz, dseq, SB_HEADS, SB_DH))
        outs[6].append(va_s.reshape(dbsz, dseq, A_GROUPS, A_DG))

    stacked = [jnp.stack(o, axis=0) for o in outs]
    return (xp.reshape(bsz, seq, D_MODEL), xs.reshape(dbsz, dseq, D_MODEL), *stacked)
```

```python
import functools

import jax
import jax.numpy as jnp
from jax import lax
from jax.experimental import pallas as pl
from jax.experimental.pallas import tpu as pltpu

F32 = jnp.float32
BF16 = jnp.bfloat16

D_MODEL = 2048
A_GROUPS = 8
A_DG = 128
A_WIDTH = A_GROUPS * A_DG
A_CHUNK = 128
SB_HEADS = 8
SB_DH = 128
SB_WIDTH = SB_HEADS * SB_DH
IN_WIDTH = 2 * A_WIDTH + 3 * SB_WIDTH
N_SEG = IN_WIDTH // 1024
CA_HEADS = 4
CA_DH = 128
CA_WIDTH = CA_HEADS * CA_DH
N_EXPERTS = 32
TOP_K = 4
D_EXPERT = D_MODEL
SWIGLU_LIMIT = 7.0
SWIGLU_ALPHA = 1.702
RMS_EPS = 1e-6

LANES = 128
SB_TILE = 256
SB_HEADS_PER_STEP = 4
VMEM_LIMIT = 56 * 1024 * 1024

MOE_PAD = 128
MOE_TILE = 256
MOE_BODY_ROWS = 512
MOE_ITEM_ROWS = 1280
MOE_CHUNK = 512
MOE_NCH = D_EXPERT // MOE_CHUNK
DISPATCH_TILE = 128
COMBINE_TILE = 128


def _rms(x, g):
    return x * lax.rsqrt(jnp.mean(x * x, axis=-1, keepdims=True) + RMS_EPS) * g


def _dot(a, b):
    return jnp.dot(a, b, preferred_element_type=F32)


def _dot_nt(a, b):
    return lax.dot_general(a, b, (((1,), (1,)), ((), ())), preferred_element_type=F32)


def _split_bf16(x):
    hi = x.astype(BF16)
    lo = (x - hi.astype(F32)).astype(BF16)
    return hi, lo


def _pack_bf16_pairs(x):
    c = x.shape[1] // 2
    r = x.astype(BF16).astype(F32)
    lo = lax.bitcast_convert_type(r[:, :c], jnp.uint32) >> 16
    hi = lax.bitcast_convert_type(r[:, c:], jnp.uint32)
    return lax.bitcast_convert_type(hi | lo, F32)


def _unpack_bf16_pairs(w):
    bits = lax.bitcast_convert_type(w, jnp.uint32)
    lo = lax.bitcast_convert_type(bits << 16, F32).astype(BF16)
    hi = lax.bitcast_convert_type(bits & jnp.uint32(0xFFFF0000), F32).astype(BF16)
    return lo, hi


def _params(n_axes):
    return pltpu.CompilerParams(dimension_semantics=("arbitrary",) * n_axes,
                                vmem_limit_bytes=VMEM_LIMIT)


def _const_spec(shape):
    return pl.BlockSpec(shape, lambda *_: (0,) * len(shape), pipeline_mode=pl.Buffered(1))


def _proj_kernel(x_ref, g_ref, w_ref, *out_refs):
    h = _rms(x_ref[...], g_ref[...]).astype(BF16)
    for s, o_ref in enumerate(out_refs):
        o_ref[...] = _dot(h, w_ref[:, s * 1024:(s + 1) * 1024])


def _proj(x, g, w, tm):
    n = x.shape[0]
    seg = jax.ShapeDtypeStruct((n, 1024), F32)
    return pl.pallas_call(
        _proj_kernel,
        grid=(n // tm,),
        in_specs=[pl.BlockSpec((tm, D_MODEL), lambda i: (i, 0)),
                  _const_spec((1, D_MODEL)),
                  _const_spec((D_MODEL, IN_WIDTH))],
        out_specs=[pl.BlockSpec((tm, 1024), lambda i: (i, 0))] * N_SEG,
        out_shape=[seg] * N_SEG,
        compiler_params=_params(1),
        name="proj",
    )(x, g, w)


def _log_sigmoid_neg(z):
    return -(jnp.maximum(z, 0.0) + jnp.log(1.0 + jnp.exp(-jnp.abs(z))))


def _sb_blocks(qbs, kbs, vbs, carries, later, mask):
    n = range(len(qbs))
    zs = [_dot_nt(qbs[i], kbs[i]) * (SB_DH ** -0.5) for i in n]
    ls = [_log_sigmoid_neg(z) for z in zs]
    if mask is not None:
        ls = [jnp.where(mask, l, 0.0) for l in ls]
    splits = [_split_bf16(l) for l in ls]
    css = [_dot(hi, later) + _dot(lo, later) for hi, lo in splits]
    aas = [jnp.exp(ls[i] + zs[i] + css[i] + carries[i]) for i in n]
    if mask is not None:
        aas = [jnp.where(mask, a, 0.0) for a in aas]
    contribs = [_dot(aas[i].astype(BF16), vbs[i]) for i in n]
    new = [carries[i] + css[i][:, 0:1] + ls[i][:, 0:1] for i in n]
    return contribs, new


def _sb_block(qb, kb, vb, carry, later, mask):
    contribs, new = _sb_blocks([qb], [kb], [vb], [carry], later, mask)
    return contribs[0], new[0]


def _later_matrix(n):
    r = lax.broadcasted_iota(jnp.int32, (n, n), 0)
    c = lax.broadcasted_iota(jnp.int32, (n, n), 1)
    return (r > c).astype(BF16)


def _sb_prompt_kernel(q_ref, k_ref, v_ref, o_ref, later_ref, carry_ref):
    qi = pl.program_id(2)
    t = SB_TILE
    later_ref[...] = _later_matrix(t)
    r = lax.broadcasted_iota(jnp.int32, (t, t), 0)
    c = lax.broadcasted_iota(jnp.int32, (t, t), 1)
    heads = [slice(h * SB_DH, (h + 1) * SB_DH) for h in range(SB_HEADS_PER_STEP)]

    def sweep(j, first):
        off = pl.multiple_of(j * t, t)
        rows = pl.ds(off, t)
        carries = [jnp.zeros((t, 1), F32) if first else carry_ref[h] for h in range(len(heads))]
        contribs, carries = _sb_blocks(
            [q_ref[:, sl].astype(BF16) for sl in heads],
            [k_ref[rows, sl].astype(BF16) for sl in heads],
            [v_ref[rows, sl].astype(BF16) for sl in heads],
            carries, later_ref[...], (c < r) if first else None)
        for h, sl in enumerate(heads):
            carry_ref[h] = carries[h]
            if first:
                o_ref[:, sl] = contribs[h]
            else:
                o_ref[:, sl] += contribs[h]

    sweep(qi, True)

    def body(step, carry):
        sweep(qi - 1 - step, False)
        return carry

    lax.fori_loop(0, qi, body, 0)


def _sb_prompt(q, k, v):
    bsz, t_len, _ = q.shape
    t = SB_TILE
    w = SB_HEADS_PER_STEP * SB_DH
    kv_spec = pl.BlockSpec((None, t_len, w), lambda b, h, i: (b, 0, h))
    return pl.pallas_call(
        _sb_prompt_kernel,
        grid=(bsz, SB_HEADS // SB_HEADS_PER_STEP, t_len // t),
        in_specs=[pl.BlockSpec((None, t, w), lambda b, h, i: (b, i, h)), kv_spec, kv_spec],
        out_specs=pl.BlockSpec((None, t, w), lambda b, h, i: (b, i, h)),
        out_shape=jax.ShapeDtypeStruct(q.shape, F32),
        scratch_shapes=[pltpu.VMEM((t, t), BF16), pltpu.VMEM((SB_HEADS_PER_STEP, t, 1), F32)],
        compiler_params=_params(3),
        name="sb_prompt",
    )(q, k, v)


def _sb_sample_kernel(q_ref, kn_ref, vn_ref, ck_ref, cv_ref, o_ref, later_ref):
    t = SB_TILE
    n_new = q_ref.shape[0]
    later_ref[...] = _later_matrix(t)
    r = lax.broadcasted_iota(jnp.int32, (n_new, t), 0)
    c = lax.broadcasted_iota(jnp.int32, (n_new, t), 1)
    heads = range(SB_HEADS)
    sls = [slice(h * SB_DH, (h + 1) * SB_DH) for h in heads]
    qbs = [q_ref[:, sl].astype(BF16) for sl in sls]
    accs, carries = _sb_blocks(qbs, [kn_ref[:, sl].astype(BF16) for sl in sls],
                               [vn_ref[:, sl].astype(BF16) for sl in sls],
                               [jnp.zeros((n_new, 1), F32)] * SB_HEADS, later_ref[...], c < r)
    for j in reversed(range(ck_ref.shape[0] // (t * SB_HEADS))):
        rows = [pl.ds(j * t * SB_HEADS + h, t, stride=SB_HEADS) for h in heads]
        contribs, carries = _sb_blocks(qbs, [ck_ref[rows[h], :].astype(BF16) for h in heads],
                                       [cv_ref[rows[h], :].astype(BF16) for h in heads],
                                       carries, later_ref[...], None)
        accs = [a + cb for a, cb in zip(accs, contribs)]
    for sl, acc in zip(sls, accs):
        o_ref[:, sl] = acc


def _sb_sample(q, k_new, v_new, cache_k, cache_v, layer):
    bsz, n_new, _ = q.shape
    past = cache_k.shape[2]
    t = SB_TILE
    assert n_new <= t and past % t == 0
    pad = ((0, 0), (0, t - n_new), (0, 0))
    kn = jnp.pad(k_new, pad)
    vn = jnp.pad(v_new, pad)
    rows = lambda n: pl.BlockSpec((None, n, SB_WIDTH), lambda b: (b, 0, 0))
    depth = cache_k.shape[0]
    flat = (depth, bsz, past * SB_HEADS, SB_DH)
    cache = pl.BlockSpec((None, None, past * SB_HEADS, SB_DH), lambda b: (layer, b, 0, 0))
    return pl.pallas_call(
        _sb_sample_kernel,
        grid=(bsz,),
        in_specs=[rows(n_new), rows(t), rows(t), cache, cache],
        out_specs=rows(n_new),
        out_shape=jax.ShapeDtypeStruct(q.shape, F32),
        scratch_shapes=[pltpu.VMEM((t, t), BF16)],
        compiler_params=_params(1),
        name="sb_sample",
    )(q, kn, vn, cache_k.reshape(flat), cache_v.reshape(flat))


def _memkv_kernel(m_ref, g_ref, wk_ref, wv_ref, k_ref, v_ref):
    h = _rms(m_ref[...], g_ref[...]).astype(BF16)
    k_ref[...] = _dot(h, wk_ref[...])
    v_ref[...] = _dot(h, wv_ref[...])


def _memkv(mem, g, wk, wv):
    n = mem.shape[0]
    tm = 256
    out = jax.ShapeDtypeStruct((n, CA_WIDTH), F32)
    return pl.pallas_call(
        _memkv_kernel,
        grid=(n // tm,),
        in_specs=[pl.BlockSpec((tm, D_MODEL), lambda i: (i, 0)),
                  _const_spec((1, D_MODEL)),
                  _const_spec((D_MODEL, CA_WIDTH)),
                  _const_spec((D_MODEL, CA_WIDTH))],
        out_specs=[pl.BlockSpec((tm, CA_WIDTH), lambda i: (i, 0))] * 2,
        out_shape=[out, out],
        compiler_params=_params(1),
        name="memkv",
    )(mem, g, wk, wv)


def _post_kernel(x_ref, u_ref, va_ref, b_ref, wsp_ref, bsp_ref, ga_ref, gb_ref, wout_ref,
                 gc_ref, wcq_ref, mk_ref, mv_ref, wco_ref, gm_ref, wr_ref, br_ref,
                 x2_ref, h2_ref, idx_ref, gate_ref, *, n_batches):
    tm = x_ref.shape[0]
    blk = wsp_ref.shape[1]

    r = lax.broadcasted_iota(jnp.int32, (blk, blk), 0)
    c = lax.broadcasted_iota(jnp.int32, (blk, blk), 1)
    tri = (c <= r).astype(F32)
    cols = []
    for g in range(A_GROUPS):
        w = (wsp_ref[g] * tri).astype(BF16)
        bias = bsp_ref[:, g:g + 1]
        sl = slice(g * A_DG, (g + 1) * A_DG)
        chunks = [slice(ch * blk, (ch + 1) * blk) for ch in range(tm // blk)]
        rows = []
        for pair in [chunks[p:p + 2] for p in range(0, len(chunks), 2)]:
            v = [va_ref[rs, sl].astype(BF16) for rs in pair]
            mixed = _dot(w, v[0] if len(v) == 1 else jnp.concatenate(v, axis=1))
            for p, rs in enumerate(pair):
                rows.append(u_ref[rs, sl] * (mixed[:, p * A_DG:(p + 1) * A_DG] + bias))
        cols.append(rows[0] if len(rows) == 1 else jnp.concatenate(rows, axis=0))
    a_out = jnp.concatenate(cols, axis=1)

    a_n = _rms(a_out, ga_ref[...]).astype(BF16)
    b_n = _rms(b_ref[...], gb_ref[...]).astype(BF16)
    x1 = x_ref[...] + _dot(a_n, wout_ref[0:A_WIDTH, :]) + _dot(b_n, wout_ref[A_WIDTH:, :])

    hc = _rms(x1, gc_ref[...]).astype(BF16)
    q = _dot(hc, wcq_ref[...])
    rows_b = tm // n_batches
    mem_b = mk_ref.shape[0] // n_batches
    per_batch = []
    for b in range(n_batches):
        rs = slice(b * rows_b, (b + 1) * rows_b)
        ms = slice(b * mem_b, (b + 1) * mem_b)
        heads = []
        for h in range(CA_HEADS):
            sl = slice(h * CA_DH, (h + 1) * CA_DH)
            s = _dot_nt(q[rs, sl].astype(BF16), mk_ref[ms, sl].astype(BF16)) * (CA_DH ** -0.5)
            p = jnp.exp(s - jnp.max(s, axis=-1, keepdims=True))
            p = p / jnp.sum(p, axis=-1, keepdims=True)
            heads.append(_dot(p.astype(BF16), mv_ref[ms, sl].astype(BF16)))
        per_batch.append(jnp.concatenate(heads, axis=1))
    o = (per_batch[0] if n_batches == 1 else jnp.concatenate(per_batch, axis=0)).astype(BF16)
    x2 = x1 + _dot(o, wco_ref[...])
    x2_ref[...] = x2

    h2 = _rms(x2, gm_ref[...])
    h2_ref[...] = _pack_bf16_pairs(h2)
    hh, hl = _split_bf16(h2)
    wh, wl = _split_bf16(wr_ref[...])
    both = _dot(hh, jnp.concatenate([wh, wl], axis=1))
    logits = both[:, :LANES] + both[:, LANES:] + _dot(hl, wh) + br_ref[...]
    lane = lax.broadcasted_iota(jnp.int32, (tm, LANES), 1).astype(F32)
    cur = jnp.where(lane < N_EXPERTS, logits, -jnp.inf)
    vals, idxs = [], []
    for _ in range(TOP_K):
        m = jnp.max(cur, axis=-1, keepdims=True)
        i = jnp.min(jnp.where(cur == m, lane, float(LANES)), axis=-1, keepdims=True)
        vals.append(m)
        idxs.append(i)
        cur = jnp.where(lane == i, -jnp.inf, cur)
    es = [jnp.exp(v - vals[0]) for v in vals]
    denom = es[0] + es[1] + es[2] + es[3]
    idx_out = jnp.zeros((tm, LANES), F32)
    gate_out = jnp.zeros((tm, LANES), F32)
    for k in range(TOP_K):
        idx_out = jnp.where(lane == k, idxs[k], idx_out)
        gate_out = jnp.where(lane == k, es[k] / denom, gate_out)
    idx_ref[...] = idx_out.astype(jnp.int32)
    gate_ref[...] = gate_out


def _post(x, u, va, b_out, mk, mv, lw, tm, rows_per_batch, blk):
    n = x.shape[0]
    n_mem = mk.shape[0] // (n // rows_per_batch)
    row = lambda w: pl.BlockSpec((tm, w), lambda i: (i, 0))
    if tm >= rows_per_batch:
        n_batches = tm // rows_per_batch
        mem = pl.BlockSpec((n_batches * n_mem, CA_WIDTH), lambda i: (i, 0))
    else:
        n_batches = 1
        tiles_per_batch = rows_per_batch // tm
        mem = pl.BlockSpec((n_mem, CA_WIDTH), lambda i: (i // tiles_per_batch, 0))
    wsp = lw["w_sp"][:, :blk, :blk]
    bsp = lw["b_sp"][:, :blk].T
    return pl.pallas_call(
        functools.partial(_post_kernel, n_batches=n_batches),
        grid=(n // tm,),
        in_specs=[row(D_MODEL), row(A_WIDTH), row(A_WIDTH), row(SB_WIDTH),
                  _const_spec((A_GROUPS, blk, blk)), _const_spec((blk, A_GROUPS)),
                  _const_spec((1, A_WIDTH)), _const_spec((1, SB_WIDTH)),
                  _const_spec((2 * A_WIDTH, D_MODEL)),
                  _const_spec((1, D_MODEL)), _const_spec((D_MODEL, CA_WIDTH)),
                  mem, mem, _const_spec((CA_WIDTH, D_MODEL)),
                  _const_spec((1, D_MODEL)), _const_spec((D_MODEL, LANES)), _const_spec((1, LANES))],
        out_specs=[row(D_MODEL), row(D_MODEL // 2), row(LANES), row(LANES)],
        out_shape=[jax.ShapeDtypeStruct((n, D_MODEL), F32), jax.ShapeDtypeStruct((n, D_MODEL // 2), F32),
                   jax.ShapeDtypeStruct((n, LANES), jnp.int32), jax.ShapeDtypeStruct((n, LANES), F32)],
        compiler_params=_params(1),
        name="post",
    )(x, u, va, b_out, wsp, bsp, lw["g_a"], lw["g_b"], lw["w_out"], lw["g_cross"], lw["w_cq"],
      mk, mv, lw["w_co"], lw["g_moe"], lw["w_router"], lw["b_router"])


def _dispatch_kernel(last_unit, meta, dest_ref, hp_ref, hs_ref, xs_hbm, stage, zero, sem, sem_zero, *,
                     n_prompt_tiles, n_tiles):
    i = pl.program_id(0)
    td = stage.shape[1]
    pad = MOE_PAD
    slot = i % 2

    def zero_copy(row0):
        return pltpu.make_async_copy(zero, xs_hbm.at[pl.ds(pl.multiple_of(row0, pad), pad), :], sem_zero)

    @pl.when(i == 0)
    def _():
        zero[...] = jnp.zeros(zero.shape, F32)

        def fill(wait):
            def act(copy):
                if wait:
                    copy.wait()
                else:
                    copy.start()

            def expert_body(e, carry):
                @pl.when(last_unit[e] >= 0)
                def _():
                    act(zero_copy(last_unit[e]))
                return carry

            def tail_body(t, carry):
                act(zero_copy(t * pad))
                return carry

            lax.fori_loop(0, N_EXPERTS, expert_body, 0)
            lax.fori_loop(meta[1], xs_hbm.shape[0] // pad, tail_body, 0)

        fill(False)
        fill(True)

    def wait_slot(slot):
        for _ in range(TOP_K):
            pltpu.make_async_copy(stage.at[slot], xs_hbm.at[pl.ds(0, td), :], sem.at[slot]).wait()

    @pl.when(i >= 2)
    def _():
        wait_slot(slot)

    @pl.when(i < n_prompt_tiles)
    def _():
        stage[slot] = hp_ref[...]

    @pl.when(i >= n_prompt_tiles)
    def _():
        stage[slot] = hs_ref[...]

    def issue(r, carry):
        for k in range(TOP_K):
            pltpu.make_async_copy(stage.at[slot, pl.ds(r, 1), :],
                                  xs_hbm.at[pl.ds(dest_ref[0, 0, r * TOP_K + k], 1), :],
                                  sem.at[slot]).start()
        return carry

    lax.fori_loop(0, td, issue, 0, unroll=4)

    @pl.when(i == n_tiles - 1)
    def _():
        wait_slot(slot)
        if n_tiles > 1:
            wait_slot(1 - slot)


def _dispatch(last_unit, meta, dest, h_prompt, h_sample, p_rows):
    td = DISPATCH_TILE
    n_p, width = h_prompt.shape
    n_s = h_sample.shape[0]
    assert n_p % td == 0 and n_s % td == 0
    n_prompt_tiles, n_tiles = n_p // td, (n_p + n_s) // td
    grid_spec = pltpu.PrefetchScalarGridSpec(
        num_scalar_prefetch=2,
        grid=(n_tiles,),
        in_specs=[pl.BlockSpec((1, 1, td * TOP_K), lambda i, *_: (i, 0, 0), memory_space=pltpu.SMEM),
                  pl.BlockSpec((td, width), lambda i, *_: (jnp.minimum(i, n_prompt_tiles - 1), 0)),
                  pl.BlockSpec((td, width), lambda i, *_: (jnp.maximum(i - n_prompt_tiles, 0), 0))],
        out_specs=pl.BlockSpec(memory_space=pl.ANY),
        scratch_shapes=[pltpu.VMEM((2, td, width), F32), pltpu.VMEM((MOE_PAD, width), F32),
                        pltpu.SemaphoreType.DMA((2,)), pltpu.SemaphoreType.DMA],
    )
    return pl.pallas_call(
        functools.partial(_dispatch_kernel, n_prompt_tiles=n_prompt_tiles, n_tiles=n_tiles),
        grid_spec=grid_spec,
        out_shape=jax.ShapeDtypeStruct((p_rows, width), F32),
        compiler_params=_params(1),
        name="moe_dispatch",
    )(last_unit, meta, dest.reshape(n_tiles, 1, td * TOP_K), h_prompt, h_sample)


def _expert_kernel(item_e, item_start, item_units, meta,
                   xs_hbm, wg_ref, wl_ref, wd_ref, bg_ref, bl_ref, bd_ref,
                   y_hbm, stage, x_buf, y_buf, sem_in, sem_out):
    s = pl.program_id(0)
    j = pl.program_id(1)
    n_slots = pl.num_programs(0)
    units = item_units[s]
    start = item_start[s]
    pad = MOE_PAD
    assert MOE_NCH >= 2

    def unit_rows(t):
        return pl.ds(pl.multiple_of(t * pad, pad), pad)

    def hbm_rows(first_row, t):
        return pl.ds(pl.multiple_of(first_row + t * pad, pad), pad)

    def copy_in(first_row, t):
        return pltpu.make_async_copy(xs_hbm.at[hbm_rows(first_row, t), :], stage.at[unit_rows(t), :], sem_in)

    def copy_out(t):
        return pltpu.make_async_copy(y_buf.at[unit_rows(t), :], y_hbm.at[hbm_rows(start, t), :], sem_out)

    def for_range(lo, hi, fn):
        def body(t, carry):
            fn(t)
            return carry
        lax.fori_loop(lo, hi, body, 0)

    @pl.when(jnp.logical_and(s == 0, j == 0))
    def _():
        for_range(0, units, lambda t: copy_in(start, t).start())

    @pl.when(jnp.logical_and(j == MOE_NCH - 1, s + 1 < n_slots))
    def _():
        nxt = jnp.minimum(s + 1, n_slots - 1)
        for_range(0, item_units[nxt], lambda t: copy_in(item_start[nxt], t).start())

    @pl.when(j == 0)
    def _():
        for_range(0, units, lambda t: copy_in(start, t).wait())

    def ffn(row0, m, first):
        wg = wg_ref[...].astype(BF16)
        wl = wl_ref[...].astype(BF16)
        wd = wd_ref[...].astype(BF16)
        for off in range(0, m, MOE_TILE):
            rows = pl.ds(pl.multiple_of(row0 + off, pad), min(MOE_TILE, m - off))
            if first:
                xb = jnp.concatenate(_unpack_bf16_pairs(stage[rows, :]), axis=1)
                x_buf[rows, :] = xb
            else:
                xb = x_buf[rows, :]
            glu = jnp.minimum(_dot(xb, wg) + bg_ref[...], SWIGLU_LIMIT)
            lin = jnp.clip(_dot(xb, wl) + bl_ref[...], -SWIGLU_LIMIT, SWIGLU_LIMIT)
            act = glu * jax.nn.sigmoid(SWIGLU_ALPHA * glu) * (lin + 1.0)
            down = _dot(act.astype(BF16), wd)
            if first:
                y_buf[rows, :] = bd_ref[...] + down
            else:
                y_buf[rows, :] += down

        @pl.when(j == MOE_NCH - 1)
        def _():
            for u in range(m // pad):
                copy_out(row0 // pad + u).start()

    def run(first):
        big = MOE_BODY_ROWS
        n_big = units // (big // pad)
        for_range(0, n_big, lambda t: ffn(t * big, big, first))
        base = n_big * big
        rem = units * pad - base
        m = big // 2
        while m >= pad:
            take = (rem & m) != 0

            @pl.when(take)
            def _(base=base, m=m):
                ffn(base, m, first)

            base = base + jnp.where(take, m, 0)
            m //= 2

    @pl.when(jnp.logical_and(units > 0, j == 0))
    def _():
        run(True)

    @pl.when(jnp.logical_and(units > 0, j > 0))
    def _():
        run(False)

    @pl.when(jnp.logical_and(j == MOE_NCH - 1, units > 0))
    def _store():
        for_range(0, units, lambda t: copy_out(t).wait())

    @pl.when(jnp.logical_and(s == pl.num_programs(0) - 1, j == MOE_NCH - 1))
    def _fill_tail():
        y_buf[pl.ds(0, pad), :] = jnp.zeros((pad, D_MODEL), F32)

        def tail_copy(t):
            dst = y_hbm.at[pl.ds(pl.multiple_of(t * pad, pad), pad), :]
            return pltpu.make_async_copy(y_buf.at[pl.ds(0, pad), :], dst, sem_out)

        n_units = y_hbm.shape[0] // pad
        for_range(meta[1], n_units, lambda t: tail_copy(t).start())
        for_range(meta[1], n_units, lambda t: tail_copy(t).wait())


def _experts(item_e, item_start, item_units, meta, xs, w_gu, b_gu, w_dn, b_dn):
    n_slots = item_e.shape[0]
    p_rows = xs.shape[0]
    c = MOE_CHUNK

    def chunk(s, j, ie, ist, iu, m):
        return jnp.where(s < m[0], j, MOE_NCH - 1)

    grid_spec = pltpu.PrefetchScalarGridSpec(
        num_scalar_prefetch=4,
        grid=(n_slots, MOE_NCH),
        in_specs=[
            pl.BlockSpec(memory_space=pl.ANY),
            pl.BlockSpec((None, D_MODEL, c), lambda s, j, ie, *a: (ie[s], 0, chunk(s, j, ie, *a))),
            pl.BlockSpec((None, D_MODEL, c), lambda s, j, ie, *a: (ie[s], 0, MOE_NCH + chunk(s, j, ie, *a))),
            pl.BlockSpec((None, c, D_MODEL), lambda s, j, ie, *a: (ie[s], chunk(s, j, ie, *a), 0)),
            pl.BlockSpec((None, 1, c), lambda s, j, ie, *a: (ie[s], 0, chunk(s, j, ie, *a))),
            pl.BlockSpec((None, 1, c), lambda s, j, ie, *a: (ie[s], 0, MOE_NCH + chunk(s, j, ie, *a))),
            pl.BlockSpec((None, 1, D_MODEL), lambda s, j, ie, *a: (ie[s], 0, 0)),
        ],
        out_specs=pl.BlockSpec(memory_space=pl.ANY),
        scratch_shapes=[
            pltpu.VMEM((MOE_ITEM_ROWS, D_MODEL // 2), F32),
            pltpu.VMEM((MOE_ITEM_ROWS, D_MODEL), BF16),
            pltpu.VMEM((MOE_ITEM_ROWS, D_MODEL), F32),
            pltpu.SemaphoreType.DMA,
            pltpu.SemaphoreType.DMA,
        ],
    )
    return pl.pallas_call(
        _expert_kernel,
        grid_spec=grid_spec,
        out_shape=jax.ShapeDtypeStruct((p_rows, D_MODEL), F32),
        compiler_params=_params(2),
        name="moe_experts",
    )(item_e, item_start, item_units, meta, xs, w_gu, w_gu, w_dn,
      b_gu.reshape(N_EXPERTS, 1, 2 * D_EXPERT), b_gu.reshape(N_EXPERTS, 1, 2 * D_EXPERT),
      b_dn.reshape(N_EXPERTS, 1, D_MODEL))


def _combine_kernel(dest_ref, next_dest_ref, x_ref, gate_ref, gf_ref, y_hbm, o_ref, buf, sem, *,
                    final_norm):
    i = pl.program_id(0)
    tc = x_ref.shape[0]
    slot = i % 2

    def issue(ref, slot):
        for r in range(tc):
            for k in range(TOP_K):
                pltpu.make_async_copy(y_hbm.at[pl.ds(ref[0, 0, r * TOP_K + k], 1), :],
                                      buf.at[slot, pl.ds(k * tc + r, 1), :], sem.at[slot]).start()

    @pl.when(i == 0)
    def _():
        issue(dest_ref, 0)

    @pl.when(i + 1 < pl.num_programs(0))
    def _():
        issue(next_dest_ref, 1 - slot)

    pltpu.make_async_copy(y_hbm.at[pl.ds(0, TOP_K * tc), :], buf.at[slot], sem.at[slot]).wait()
    gate = gate_ref[...]
    acc = x_ref[...]
    for k in range(TOP_K):
        acc = acc + gate[:, k:k + 1] * buf[slot, k * tc:(k + 1) * tc, :]
    o_ref[...] = _rms(acc, gf_ref[...]) if final_norm else acc


def _combine(dest, x2, gate, g_final, y, final_norm):
    tc = COMBINE_TILE
    n = x2.shape[0]
    n_tiles = n // tc
    dest = dest.reshape(n_tiles, 1, tc * TOP_K)
    return pl.pallas_call(
        functools.partial(_combine_kernel, final_norm=final_norm),
        grid=(n_tiles,),
        in_specs=[pl.BlockSpec((1, 1, tc * TOP_K), lambda i: (i, 0, 0), memory_space=pltpu.SMEM),
                  pl.BlockSpec((1, 1, tc * TOP_K), lambda i: (jnp.minimum(i + 1, n_tiles - 1), 0, 0),
                               memory_space=pltpu.SMEM),
                  pl.BlockSpec((tc, D_MODEL), lambda i: (i, 0)),
                  pl.BlockSpec((tc, LANES), lambda i: (i, 0)),
                  _const_spec((1, D_MODEL)),
                  pl.BlockSpec(memory_space=pl.ANY)],
        out_specs=pl.BlockSpec((tc, D_MODEL), lambda i: (i, 0)),
        out_shape=jax.ShapeDtypeStruct((n, D_MODEL), F32),
        scratch_shapes=[pltpu.VMEM((2, TOP_K * tc, D_MODEL), F32), pltpu.SemaphoreType.DMA((2,))],
        compiler_params=_params(1),
        name="moe_combine",
    )(dest, dest, x2, gate, g_final, y)


def _route(top_idx):
    n_tok = top_idx.shape[0]
    n_assign = n_tok * TOP_K
    pad, item_rows = MOE_PAD, MOE_ITEM_ROWS
    p_rows = -(-(n_assign + N_EXPERTS * (pad - 1)) // pad) * pad
    n_slots = N_EXPERTS + -(-p_rows // item_rows)
    experts = jnp.arange(N_EXPERTS, dtype=jnp.int32)

    flat_e = top_idx.reshape(-1)
    onehot = (top_idx[:, :, None] == experts[None, None, :]).astype(jnp.int32)
    chosen = jnp.sum(onehot, axis=1)
    running = jnp.cumsum(chosen, axis=0)
    rank = jnp.sum((running - chosen)[:, None, :] * onehot, axis=2).reshape(-1)
    counts = running[-1]
    padded = (counts + pad - 1) // pad * pad
    pad_end = jnp.cumsum(padded)
    pad_start = pad_end - padded
    dest = pad_start[flat_e] + rank
    last_unit = jnp.where(padded > 0, pad_end - pad, -1).astype(jnp.int32)

    items_per_e = (padded + item_rows - 1) // item_rows
    item_end = jnp.cumsum(items_per_e)
    n_items = item_end[-1]
    slot = jnp.arange(n_slots, dtype=jnp.int32)
    live = slot < n_items
    s_eff = jnp.minimum(slot, n_items - 1)
    e = jnp.minimum(jnp.searchsorted(item_end, s_eff, side="right"), N_EXPERTS - 1).astype(jnp.int32)
    local = s_eff - (item_end[e] - items_per_e[e])
    start = pad_start[e] + local * item_rows
    n_rows = jnp.clip(padded[e] - local * item_rows, 0, item_rows)
    units = jnp.where(live, n_rows // pad, 0).astype(jnp.int32)
    meta = jnp.stack([n_items, pad_end[-1] // pad]).astype(jnp.int32)
    return dest.astype(jnp.int32), last_unit, p_rows, e, start.astype(jnp.int32), units, meta


def _group_front(x, sb_fn, mk, mv, lw, tm_proj, tm_post, rows_per_batch, blk):
    u, va, q, k, vb = _proj(x, lw["g_mix"], lw["w_in"], tm_proj)
    b_out = sb_fn(q, k, vb)
    x2, h2, idx, gate = _post(x, u, va, b_out, mk, mv, lw, tm_post, rows_per_batch, blk)
    return x2, h2, idx, gate, k, vb, va


def kernel(x_prompt, x_sample, cache_sb_k, cache_sb_v, cache_mem_k, cache_mem_v, mem_prompt, g_mix, w_in, w_sp, b_sp, g_a, g_b, w_out, g_cross, g_mem, w_cq, w_ck, w_cv, w_co, g_moe, w_router, b_router, w_gu, b_gu, w_dn, b_dn, g_final):
    depth = g_mix.shape[0]
    bsz, seq, _ = x_prompt.shape
    dbsz, dseq, _ = x_sample.shape
    n_mem = mem_prompt.shape[1]
    n_p, n_s = bsz * seq, dbsz * dseq

    xp = x_prompt.reshape(n_p, D_MODEL)
    xs = x_sample.reshape(n_s, D_MODEL)
    mem = mem_prompt.reshape(bsz * n_mem, D_MODEL)
    outs = [[] for _ in range(7)]
    for l in range(depth):
        lw = {
            "g_mix": g_mix[l][None], "w_in": w_in[l].astype(BF16),
            "w_sp": w_sp[l], "b_sp": b_sp[l], "g_a": g_a[l][None], "g_b": g_b[l][None],
            "w_out": w_out[l].astype(BF16), "g_cross": g_cross[l][None],
            "w_cq": w_cq[l].astype(BF16), "w_co": w_co[l].astype(BF16), "g_moe": g_moe[l][None],
            "w_router": jnp.pad(w_router[l], ((0, 0), (0, LANES - N_EXPERTS))),
            "b_router": jnp.pad(b_router[l], (0, LANES - N_EXPERTS))[None],
        }
        mk, mv = _memkv(mem, g_mem[l][None], w_ck[l].astype(BF16), w_cv[l].astype(BF16))

        def sb_p(q, k, v):
            shp = (bsz, seq, SB_WIDTH)
            return _sb_prompt(q.reshape(shp), k.reshape(shp), v.reshape(shp)).reshape(n_p, SB_WIDTH)

        def sb_s(q, k, v):
            shp = (dbsz, dseq, SB_WIDTH)
            return _sb_sample(q.reshape(shp), k.reshape(shp), v.reshape(shp),
                              cache_sb_k, cache_sb_v, l).reshape(n_s, SB_WIDTH)

        xp2, hp, idx_p, gate_p, k_p, v_p, _ = _group_front(
            xp, sb_p, mk, mv, lw, 256, 256, seq, A_CHUNK)
        xs2, hs, idx_s, gate_s, k_s, v_s, va_s = _group_front(
            xs, sb_s, cache_mem_k[l].reshape(dbsz * n_mem, CA_WIDTH),
            cache_mem_v[l].reshape(dbsz * n_mem, CA_WIDTH), lw, n_s, n_s, dseq, min(dseq, A_CHUNK))

        top_idx = jnp.concatenate([idx_p[:, :TOP_K], idx_s[:, :TOP_K]], axis=0)
        dest, last_unit, p_rows, item_e, item_start, item_units, meta = _route(top_idx)
        x_sorted = _dispatch(last_unit, meta, dest, hp, hs, p_rows)
        y = _experts(item_e, item_start, item_units, meta, x_sorted, w_gu[l], b_gu[l], w_dn[l], b_dn[l])
        last = l == depth - 1
        dest = dest.reshape(n_p + n_s, TOP_K)
        xp = _combine(dest[:n_p], xp2, gate_p, g_final[None], y, last)
        xs = _combine(dest[n_p:], xs2, gate_s, g_final[None], y, last)

        outs[0].append(k_p.reshape(bsz, seq, SB_HEADS, SB_DH))
        outs[1].append(v_p.reshape(bsz, seq, SB_HEADS, SB_DH))
        outs[2].append(mk.reshape(bsz, n_mem, CA_HEADS, CA_DH))
        outs[3].append(mv.reshape(bsz, n_mem, CA_HEADS, CA_DH))
        outs[4].append(k_s.reshape(dbsz, dseq, SB_HEADS, SB_DH))
        outs[5].append(v_s.reshape(dbsz, dseq, SB_HEADS, SB_DH))
        outs[6].append(va_s.reshape(dbsz, dseq, A_GROUPS, A_DG))

    stacked = [jnp.stack(o, axis=0) for o in outs]
    return (xp.reshape(bsz, seq, D_MODEL), xs.reshape(dbsz, dseq, D_MODEL), *stacked)
```

```python
import functools

import jax
import jax.numpy as jnp
from jax import lax
from jax.experimental import pallas as pl
from jax.experimental.pallas import tpu as pltpu

F32 = jnp.float32
BF16 = jnp.bfloat16

D_MODEL = 2048
A_GROUPS = 8
A_DG = 128
A_WIDTH = A_GROUPS * A_DG
A_CHUNK = 128
SB_HEADS = 8
SB_DH = 128
SB_WIDTH = SB_HEADS * SB_DH
IN_WIDTH = 2 * A_WIDTH + 3 * SB_WIDTH
N_SEG = IN_WIDTH // 1024
CA_HEADS = 4
CA_DH = 128
CA_WIDTH = CA_HEADS * CA_DH
N_EXPERTS = 32
TOP_K = 4
D_EXPERT = D_MODEL
SWIGLU_LIMIT = 7.0
SWIGLU_ALPHA = 1.702
RMS_EPS = 1e-6

LANES = 128
SB_TILE = 256
SB_HEADS_PER_STEP = 8
VMEM_LIMIT = 56 * 1024 * 1024

MOE_PAD = 128
MOE_TILE = 256
MOE_BODY_ROWS = 512
MOE_ITEM_ROWS = 1280
MOE_CHUNK = 512
MOE_NCH = D_EXPERT // MOE_CHUNK
DISPATCH_TILE = 128
COMBINE_TILE = 128


def _rms(x, g):
    return x * lax.rsqrt(jnp.mean(x * x, axis=-1, keepdims=True) + RMS_EPS) * g


def _dot(a, b):
    return jnp.dot(a, b, preferred_element_type=F32)


def _dot_nt(a, b):
    return lax.dot_general(a, b, (((1,), (1,)), ((), ())), preferred_element_type=F32)


def _split_bf16(x):
    hi = x.astype(BF16)
    lo = (x - hi.astype(F32)).astype(BF16)
    return hi, lo


def _pack_bf16_pairs(x):
    c = x.shape[1] // 2
    r = x.astype(BF16).astype(F32)
    lo = lax.bitcast_convert_type(r[:, :c], jnp.uint32) >> 16
    hi = lax.bitcast_convert_type(r[:, c:], jnp.uint32)
    return lax.bitcast_convert_type(hi | lo, F32)


def _unpack_bf16_pairs(w):
    bits = lax.bitcast_convert_type(w, jnp.uint32)
    lo = lax.bitcast_convert_type(bits << 16, F32).astype(BF16)
    hi = lax.bitcast_convert_type(bits & jnp.uint32(0xFFFF0000), F32).astype(BF16)
    return lo, hi


def _params(n_axes):
    return pltpu.CompilerParams(dimension_semantics=("arbitrary",) * n_axes,
                                vmem_limit_bytes=VMEM_LIMIT)


def _const_spec(shape):
    return pl.BlockSpec(shape, lambda *_: (0,) * len(shape), pipeline_mode=pl.Buffered(1))


def _proj_kernel(x_ref, g_ref, w_ref, *out_refs):
    h = _rms(x_ref[...], g_ref[...]).astype(BF16)
    for s, o_ref in enumerate(out_refs):
        o_ref[...] = _dot(h, w_ref[:, s * 1024:(s + 1) * 1024])


def _proj(x, g, w, tm):
    n = x.shape[0]
    seg = jax.ShapeDtypeStruct((n, 1024), F32)
    return pl.pallas_call(
        _proj_kernel,
        grid=(n // tm,),
        in_specs=[pl.BlockSpec((tm, D_MODEL), lambda i: (i, 0)),
                  _const_spec((1, D_MODEL)),
                  _const_spec((D_MODEL, IN_WIDTH))],
        out_specs=[pl.BlockSpec((tm, 1024), lambda i: (i, 0))] * N_SEG,
        out_shape=[seg] * N_SEG,
        compiler_params=_params(1),
        name="proj",
    )(x, g, w)


def _log_sigmoid(x):
    neg_abs = lax.bitcast_convert_type(
        lax.bitcast_convert_type(x, jnp.uint32) | jnp.uint32(0x80000000), F32)
    return jnp.minimum(x, 0.0) - jnp.log(1.0 + jnp.exp(neg_abs))


def _sb_blocks(qbs, kbs, vbs, carries, later, mask):
    n = range(len(qbs))
    nzs = [_dot_nt(qbs[i], kbs[i]) * -(SB_DH ** -0.5) for i in n]
    ls = [_log_sigmoid(nz) for nz in nzs]
    if mask is not None:
        ls = [jnp.where(mask, l, 0.0) for l in ls]
    css = [_dot(jnp.concatenate(_split_bf16(l), axis=1), later) for l in ls]
    aas = [jnp.exp(ls[i] - nzs[i] + css[i] + carries[i]) for i in n]
    if mask is not None:
        aas = [jnp.where(mask, a, 0.0) for a in aas]
    contribs = [_dot(aas[i].astype(BF16), vbs[i]) for i in n]
    new = [carries[i] + css[i][:, 0:1] + ls[i][:, 0:1] for i in n]
    return contribs, new


def _later_matrix(n):
    r = lax.broadcasted_iota(jnp.int32, (n, n), 0)
    c = lax.broadcasted_iota(jnp.int32, (n, n), 1)
    m = (r > c).astype(BF16)
    return jnp.concatenate([m, m], axis=0)


def _sb_prompt_kernel(q_ref, k_ref, v_ref, o_ref, later_ref, carry_ref):
    qi = pl.program_id(2)
    t = SB_TILE
    later_ref[...] = _later_matrix(t)
    r = lax.broadcasted_iota(jnp.int32, (t, t), 0)
    c = lax.broadcasted_iota(jnp.int32, (t, t), 1)
    heads = [slice(h * SB_DH, (h + 1) * SB_DH) for h in range(SB_HEADS_PER_STEP)]

    def sweep(j, first):
        off = pl.multiple_of(j * t, t)
        rows = pl.ds(off, t)
        carries = [jnp.zeros((t, 1), F32) if first else carry_ref[h] for h in range(len(heads))]
        contribs, carries = _sb_blocks(
            [q_ref[:, sl].astype(BF16) for sl in heads],
            [k_ref[rows, sl].astype(BF16) for sl in heads],
            [v_ref[rows, sl].astype(BF16) for sl in heads],
            carries, later_ref[...], (c < r) if first else None)
        for h, sl in enumerate(heads):
            carry_ref[h] = carries[h]
            if first:
                o_ref[:, sl] = contribs[h]
            else:
                o_ref[:, sl] += contribs[h]

    sweep(qi, True)

    def body(step, carry):
        sweep(qi - 1 - step, False)
        return carry

    lax.fori_loop(0, qi, body, 0)


def _sb_prompt(q, k, v):
    bsz, t_len, _ = q.shape
    t = SB_TILE
    w = SB_HEADS_PER_STEP * SB_DH
    kv_spec = pl.BlockSpec((None, t_len, w), lambda b, h, i: (b, 0, h))
    return pl.pallas_call(
        _sb_prompt_kernel,
        grid=(bsz, SB_HEADS // SB_HEADS_PER_STEP, t_len // t),
        in_specs=[pl.BlockSpec((None, t, w), lambda b, h, i: (b, i, h)), kv_spec, kv_spec],
        out_specs=pl.BlockSpec((None, t, w), lambda b, h, i: (b, i, h)),
        out_shape=jax.ShapeDtypeStruct(q.shape, F32),
        scratch_shapes=[pltpu.VMEM((2 * t, t), BF16), pltpu.VMEM((SB_HEADS_PER_STEP, t, 1), F32)],
        compiler_params=_params(3),
        name="sb_prompt",
    )(q, k, v)


def _sb_sample_kernel(q_ref, kn_ref, vn_ref, ck_ref, cv_ref, o_ref, later_ref):
    t = SB_TILE
    n_new = q_ref.shape[0]
    later_ref[...] = _later_matrix(t)
    r = lax.broadcasted_iota(jnp.int32, (n_new, t), 0)
    c = lax.broadcasted_iota(jnp.int32, (n_new, t), 1)
    heads = range(SB_HEADS)
    sls = [slice(h * SB_DH, (h + 1) * SB_DH) for h in heads]
    qbs = [q_ref[:, sl].astype(BF16) for sl in sls]
    accs, carries = _sb_blocks(qbs, [kn_ref[:, sl].astype(BF16) for sl in sls],
                               [vn_ref[:, sl].astype(BF16) for sl in sls],
                               [jnp.zeros((n_new, 1), F32)] * SB_HEADS, later_ref[...], c < r)
    for j in reversed(range(ck_ref.shape[0] // (t * SB_HEADS))):
        rows = [pl.ds(j * t * SB_HEADS + h, t, stride=SB_HEADS) for h in heads]
        contribs, carries = _sb_blocks(qbs, [ck_ref[rows[h], :].astype(BF16) for h in heads],
                                       [cv_ref[rows[h], :].astype(BF16) for h in heads],
                                       carries, later_ref[...], None)
        accs = [a + cb for a, cb in zip(accs, contribs)]
    for sl, acc in zip(sls, accs):
        o_ref[:, sl] = acc


def _sb_sample(q, k_new, v_new, cache_k, cache_v, layer):
    bsz, n_new, _ = q.shape
    past = cache_k.shape[2]
    t = SB_TILE
    assert n_new <= t and past % t == 0
    pad = ((0, 0), (0, t - n_new), (0, 0))
    kn = jnp.pad(k_new, pad)
    vn = jnp.pad(v_new, pad)
    rows = lambda n: pl.BlockSpec((None, n, SB_WIDTH), lambda b: (b, 0, 0))
    depth = cache_k.shape[0]
    flat = (depth, bsz, past * SB_HEADS, SB_DH)
    cache = pl.BlockSpec((None, None, past * SB_HEADS, SB_DH), lambda b: (layer, b, 0, 0))
    return pl.pallas_call(
        _sb_sample_kernel,
        grid=(bsz,),
        in_specs=[rows(n_new), rows(t), rows(t), cache, cache],
        out_specs=rows(n_new),
        out_shape=jax.ShapeDtypeStruct(q.shape, F32),
        scratch_shapes=[pltpu.VMEM((2 * t, t), BF16)],
        compiler_params=_params(1),
        name="sb_sample",
    )(q, kn, vn, cache_k.reshape(flat), cache_v.reshape(flat))


def _memkv_kernel(m_ref, g_ref, wk_ref, wv_ref, k_ref, v_ref):
    h = _rms(m_ref[...], g_ref[...]).astype(BF16)
    k_ref[...] = _dot(h, wk_ref[...])
    v_ref[...] = _dot(h, wv_ref[...])


def _memkv(mem, g, wk, wv):
    n = mem.shape[0]
    tm = 256
    out = jax.ShapeDtypeStruct((n, CA_WIDTH), F32)
    return pl.pallas_call(
        _memkv_kernel,
        grid=(n // tm,),
        in_specs=[pl.BlockSpec((tm, D_MODEL), lambda i: (i, 0)),
                  _const_spec((1, D_MODEL)),
                  _const_spec((D_MODEL, CA_WIDTH)),
                  _const_spec((D_MODEL, CA_WIDTH))],
        out_specs=[pl.BlockSpec((tm, CA_WIDTH), lambda i: (i, 0))] * 2,
        out_shape=[out, out],
        compiler_params=_params(1),
        name="memkv",
    )(mem, g, wk, wv)


def _post_kernel(x_ref, u_ref, va_ref, b_ref, wsp_ref, bsp_ref, ga_ref, gb_ref, wout_ref,
                 gc_ref, wcq_ref, mk_ref, mv_ref, wco_ref, gm_ref, wr_ref, br_ref,
                 x2_ref, h2_ref, idx_ref, gate_ref, *, n_batches):
    tm = x_ref.shape[0]
    blk = wsp_ref.shape[1]

    r = lax.broadcasted_iota(jnp.int32, (blk, blk), 0)
    c = lax.broadcasted_iota(jnp.int32, (blk, blk), 1)
    tri = (c <= r).astype(F32)
    cols = []
    for g in range(A_GROUPS):
        w = (wsp_ref[g] * tri).astype(BF16)
        bias = bsp_ref[:, g:g + 1]
        sl = slice(g * A_DG, (g + 1) * A_DG)
        chunks = [slice(ch * blk, (ch + 1) * blk) for ch in range(tm // blk)]
        rows = []
        for pair in [chunks[p:p + 2] for p in range(0, len(chunks), 2)]:
            v = [va_ref[rs, sl].astype(BF16) for rs in pair]
            mixed = _dot(w, v[0] if len(v) == 1 else jnp.concatenate(v, axis=1))
            for p, rs in enumerate(pair):
                rows.append(u_ref[rs, sl] * (mixed[:, p * A_DG:(p + 1) * A_DG] + bias))
        cols.append(rows[0] if len(rows) == 1 else jnp.concatenate(rows, axis=0))
    a_out = jnp.concatenate(cols, axis=1)

    a_n = _rms(a_out, ga_ref[...]).astype(BF16)
    b_n = _rms(b_ref[...], gb_ref[...]).astype(BF16)
    x1 = x_ref[...] + _dot(a_n, wout_ref[0:A_WIDTH, :]) + _dot(b_n, wout_ref[A_WIDTH:, :])

    hc = _rms(x1, gc_ref[...]).astype(BF16)
    q = _dot(hc, wcq_ref[...])
    rows_b = tm // n_batches
    mem_b = mk_ref.shape[0] // n_batches
    per_batch = []
    for b in range(n_batches):
        rs = slice(b * rows_b, (b + 1) * rows_b)
        ms = slice(b * mem_b, (b + 1) * mem_b)
        heads = []
        for h in range(CA_HEADS):
            sl = slice(h * CA_DH, (h + 1) * CA_DH)
            s = _dot_nt(q[rs, sl].astype(BF16), mk_ref[ms, sl].astype(BF16)) * (CA_DH ** -0.5)
            p = jnp.exp(s - jnp.max(s, axis=-1, keepdims=True))
            p = p / jnp.sum(p, axis=-1, keepdims=True)
            heads.append(_dot(p.astype(BF16), mv_ref[ms, sl].astype(BF16)))
        per_batch.append(jnp.concatenate(heads, axis=1))
    o = (per_batch[0] if n_batches == 1 else jnp.concatenate(per_batch, axis=0)).astype(BF16)
    x2 = x1 + _dot(o, wco_ref[...])
    x2_ref[...] = x2

    h2 = _rms(x2, gm_ref[...])
    h2_ref[...] = _pack_bf16_pairs(h2)
    hh, hl = _split_bf16(h2)
    wh, wl = _split_bf16(wr_ref[...])
    both = _dot(hh, jnp.concatenate([wh, wl], axis=1))
    logits = both[:, :LANES] + both[:, LANES:] + _dot(hl, wh) + br_ref[...]
    lane = lax.broadcasted_iota(jnp.int32, (tm, LANES), 1).astype(F32)
    cur = jnp.where(lane < N_EXPERTS, logits, -jnp.inf)
    vals, idxs = [], []
    for _ in range(TOP_K):
        m = jnp.max(cur, axis=-1, keepdims=True)
        i = jnp.min(jnp.where(cur == m, lane, float(LANES)), axis=-1, keepdims=True)
        vals.append(m)
        idxs.append(i)
        cur = jnp.where(lane == i, -jnp.inf, cur)
    es = [jnp.exp(v - vals[0]) for v in vals]
    denom = es[0] + es[1] + es[2] + es[3]
    idx_out = jnp.zeros((tm, LANES), F32)
    gate_out = jnp.zeros((tm, LANES), F32)
    for k in range(TOP_K):
        idx_out = jnp.where(lane == k, idxs[k], idx_out)
        gate_out = jnp.where(lane == k, es[k] / denom, gate_out)
    idx_ref[...] = idx_out.astype(jnp.int32)
    gate_ref[...] = gate_out


def _post(x, u, va, b_out, mk, mv, lw, tm, rows_per_batch, blk):
    n = x.shape[0]
    n_mem = mk.shape[0] // (n // rows_per_batch)
    row = lambda w: pl.BlockSpec((tm, w), lambda i: (i, 0))
    if tm >= rows_per_batch:
        n_batches = tm // rows_per_batch
        mem = pl.BlockSpec((n_batches * n_mem, CA_WIDTH), lambda i: (i, 0))
    else:
        n_batches = 1
        tiles_per_batch = rows_per_batch // tm
        mem = pl.BlockSpec((n_mem, CA_WIDTH), lambda i: (i // tiles_per_batch, 0))
    wsp = lw["w_sp"][:, :blk, :blk]
    bsp = lw["b_sp"][:, :blk].T
    return pl.pallas_call(
        functools.partial(_post_kernel, n_batches=n_batches),
        grid=(n // tm,),
        in_specs=[row(D_MODEL), row(A_WIDTH), row(A_WIDTH), row(SB_WIDTH),
                  _const_spec((A_GROUPS, blk, blk)), _const_spec((blk, A_GROUPS)),
                  _const_spec((1, A_WIDTH)), _const_spec((1, SB_WIDTH)),
                  _const_spec((2 * A_WIDTH, D_MODEL)),
                  _const_spec((1, D_MODEL)), _const_spec((D_MODEL, CA_WIDTH)),
                  mem, mem, _const_spec((CA_WIDTH, D_MODEL)),
                  _const_spec((1, D_MODEL)), _const_spec((D_MODEL, LANES)), _const_spec((1, LANES))],
        out_specs=[row(D_MODEL), row(D_MODEL // 2), row(LANES), row(LANES)],
        out_shape=[jax.ShapeDtypeStruct((n, D_MODEL), F32), jax.ShapeDtypeStruct((n, D_MODEL // 2), F32),
                   jax.ShapeDtypeStruct((n, LANES), jnp.int32), jax.ShapeDtypeStruct((n, LANES), F32)],
        compiler_params=_params(1),
        name="post",
    )(x, u, va, b_out, wsp, bsp, lw["g_a"], lw["g_b"], lw["w_out"], lw["g_cross"], lw["w_cq"],
      mk, mv, lw["w_co"], lw["g_moe"], lw["w_router"], lw["b_router"])


def _dispatch_kernel(last_unit, meta, dest_ref, hp_ref, hs_ref, xs_hbm, stage, zero, sem, sem_zero, *,
                     n_prompt_tiles, n_tiles):
    i = pl.program_id(0)
    td = stage.shape[1]
    pad = MOE_PAD
    slot = i % 2

    def zero_copy(row0):
        return pltpu.make_async_copy(zero, xs_hbm.at[pl.ds(pl.multiple_of(row0, pad), pad), :], sem_zero)

    @pl.when(i == 0)
    def _():
        zero[...] = jnp.zeros(zero.shape, F32)

        def fill(wait):
            def act(copy):
                if wait:
                    copy.wait()
                else:
                    copy.start()

            def expert_body(e, carry):
                @pl.when(last_unit[e] >= 0)
                def _():
                    act(zero_copy(last_unit[e]))
                return carry

            def tail_body(t, carry):
                act(zero_copy(t * pad))
                return carry

            lax.fori_loop(0, N_EXPERTS, expert_body, 0)
            lax.fori_loop(meta[1], xs_hbm.shape[0] // pad, tail_body, 0)

        fill(False)
        fill(True)

    def wait_slot(slot):
        for _ in range(TOP_K):
            pltpu.make_async_copy(stage.at[slot], xs_hbm.at[pl.ds(0, td), :], sem.at[slot]).wait()

    @pl.when(i >= 2)
    def _():
        wait_slot(slot)

    @pl.when(i < n_prompt_tiles)
    def _():
        stage[slot] = hp_ref[...]

    @pl.when(i >= n_prompt_tiles)
    def _():
        stage[slot] = hs_ref[...]

    def issue(r, carry):
        for k in range(TOP_K):
            pltpu.make_async_copy(stage.at[slot, pl.ds(r, 1), :],
                                  xs_hbm.at[pl.ds(dest_ref[0, 0, r * TOP_K + k], 1), :],
                                  sem.at[slot]).start()
        return carry

    lax.fori_loop(0, td, issue, 0, unroll=4)

    @pl.when(i == n_tiles - 1)
    def _():
        wait_slot(slot)
        if n_tiles > 1:
            wait_slot(1 - slot)


def _dispatch(last_unit, meta, dest, h_prompt, h_sample, p_rows):
    td = DISPATCH_TILE
    n_p, width = h_prompt.shape
    n_s = h_sample.shape[0]
    assert n_p % td == 0 and n_s % td == 0
    n_prompt_tiles, n_tiles = n_p // td, (n_p + n_s) // td
    grid_spec = pltpu.PrefetchScalarGridSpec(
        num_scalar_prefetch=2,
        grid=(n_tiles,),
        in_specs=[pl.BlockSpec((1, 1, td * TOP_K), lambda i, *_: (i, 0, 0), memory_space=pltpu.SMEM),
                  pl.BlockSpec((td, width), lambda i, *_: (jnp.minimum(i, n_prompt_tiles - 1), 0)),
                  pl.BlockSpec((td, width), lambda i, *_: (jnp.maximum(i - n_prompt_tiles, 0), 0))],
        out_specs=pl.BlockSpec(memory_space=pl.ANY),
        scratch_shapes=[pltpu.VMEM((2, td, width), F32), pltpu.VMEM((MOE_PAD, width), F32),
                        pltpu.SemaphoreType.DMA((2,)), pltpu.SemaphoreType.DMA],
    )
    return pl.pallas_call(
        functools.partial(_dispatch_kernel, n_prompt_tiles=n_prompt_tiles, n_tiles=n_tiles),
        grid_spec=grid_spec,
        out_shape=jax.ShapeDtypeStruct((p_rows, width), F32),
        compiler_params=_params(1),
        name="moe_dispatch",
    )(last_unit, meta, dest.reshape(n_tiles, 1, td * TOP_K), h_prompt, h_sample)


def _expert_kernel(item_e, item_start, item_units, meta,
                   xs_hbm, wg_ref, wl_ref, wd_ref, bg_ref, bl_ref, bd_ref,
                   y_hbm, stage, x_buf, y_buf, sem_in, sem_out):
    s = pl.program_id(0)
    j = pl.program_id(1)
    n_slots = pl.num_programs(0)
    units = item_units[s]
    start = item_start[s]
    pad = MOE_PAD
    assert MOE_NCH >= 2

    def unit_rows(t):
        return pl.ds(pl.multiple_of(t * pad, pad), pad)

    def hbm_rows(first_row, t):
        return pl.ds(pl.multiple_of(first_row + t * pad, pad), pad)

    def copy_in(first_row, t):
        return pltpu.make_async_copy(xs_hbm.at[hbm_rows(first_row, t), :], stage.at[unit_rows(t), :], sem_in)

    def copy_out(t):
        return pltpu.make_async_copy(y_buf.at[unit_rows(t), :], y_hbm.at[hbm_rows(start, t), :], sem_out)

    def for_range(lo, hi, fn):
        def body(t, carry):
            fn(t)
            return carry
        lax.fori_loop(lo, hi, body, 0)

    @pl.when(jnp.logical_and(s == 0, j == 0))
    def _():
        for_range(0, units, lambda t: copy_in(start, t).start())

    @pl.when(jnp.logical_and(j == MOE_NCH - 1, s + 1 < n_slots))
    def _():
        nxt = jnp.minimum(s + 1, n_slots - 1)
        for_range(0, item_units[nxt], lambda t: copy_in(item_start[nxt], t).start())

    @pl.when(j == 0)
    def _():
        for_range(0, units, lambda t: copy_in(start, t).wait())

    def ffn(row0, m, first):
        wg = wg_ref[...].astype(BF16)
        wl = wl_ref[...].astype(BF16)
        wd = wd_ref[...].astype(BF16)
        for off in range(0, m, MOE_TILE):
            rows = pl.ds(pl.multiple_of(row0 + off, pad), min(MOE_TILE, m - off))
            if first:
                xb = jnp.concatenate(_unpack_bf16_pairs(stage[rows, :]), axis=1)
                x_buf[rows, :] = xb
            else:
                xb = x_buf[rows, :]
            glu = jnp.minimum(_dot(xb, wg) + bg_ref[...], SWIGLU_LIMIT)
            lin = jnp.clip(_dot(xb, wl) + bl_ref[...], -SWIGLU_LIMIT, SWIGLU_LIMIT)
            act = glu * jax.nn.sigmoid(SWIGLU_ALPHA * glu) * (lin + 1.0)
            down = _dot(act.astype(BF16), wd)
            if first:
                y_buf[rows, :] = bd_ref[...] + down
            else:
                y_buf[rows, :] += down

        @pl.when(j == MOE_NCH - 1)
        def _():
            for u in range(m // pad):
                copy_out(row0 // pad + u).start()

    def run(first):
        big = MOE_BODY_ROWS
        n_big = units // (big // pad)
        for_range(0, n_big, lambda t: ffn(t * big, big, first))
        base = n_big * big
        rem = units * pad - base
        m = big // 2
        while m >= pad:
            take = (rem & m) != 0

            @pl.when(take)
            def _(base=base, m=m):
                ffn(base, m, first)

            base = base + jnp.where(take, m, 0)
            m //= 2

    @pl.when(jnp.logical_and(units > 0, j == 0))
    def _():
        run(True)

    @pl.when(jnp.logical_and(units > 0, j > 0))
    def _():
        run(False)

    @pl.when(jnp.logical_and(j == MOE_NCH - 1, units > 0))
    def _store():
        for_range(0, units, lambda t: copy_out(t).wait())

    @pl.when(jnp.logical_and(s == pl.num_programs(0) - 1, j == MOE_NCH - 1))
    def _fill_tail():
        y_buf[pl.ds(0, pad), :] = jnp.zeros((pad, D_MODEL), F32)

        def tail_copy(t):
            dst = y_hbm.at[pl.ds(pl.multiple_of(t * pad, pad), pad), :]
            return pltpu.make_async_copy(y_buf.at[pl.ds(0, pad), :], dst, sem_out)

        n_units = y_hbm.shape[0] // pad
        for_range(meta[1], n_units, lambda t: tail_copy(t).start())
        for_range(meta[1], n_units, lambda t: tail_copy(t).wait())


def _experts(item_e, item_start, item_units, meta, xs, w_gu, b_gu, w_dn, b_dn):
    n_slots = item_e.shape[0]
    p_rows = xs.shape[0]
    c = MOE_CHUNK

    def chunk(s, j, ie, ist, iu, m):
        return jnp.where(s < m[0], j, MOE_NCH - 1)

    grid_spec = pltpu.PrefetchScalarGridSpec(
        num_scalar_prefetch=4,
        grid=(n_slots, MOE_NCH),
        in_specs=[
            pl.BlockSpec(memory_space=pl.ANY),
            pl.BlockSpec((None, D_MODEL, c), lambda s, j, ie, *a: (ie[s], 0, chunk(s, j, ie, *a))),
            pl.BlockSpec((None, D_MODEL, c), lambda s, j, ie, *a: (ie[s], 0, MOE_NCH + chunk(s, j, ie, *a))),
            pl.BlockSpec((None, c, D_MODEL), lambda s, j, ie, *a: (ie[s], chunk(s, j, ie, *a), 0)),
            pl.BlockSpec((None, 1, c), lambda s, j, ie, *a: (ie[s], 0, chunk(s, j, ie, *a))),
            pl.BlockSpec((None, 1, c), lambda s, j, ie, *a: (ie[s], 0, MOE_NCH + chunk(s, j, ie, *a))),
            pl.BlockSpec((None, 1, D_MODEL), lambda s, j, ie, *a: (ie[s], 0, 0)),
        ],
        out_specs=pl.BlockSpec(memory_space=pl.ANY),
        scratch_shapes=[
            pltpu.VMEM((MOE_ITEM_ROWS, D_MODEL // 2), F32),
            pltpu.VMEM((MOE_ITEM_ROWS, D_MODEL), BF16),
            pltpu.VMEM((MOE_ITEM_ROWS, D_MODEL), F32),
            pltpu.SemaphoreType.DMA,
            pltpu.SemaphoreType.DMA,
        ],
    )
    return pl.pallas_call(
        _expert_kernel,
        grid_spec=grid_spec,
        out_shape=jax.ShapeDtypeStruct((p_rows, D_MODEL), F32),
        compiler_params=_params(2),
        name="moe_experts",
    )(item_e, item_start, item_units, meta, xs, w_gu, w_gu, w_dn,
      b_gu.reshape(N_EXPERTS, 1, 2 * D_EXPERT), b_gu.reshape(N_EXPERTS, 1, 2 * D_EXPERT),
      b_dn.reshape(N_EXPERTS, 1, D_MODEL))


def _combine_kernel(dest_ref, next_dest_ref, x_ref, gate_ref, gf_ref, y_hbm, o_ref, buf, sem, *,
                    final_norm):
    i = pl.program_id(0)
    tc = x_ref.shape[0]
    slot = i % 2

    def issue(ref, slot):
        for r in range(tc):
            for k in range(TOP_K):
                pltpu.make_async_copy(y_hbm.at[pl.ds(ref[0, 0, r * TOP_K + k], 1), :],
                                      buf.at[slot, pl.ds(k * tc + r, 1), :], sem.at[slot]).start()

    @pl.when(i == 0)
    def _():
        issue(dest_ref, 0)

    @pl.when(i + 1 < pl.num_programs(0))
    def _():
        issue(next_dest_ref, 1 - slot)

    pltpu.make_async_copy(y_hbm.at[pl.ds(0, TOP_K * tc), :], buf.at[slot], sem.at[slot]).wait()
    gate = gate_ref[...]
    acc = x_ref[...]
    for k in range(TOP_K):
        acc = acc + gate[:, k:k + 1] * buf[slot, k * tc:(k + 1) * tc, :]
    o_ref[...] = _rms(acc, gf_ref[...]) if final_norm else acc


def _combine(dest, x2, gate, g_final, y, final_norm):
    tc = COMBINE_TILE
    n = x2.shape[0]
    n_tiles = n // tc
    dest = dest.reshape(n_tiles, 1, tc * TOP_K)
    return pl.pallas_call(
        functools.partial(_combine_kernel, final_norm=final_norm),
        grid=(n_tiles,),
        in_specs=[pl.BlockSpec((1, 1, tc * TOP_K), lambda i: (i, 0, 0), memory_space=pltpu.SMEM),
                  pl.BlockSpec((1, 1, tc * TOP_K), lambda i: (jnp.minimum(i + 1, n_tiles - 1), 0, 0),
                               memory_space=pltpu.SMEM),
                  pl.BlockSpec((tc, D_MODEL), lambda i: (i, 0)),
                  pl.BlockSpec((tc, LANES), lambda i: (i, 0)),
                  _const_spec((1, D_MODEL)),
                  pl.BlockSpec(memory_space=pl.ANY)],
        out_specs=pl.BlockSpec((tc, D_MODEL), lambda i: (i, 0)),
        out_shape=jax.ShapeDtypeStruct((n, D_MODEL), F32),
        scratch_shapes=[pltpu.VMEM((2, TOP_K * tc, D_MODEL), F32), pltpu.SemaphoreType.DMA((2,))],
        compiler_params=_params(1),
        name="moe_combine",
    )(dest, dest, x2, gate, g_final, y)


def _route(top_idx):
    n_tok = top_idx.shape[0]
    n_assign = n_tok * TOP_K
    pad, item_rows = MOE_PAD, MOE_ITEM_ROWS
    p_rows = -(-(n_assign + N_EXPERTS * (pad - 1)) // pad) * pad
    n_slots = N_EXPERTS + -(-p_rows // item_rows)
    experts = jnp.arange(N_EXPERTS, dtype=jnp.int32)

    flat_e = top_idx.reshape(-1)
    onehot = (top_idx[:, :, None] == experts[None, None, :]).astype(jnp.int32)
    chosen = jnp.sum(onehot, axis=1)
    running = jnp.cumsum(chosen, axis=0)
    rank = jnp.sum((running - chosen)[:, None, :] * onehot, axis=2).reshape(-1)
    counts = running[-1]
    padded = (counts + pad - 1) // pad * pad
    pad_end = jnp.cumsum(padded)
    pad_start = pad_end - padded
    dest = pad_start[flat_e] + rank
    last_unit = jnp.where(padded > 0, pad_end - pad, -1).astype(jnp.int32)

    items_per_e = (padded + item_rows - 1) // item_rows
    item_end = jnp.cumsum(items_per_e)
    n_items = item_end[-1]
    slot = jnp.arange(n_slots, dtype=jnp.int32)
    live = slot < n_items
    s_eff = jnp.minimum(slot, n_items - 1)
    e = jnp.minimum(jnp.searchsorted(item_end, s_eff, side="right"), N_EXPERTS - 1).astype(jnp.int32)
    local = s_eff - (item_end[e] - items_per_e[e])
    start = pad_start[e] + local * item_rows
    n_rows = jnp.clip(padded[e] - local * item_rows, 0, item_rows)
    units = jnp.where(live, n_rows // pad, 0).astype(jnp.int32)
    meta = jnp.stack([n_items, pad_end[-1] // pad]).astype(jnp.int32)
    return dest.astype(jnp.int32), last_unit, p_rows, e, start.astype(jnp.int32), units, meta


def _group_front(x, sb_fn, mk, mv, lw, tm_proj, tm_post, rows_per_batch, blk):
    u, va, q, k, vb = _proj(x, lw["g_mix"], lw["w_in"], tm_proj)
    b_out = sb_fn(q, k, vb)
    x2, h2, idx, gate = _post(x, u, va, b_out, mk, mv, lw, tm_post, rows_per_batch, blk)
    return x2, h2, idx, gate, k, vb, va


def kernel(x_prompt, x_sample, cache_sb_k, cache_sb_v, cache_mem_k, cache_mem_v, mem_prompt, g_mix, w_in, w_sp, b_sp, g_a, g_b, w_out, g_cross, g_mem, w_cq, w_ck, w_cv, w_co, g_moe, w_router, b_router, w_gu, b_gu, w_dn, b_dn, g_final):
    depth = g_mix.shape[0]
    bsz, seq, _ = x_prompt.shape
    dbsz, dseq, _ = x_sample.shape
    n_mem = mem_prompt.shape[1]
    n_p, n_s = bsz * seq, dbsz * dseq

    xp = x_prompt.reshape(n_p, D_MODEL)
    xs = x_sample.reshape(n_s, D_MODEL)
    mem = mem_prompt.reshape(bsz * n_mem, D_MODEL)
    outs = [[] for _ in range(7)]
    for l in range(depth):
        lw = {
            "g_mix": g_mix[l][None], "w_in": w_in[l].astype(BF16),
            "w_sp": w_sp[l], "b_sp": b_sp[l], "g_a": g_a[l][None], "g_b": g_b[l][None],
            "w_out": w_out[l].astype(BF16), "g_cross": g_cross[l][None],
            "w_cq": w_cq[l].astype(BF16), "w_co": w_co[l].astype(BF16), "g_moe": g_moe[l][None],
            "w_router": jnp.pad(w_router[l], ((0, 0), (0, LANES - N_EXPERTS))),
            "b_router": jnp.pad(b_router[l], (0, LANES - N_EXPERTS))[None],
        }
        mk, mv = _memkv(mem, g_mem[l][None], w_ck[l].astype(BF16), w_cv[l].astype(BF16))

        def sb_p(q, k, v):
            shp = (bsz, seq, SB_WIDTH)
            return _sb_prompt(q.reshape(shp), k.reshape(shp), v.reshape(shp)).reshape(n_p, SB_WIDTH)

        def sb_s(q, k, v):
            shp = (dbsz, dseq, SB_WIDTH)
            return _sb_sample(q.reshape(shp), k.reshape(shp), v.reshape(shp),
                              cache_sb_k, cache_sb_v, l).reshape(n_s, SB_WIDTH)

        xp2, hp, idx_p, gate_p, k_p, v_p, _ = _group_front(
            xp, sb_p, mk, mv, lw, 256, 256, seq, A_CHUNK)
        xs2, hs, idx_s, gate_s, k_s, v_s, va_s = _group_front(
            xs, sb_s, cache_mem_k[l].reshape(dbsz * n_mem, CA_WIDTH),
            cache_mem_v[l].reshape(dbsz * n_mem, CA_WIDTH), lw, n_s, n_s, dseq, min(dseq, A_CHUNK))

        top_idx = jnp.concatenate([idx_p[:, :TOP_K], idx_s[:, :TOP_K]], axis=0)
        dest, last_unit, p_rows, item_e, item_start, item_units, meta = _route(top_idx)
        x_sorted = _dispatch(last_unit, meta, dest, hp, hs, p_rows)
        y = _experts(item_e, item_start, item_units, meta, x_sorted, w_gu[l], b_gu[l], w_dn[l], b_dn[l])
        last = l == depth - 1
        dest = dest.reshape(n_p + n_s, TOP_K)
        xp = _combine(dest[:n_p], xp2, gate_p, g_final[None], y, last)
        xs = _combine(dest[n_p:], xs2, gate_s, g_final[None], y, last)

        outs[0].append(k_p.reshape(bsz, seq, SB_HEADS, SB_DH))
        outs[1].append(v_p.reshape(bsz, seq, SB_HEADS, SB_DH))
        outs[2].append(mk.reshape(bsz, n_mem, CA_HEADS, CA_DH))
        outs[3].append(mv.reshape(bsz, n_mem, CA_HEADS, CA_DH))
        outs[4].append(k_s.reshape(dbsz, dseq, SB_HEADS, SB_DH))
        outs[5].append(v_s.reshape(dbsz, dseq, SB_HEADS, SB_DH))
        outs[6].append(va_s.reshape(dbsz, dseq, A_GROUPS, A_DG))

    stacked = [jnp.stack(o, axis=0) for o in outs]
    return (xp.reshape(bsz, seq, D_MODEL), xs.reshape(dbsz, dseq, D_MODEL), *stacked)
```

```python
import functools

import jax
import jax.numpy as jnp
from jax import lax
from jax.experimental import pallas as pl
from jax.experimental.pallas import tpu as pltpu

F32 = jnp.float32
BF16 = jnp.bfloat16

D_MODEL = 2048
A_GROUPS = 8
A_DG = 128
A_WIDTH = A_GROUPS * A_DG
A_CHUNK = 128
SB_HEADS = 8
SB_DH = 128
SB_WIDTH = SB_HEADS * SB_DH
IN_WIDTH = 2 * A_WIDTH + 3 * SB_WIDTH
N_SEG = IN_WIDTH // 1024
CA_HEADS = 4
CA_DH = 128
CA_WIDTH = CA_HEADS * CA_DH
N_EXPERTS = 32
TOP_K = 4
D_EXPERT = D_MODEL
SWIGLU_LIMIT = 7.0
SWIGLU_ALPHA = 1.702
RMS_EPS = 1e-6

LANES = 128
SB_TILE = 256
SB_HEADS_PER_STEP = 8
VMEM_LIMIT = 56 * 1024 * 1024

MOE_PAD = 128
MOE_TILE = 256
MOE_BODY_ROWS = 1024
MOE_ITEM_ROWS = 1280
MOE_CHUNK = 512
MOE_NCH = D_EXPERT // MOE_CHUNK
DISPATCH_TILE = 128
COMBINE_TILE = 128


def _rms(x, g):
    return x * lax.rsqrt(jnp.mean(x * x, axis=-1, keepdims=True) + RMS_EPS) * g


def _dot(a, b):
    return jnp.dot(a, b, preferred_element_type=F32)


def _dot_nt(a, b):
    return lax.dot_general(a, b, (((1,), (1,)), ((), ())), preferred_element_type=F32)


def _split_bf16(x):
    hi = x.astype(BF16)
    lo = (x - hi.astype(F32)).astype(BF16)
    return hi, lo


def _pack_bf16_pairs(x):
    c = x.shape[1] // 2
    r = x.astype(BF16).astype(F32)
    lo = lax.bitcast_convert_type(r[:, :c], jnp.uint32) >> 16
    hi = lax.bitcast_convert_type(r[:, c:], jnp.uint32)
    return lax.bitcast_convert_type(hi | lo, F32)


def _unpack_bf16_pairs(w):
    bits = lax.bitcast_convert_type(w, jnp.uint32)
    lo = lax.bitcast_convert_type(bits << 16, F32).astype(BF16)
    hi = lax.bitcast_convert_type(bits & jnp.uint32(0xFFFF0000), F32).astype(BF16)
    return lo, hi


def _params(n_axes):
    return pltpu.CompilerParams(dimension_semantics=("arbitrary",) * n_axes,
                                vmem_limit_bytes=VMEM_LIMIT)


def _const_spec(shape):
    return pl.BlockSpec(shape, lambda *_: (0,) * len(shape), pipeline_mode=pl.Buffered(1))


def _proj_kernel(x_ref, g_ref, w_ref, *out_refs):
    h = _rms(x_ref[...], g_ref[...]).astype(BF16)
    for s, o_ref in enumerate(out_refs):
        o_ref[...] = _dot(h, w_ref[:, s * 1024:(s + 1) * 1024])


def _proj(x, g, w, tm):
    n = x.shape[0]
    seg = jax.ShapeDtypeStruct((n, 1024), F32)
    return pl.pallas_call(
        _proj_kernel,
        grid=(n // tm,),
        in_specs=[pl.BlockSpec((tm, D_MODEL), lambda i: (i, 0)),
                  _const_spec((1, D_MODEL)),
                  _const_spec((D_MODEL, IN_WIDTH))],
        out_specs=[pl.BlockSpec((tm, 1024), lambda i: (i, 0))] * N_SEG,
        out_shape=[seg] * N_SEG,
        compiler_params=_params(1),
        name="proj",
    )(x, g, w)


def _log_sigmoid(x):
    neg_abs = lax.bitcast_convert_type(
        lax.bitcast_convert_type(x, jnp.uint32) | jnp.uint32(0x80000000), F32)
    return jnp.minimum(x, 0.0) - jnp.log(1.0 + jnp.exp(neg_abs))


def _sb_blocks(qbs, kbs, vbs, carries, later, mask):
    n = range(len(qbs))
    nzs = [_dot_nt(qbs[i], kbs[i]) * -(SB_DH ** -0.5) for i in n]
    ls = [_log_sigmoid(nz) for nz in nzs]
    if mask is not None:
        ls = [jnp.where(mask, l, 0.0) for l in ls]
    css = [_dot(jnp.concatenate(_split_bf16(l), axis=1), later) for l in ls]
    aas = [jnp.exp(ls[i] - nzs[i] + css[i] + carries[i]) for i in n]
    if mask is not None:
        aas = [jnp.where(mask, a, 0.0) for a in aas]
    contribs = [_dot(aas[i].astype(BF16), vbs[i]) for i in n]
    new = [carries[i] + css[i][:, 0:1] + ls[i][:, 0:1] for i in n]
    return contribs, new


def _later_matrix(n):
    r = lax.broadcasted_iota(jnp.int32, (n, n), 0)
    c = lax.broadcasted_iota(jnp.int32, (n, n), 1)
    m = (r > c).astype(BF16)
    return jnp.concatenate([m, m], axis=0)


def _sb_prompt_kernel(q_ref, k_ref, v_ref, o_ref, later_ref, carry_ref):
    qi = pl.program_id(2)
    t = SB_TILE
    later_ref[...] = _later_matrix(t)
    r = lax.broadcasted_iota(jnp.int32, (t, t), 0)
    c = lax.broadcasted_iota(jnp.int32, (t, t), 1)
    heads = [slice(h * SB_DH, (h + 1) * SB_DH) for h in range(SB_HEADS_PER_STEP)]

    def sweep(j, first):
        off = pl.multiple_of(j * t, t)
        rows = pl.ds(off, t)
        carries = [jnp.zeros((t, 1), F32) if first else carry_ref[h] for h in range(len(heads))]
        contribs, carries = _sb_blocks(
            [q_ref[:, sl].astype(BF16) for sl in heads],
            [k_ref[rows, sl].astype(BF16) for sl in heads],
            [v_ref[rows, sl].astype(BF16) for sl in heads],
            carries, later_ref[...], (c < r) if first else None)
        for h, sl in enumerate(heads):
            carry_ref[h] = carries[h]
            if first:
                o_ref[:, sl] = contribs[h]
            else:
                o_ref[:, sl] += contribs[h]

    sweep(qi, True)

    def body(step, carry):
        sweep(qi - 1 - step, False)
        return carry

    lax.fori_loop(0, qi, body, 0)


def _sb_prompt(q, k, v):
    bsz, t_len, _ = q.shape
    t = SB_TILE
    w = SB_HEADS_PER_STEP * SB_DH
    kv_spec = pl.BlockSpec((None, t_len, w), lambda b, h, i: (b, 0, h))
    return pl.pallas_call(
        _sb_prompt_kernel,
        grid=(bsz, SB_HEADS // SB_HEADS_PER_STEP, t_len // t),
        in_specs=[pl.BlockSpec((None, t, w), lambda b, h, i: (b, i, h)), kv_spec, kv_spec],
        out_specs=pl.BlockSpec((None, t, w), lambda b, h, i: (b, i, h)),
        out_shape=jax.ShapeDtypeStruct(q.shape, F32),
        scratch_shapes=[pltpu.VMEM((2 * t, t), BF16), pltpu.VMEM((SB_HEADS_PER_STEP, t, 1), F32)],
        compiler_params=_params(3),
        name="sb_prompt",
    )(q, k, v)


def _sb_sample_kernel(q_ref, kn_ref, vn_ref, ck_ref, cv_ref, o_ref, later_ref):
    t = SB_TILE
    n_new = q_ref.shape[0]
    later_ref[...] = _later_matrix(t)
    r = lax.broadcasted_iota(jnp.int32, (n_new, t), 0)
    c = lax.broadcasted_iota(jnp.int32, (n_new, t), 1)
    heads = range(SB_HEADS)
    sls = [slice(h * SB_DH, (h + 1) * SB_DH) for h in heads]
    qbs = [q_ref[:, sl].astype(BF16) for sl in sls]
    accs, carries = _sb_blocks(qbs, [kn_ref[:, sl].astype(BF16) for sl in sls],
                               [vn_ref[:, sl].astype(BF16) for sl in sls],
                               [jnp.zeros((n_new, 1), F32)] * SB_HEADS, later_ref[...], c < r)
    for j in reversed(range(ck_ref.shape[0] // (t * SB_HEADS))):
        rows = [pl.ds(j * t * SB_HEADS + h, t, stride=SB_HEADS) for h in heads]
        contribs, carries = _sb_blocks(qbs, [ck_ref[rows[h], :].astype(BF16) for h in heads],
                                       [cv_ref[rows[h], :].astype(BF16) for h in heads],
                                       carries, later_ref[...], None)
        accs = [a + cb for a, cb in zip(accs, contribs)]
    for sl, acc in zip(sls, accs):
        o_ref[:, sl] = acc


def _sb_sample(q, k_new, v_new, cache_k, cache_v, layer):
    bsz, n_new, _ = q.shape
    past = cache_k.shape[2]
    t = SB_TILE
    assert n_new <= t and past % t == 0
    pad = ((0, 0), (0, t - n_new), (0, 0))
    kn = jnp.pad(k_new, pad)
    vn = jnp.pad(v_new, pad)
    rows = lambda n: pl.BlockSpec((None, n, SB_WIDTH), lambda b: (b, 0, 0))
    depth = cache_k.shape[0]
    flat = (depth, bsz, past * SB_HEADS, SB_DH)
    cache = pl.BlockSpec((None, None, past * SB_HEADS, SB_DH), lambda b: (layer, b, 0, 0))
    return pl.pallas_call(
        _sb_sample_kernel,
        grid=(bsz,),
        in_specs=[rows(n_new), rows(t), rows(t), cache, cache],
        out_specs=rows(n_new),
        out_shape=jax.ShapeDtypeStruct(q.shape, F32),
        scratch_shapes=[pltpu.VMEM((2 * t, t), BF16)],
        compiler_params=_params(1),
        name="sb_sample",
    )(q, kn, vn, cache_k.reshape(flat), cache_v.reshape(flat))


def _memkv_kernel(m_ref, g_ref, wk_ref, wv_ref, k_ref, v_ref):
    h = _rms(m_ref[...], g_ref[...]).astype(BF16)
    k_ref[...] = _dot(h, wk_ref[...])
    v_ref[...] = _dot(h, wv_ref[...])


def _memkv(mem, g, wk, wv):
    n = mem.shape[0]
    tm = 256
    out = jax.ShapeDtypeStruct((n, CA_WIDTH), F32)
    return pl.pallas_call(
        _memkv_kernel,
        grid=(n // tm,),
        in_specs=[pl.BlockSpec((tm, D_MODEL), lambda i: (i, 0)),
                  _const_spec((1, D_MODEL)),
                  _const_spec((D_MODEL, CA_WIDTH)),
                  _const_spec((D_MODEL, CA_WIDTH))],
        out_specs=[pl.BlockSpec((tm, CA_WIDTH), lambda i: (i, 0))] * 2,
        out_shape=[out, out],
        compiler_params=_params(1),
        name="memkv",
    )(mem, g, wk, wv)


def _post_kernel(x_ref, u_ref, va_ref, b_ref, wsp_ref, bsp_ref, ga_ref, gb_ref, wout_ref,
                 gc_ref, wcq_ref, mk_ref, mv_ref, wco_ref, gm_ref, wr_ref, br_ref,
                 x2_ref, h2_ref, idx_ref, gate_ref, *, n_batches):
    tm = x_ref.shape[0]
    blk = wsp_ref.shape[1]

    r = lax.broadcasted_iota(jnp.int32, (blk, blk), 0)
    c = lax.broadcasted_iota(jnp.int32, (blk, blk), 1)
    tri = (c <= r).astype(F32)
    cols = []
    for g in range(A_GROUPS):
        w = (wsp_ref[g] * tri).astype(BF16)
        bias = bsp_ref[:, g:g + 1]
        sl = slice(g * A_DG, (g + 1) * A_DG)
        chunks = [slice(ch * blk, (ch + 1) * blk) for ch in range(tm // blk)]
        rows = []
        for pair in [chunks[p:p + 2] for p in range(0, len(chunks), 2)]:
            v = [va_ref[rs, sl].astype(BF16) for rs in pair]
            mixed = _dot(w, v[0] if len(v) == 1 else jnp.concatenate(v, axis=1))
            for p, rs in enumerate(pair):
                rows.append(u_ref[rs, sl] * (mixed[:, p * A_DG:(p + 1) * A_DG] + bias))
        cols.append(rows[0] if len(rows) == 1 else jnp.concatenate(rows, axis=0))
    a_out = jnp.concatenate(cols, axis=1)

    a_n = _rms(a_out, ga_ref[...]).astype(BF16)
    b_n = _rms(b_ref[...], gb_ref[...]).astype(BF16)
    x1 = x_ref[...] + _dot(a_n, wout_ref[0:A_WIDTH, :]) + _dot(b_n, wout_ref[A_WIDTH:, :])

    hc = _rms(x1, gc_ref[...]).astype(BF16)
    q = _dot(hc, wcq_ref[...])
    rows_b = tm // n_batches
    mem_b = mk_ref.shape[0] // n_batches
    per_batch = []
    for b in range(n_batches):
        rs = slice(b * rows_b, (b + 1) * rows_b)
        ms = slice(b * mem_b, (b + 1) * mem_b)
        heads = []
        for h in range(CA_HEADS):
            sl = slice(h * CA_DH, (h + 1) * CA_DH)
            s = _dot_nt(q[rs, sl].astype(BF16), mk_ref[ms, sl].astype(BF16)) * (CA_DH ** -0.5)
            p = jnp.exp(s - jnp.max(s, axis=-1, keepdims=True))
            p = p / jnp.sum(p, axis=-1, keepdims=True)
            heads.append(_dot(p.astype(BF16), mv_ref[ms, sl].astype(BF16)))
        per_batch.append(jnp.concatenate(heads, axis=1))
    o = (per_batch[0] if n_batches == 1 else jnp.concatenate(per_batch, axis=0)).astype(BF16)
    x2 = x1 + _dot(o, wco_ref[...])
    x2_ref[...] = x2

    h2 = _rms(x2, gm_ref[...])
    h2_ref[...] = _pack_bf16_pairs(h2)
    hh, hl = _split_bf16(h2)
    wh, wl = _split_bf16(wr_ref[...])
    both = _dot(hh, jnp.concatenate([wh, wl], axis=1))
    logits = both[:, :LANES] + both[:, LANES:] + _dot(hl, wh) + br_ref[...]
    lane = lax.broadcasted_iota(jnp.int32, (tm, LANES), 1).astype(F32)
    cur = jnp.where(lane < N_EXPERTS, logits, -jnp.inf)
    vals, idxs = [], []
    for _ in range(TOP_K):
        m = jnp.max(cur, axis=-1, keepdims=True)
        i = jnp.min(jnp.where(cur == m, lane, float(LANES)), axis=-1, keepdims=True)
        vals.append(m)
        idxs.append(i)
        cur = jnp.where(lane == i, -jnp.inf, cur)
    es = [jnp.exp(v - vals[0]) for v in vals]
    denom = es[0] + es[1] + es[2] + es[3]
    idx_out = jnp.zeros((tm, LANES), F32)
    gate_out = jnp.zeros((tm, LANES), F32)
    for k in range(TOP_K):
        idx_out = jnp.where(lane == k, idxs[k], idx_out)
        gate_out = jnp.where(lane == k, es[k] / denom, gate_out)
    idx_ref[...] = idx_out.astype(jnp.int32)
    gate_ref[...] = gate_out


def _post(x, u, va, b_out, mk, mv, lw, tm, rows_per_batch, blk):
    n = x.shape[0]
    n_mem = mk.shape[0] // (n // rows_per_batch)
    row = lambda w: pl.BlockSpec((tm, w), lambda i: (i, 0))
    if tm >= rows_per_batch:
        n_batches = tm // rows_per_batch
        mem = pl.BlockSpec((n_batches * n_mem, CA_WIDTH), lambda i: (i, 0))
    else:
        n_batches = 1
        tiles_per_batch = rows_per_batch // tm
        mem = pl.BlockSpec((n_mem, CA_WIDTH), lambda i: (i // tiles_per_batch, 0))
    wsp = lw["w_sp"][:, :blk, :blk]
    bsp = lw["b_sp"][:, :blk].T
    return pl.pallas_call(
        functools.partial(_post_kernel, n_batches=n_batches),
        grid=(n // tm,),
        in_specs=[row(D_MODEL), row(A_WIDTH), row(A_WIDTH), row(SB_WIDTH),
                  _const_spec((A_GROUPS, blk, blk)), _const_spec((blk, A_GROUPS)),
                  _const_spec((1, A_WIDTH)), _const_spec((1, SB_WIDTH)),
                  _const_spec((2 * A_WIDTH, D_MODEL)),
                  _const_spec((1, D_MODEL)), _const_spec((D_MODEL, CA_WIDTH)),
                  mem, mem, _const_spec((CA_WIDTH, D_MODEL)),
                  _const_spec((1, D_MODEL)), _const_spec((D_MODEL, LANES)), _const_spec((1, LANES))],
        out_specs=[row(D_MODEL), row(D_MODEL // 2), row(LANES), row(LANES)],
        out_shape=[jax.ShapeDtypeStruct((n, D_MODEL), F32), jax.ShapeDtypeStruct((n, D_MODEL // 2), F32),
                   jax.ShapeDtypeStruct((n, LANES), jnp.int32), jax.ShapeDtypeStruct((n, LANES), F32)],
        compiler_params=_params(1),
        name="post",
    )(x, u, va, b_out, wsp, bsp, lw["g_a"], lw["g_b"], lw["w_out"], lw["g_cross"], lw["w_cq"],
      mk, mv, lw["w_co"], lw["g_moe"], lw["w_router"], lw["b_router"])


def _dispatch_kernel(last_unit, meta, dest_ref, hp_ref, hs_ref, xs_hbm, stage, zero, sem, sem_zero, *,
                     n_prompt_tiles, n_tiles):
    i = pl.program_id(0)
    td = stage.shape[1]
    pad = MOE_PAD
    slot = i % 2

    def zero_copy(row0):
        return pltpu.make_async_copy(zero, xs_hbm.at[pl.ds(pl.multiple_of(row0, pad), pad), :], sem_zero)

    @pl.when(i == 0)
    def _():
        zero[...] = jnp.zeros(zero.shape, F32)

        def fill(wait):
            def act(copy):
                if wait:
                    copy.wait()
                else:
                    copy.start()

            def expert_body(e, carry):
                @pl.when(last_unit[e] >= 0)
                def _():
                    act(zero_copy(last_unit[e]))
                return carry

            def tail_body(t, carry):
                act(zero_copy(t * pad))
                return carry

            lax.fori_loop(0, N_EXPERTS, expert_body, 0)
            lax.fori_loop(meta[1], xs_hbm.shape[0] // pad, tail_body, 0)

        fill(False)
        fill(True)

    def wait_slot(slot):
        for _ in range(TOP_K):
            pltpu.make_async_copy(stage.at[slot], xs_hbm.at[pl.ds(0, td), :], sem.at[slot]).wait()

    @pl.when(i >= 2)
    def _():
        wait_slot(slot)

    @pl.when(i < n_prompt_tiles)
    def _():
        stage[slot] = hp_ref[...]

    @pl.when(i >= n_prompt_tiles)
    def _():
        stage[slot] = hs_ref[...]

    def issue(r, carry):
        for k in range(TOP_K):
            pltpu.make_async_copy(stage.at[slot, pl.ds(r, 1), :],
                                  xs_hbm.at[pl.ds(dest_ref[0, 0, r * TOP_K + k], 1), :],
                                  sem.at[slot]).start()
        return carry

    lax.fori_loop(0, td, issue, 0, unroll=4)

    @pl.when(i == n_tiles - 1)
    def _():
        wait_slot(slot)
        if n_tiles > 1:
            wait_slot(1 - slot)


def _dispatch(last_unit, meta, dest, h_prompt, h_sample, p_rows):
    td = DISPATCH_TILE
    n_p, width = h_prompt.shape
    n_s = h_sample.shape[0]
    assert n_p % td == 0 and n_s % td == 0
    n_prompt_tiles, n_tiles = n_p // td, (n_p + n_s) // td
    grid_spec = pltpu.PrefetchScalarGridSpec(
        num_scalar_prefetch=2,
        grid=(n_tiles,),
        in_specs=[pl.BlockSpec((1, 1, td * TOP_K), lambda i, *_: (i, 0, 0), memory_space=pltpu.SMEM),
                  pl.BlockSpec((td, width), lambda i, *_: (jnp.minimum(i, n_prompt_tiles - 1), 0)),
                  pl.BlockSpec((td, width), lambda i, *_: (jnp.maximum(i - n_prompt_tiles, 0), 0))],
        out_specs=pl.BlockSpec(memory_space=pl.ANY),
        scratch_shapes=[pltpu.VMEM((2, td, width), F32), pltpu.VMEM((MOE_PAD, width), F32),
                        pltpu.SemaphoreType.DMA((2,)), pltpu.SemaphoreType.DMA],
    )
    return pl.pallas_call(
        functools.partial(_dispatch_kernel, n_prompt_tiles=n_prompt_tiles, n_tiles=n_tiles),
        grid_spec=grid_spec,
        out_shape=jax.ShapeDtypeStruct((p_rows, width), F32),
        compiler_params=_params(1),
        name="moe_dispatch",
    )(last_unit, meta, dest.reshape(n_tiles, 1, td * TOP_K), h_prompt, h_sample)


def _expert_kernel(item_e, item_start, item_units, meta,
                   xs_hbm, wg_ref, wl_ref, wd_ref, bg_ref, bl_ref, bd_ref,
                   y_hbm, stage, x_buf, y_buf, sem_in, sem_out):
    s = pl.program_id(0)
    j = pl.program_id(1)
    n_slots = pl.num_programs(0)
    units = item_units[s]
    start = item_start[s]
    pad = MOE_PAD
    assert MOE_NCH >= 2

    def unit_rows(t):
        return pl.ds(pl.multiple_of(t * pad, pad), pad)

    def hbm_rows(first_row, t):
        return pl.ds(pl.multiple_of(first_row + t * pad, pad), pad)

    def copy_in(first_row, t):
        return pltpu.make_async_copy(xs_hbm.at[hbm_rows(first_row, t), :], stage.at[unit_rows(t), :], sem_in)

    def copy_out(t):
        return pltpu.make_async_copy(y_buf.at[unit_rows(t), :], y_hbm.at[hbm_rows(start, t), :], sem_out)

    def for_range(lo, hi, fn):
        def body(t, carry):
            fn(t)
            return carry
        lax.fori_loop(lo, hi, body, 0)

    @pl.when(jnp.logical_and(s == 0, j == 0))
    def _():
        for_range(0, units, lambda t: copy_in(start, t).start())

    @pl.when(jnp.logical_and(j == MOE_NCH - 1, s + 1 < n_slots))
    def _():
        nxt = jnp.minimum(s + 1, n_slots - 1)
        for_range(0, item_units[nxt], lambda t: copy_in(item_start[nxt], t).start())

    @pl.when(j == 0)
    def _():
        for_range(0, units, lambda t: copy_in(start, t).wait())

    def ffn(row0, m, first):
        wg = wg_ref[...].astype(BF16)
        wl = wl_ref[...].astype(BF16)
        wd = wd_ref[...].astype(BF16)
        for off in range(0, m, MOE_TILE):
            rows = pl.ds(pl.multiple_of(row0 + off, pad), min(MOE_TILE, m - off))
            if first:
                xb = jnp.concatenate(_unpack_bf16_pairs(stage[rows, :]), axis=1)
                x_buf[rows, :] = xb
            else:
                xb = x_buf[rows, :]
            glu = jnp.minimum(_dot(xb, wg) + bg_ref[...], SWIGLU_LIMIT)
            lin = jnp.clip(_dot(xb, wl) + bl_ref[...], -SWIGLU_LIMIT, SWIGLU_LIMIT)
            act = glu * jax.nn.sigmoid(SWIGLU_ALPHA * glu) * (lin + 1.0)
            down = _dot(act.astype(BF16), wd)
            if first:
                y_buf[rows, :] = bd_ref[...] + down
            else:
                y_buf[rows, :] += down

        @pl.when(j == MOE_NCH - 1)
        def _():
            for u in range(m // pad):
                copy_out(row0 // pad + u).start()

    def run(first):
        big = MOE_BODY_ROWS
        n_big = units // (big // pad)
        for_range(0, n_big, lambda t: ffn(t * big, big, first))
        base = n_big * big
        rem = units * pad - base
        m = big // 2
        while m >= pad:
            take = (rem & m) != 0

            @pl.when(take)
            def _(base=base, m=m):
                ffn(base, m, first)

            base = base + jnp.where(take, m, 0)
            m //= 2

    @pl.when(jnp.logical_and(units > 0, j == 0))
    def _():
        run(True)

    @pl.when(jnp.logical_and(units > 0, j > 0))
    def _():
        run(False)

    @pl.when(jnp.logical_and(j == MOE_NCH - 1, units > 0))
    def _store():
        for_range(0, units, lambda t: copy_out(t).wait())

    @pl.when(jnp.logical_and(s == pl.num_programs(0) - 1, j == MOE_NCH - 1))
    def _fill_tail():
        y_buf[pl.ds(0, pad), :] = jnp.zeros((pad, D_MODEL), F32)

        def tail_copy(t):
            dst = y_hbm.at[pl.ds(pl.multiple_of(t * pad, pad), pad), :]
            return pltpu.make_async_copy(y_buf.at[pl.ds(0, pad), :], dst, sem_out)

        n_units = y_hbm.shape[0] // pad
        for_range(meta[1], n_units, lambda t: tail_copy(t).start())
        for_range(meta[1], n_units, lambda t: tail_copy(t).wait())


def _experts(item_e, item_start, item_units, meta, xs, w_gu, b_gu, w_dn, b_dn):
    n_slots = item_e.shape[0]
    p_rows = xs.shape[0]
    c = MOE_CHUNK

    def chunk(s, j, ie, ist, iu, m):
        return jnp.where(s < m[0], j, MOE_NCH - 1)

    grid_spec = pltpu.PrefetchScalarGridSpec(
        num_scalar_prefetch=4,
        grid=(n_slots, MOE_NCH),
        in_specs=[
            pl.BlockSpec(memory_space=pl.ANY),
            pl.BlockSpec((None, D_MODEL, c), lambda s, j, ie, *a: (ie[s], 0, chunk(s, j, ie, *a))),
            pl.BlockSpec((None, D_MODEL, c), lambda s, j, ie, *a: (ie[s], 0, MOE_NCH + chunk(s, j, ie, *a))),
            pl.BlockSpec((None, c, D_MODEL), lambda s, j, ie, *a: (ie[s], chunk(s, j, ie, *a), 0)),
            pl.BlockSpec((None, 1, c), lambda s, j, ie, *a: (ie[s], 0, chunk(s, j, ie, *a))),
            pl.BlockSpec((None, 1, c), lambda s, j, ie, *a: (ie[s], 0, MOE_NCH + chunk(s, j, ie, *a))),
            pl.BlockSpec((None, 1, D_MODEL), lambda s, j, ie, *a: (ie[s], 0, 0)),
        ],
        out_specs=pl.BlockSpec(memory_space=pl.ANY),
        scratch_shapes=[
            pltpu.VMEM((MOE_ITEM_ROWS, D_MODEL // 2), F32),
            pltpu.VMEM((MOE_ITEM_ROWS, D_MODEL), BF16),
            pltpu.VMEM((MOE_ITEM_ROWS, D_MODEL), F32),
            pltpu.SemaphoreType.DMA,
            pltpu.SemaphoreType.DMA,
        ],
    )
    return pl.pallas_call(
        _expert_kernel,
        grid_spec=grid_spec,
        out_shape=jax.ShapeDtypeStruct((p_rows, D_MODEL), F32),
        compiler_params=_params(2),
        name="moe_experts",
    )(item_e, item_start, item_units, meta, xs, w_gu, w_gu, w_dn,
      b_gu.reshape(N_EXPERTS, 1, 2 * D_EXPERT), b_gu.reshape(N_EXPERTS, 1, 2 * D_EXPERT),
      b_dn.reshape(N_EXPERTS, 1, D_MODEL))


def _combine_kernel(dest_ref, next_dest_ref, x_ref, gate_ref, gf_ref, y_hbm, o_ref, buf, sem, *,
                    final_norm):
    i = pl.program_id(0)
    tc = x_ref.shape[0]
    slot = i % 2

    def issue(ref, slot):
        for r in range(tc):
            for k in range(TOP_K):
                pltpu.make_async_copy(y_hbm.at[pl.ds(ref[0, 0, r * TOP_K + k], 1), :],
                                      buf.at[slot, pl.ds(k * tc + r, 1), :], sem.at[slot]).start()

    @pl.when(i == 0)
    def _():
        issue(dest_ref, 0)

    @pl.when(i + 1 < pl.num_programs(0))
    def _():
        issue(next_dest_ref, 1 - slot)

    pltpu.make_async_copy(y_hbm.at[pl.ds(0, TOP_K * tc), :], buf.at[slot], sem.at[slot]).wait()
    gate = gate_ref[...]
    acc = x_ref[...]
    for k in range(TOP_K):
        acc = acc + gate[:, k:k + 1] * buf[slot, k * tc:(k + 1) * tc, :]
    o_ref[...] = _rms(acc, gf_ref[...]) if final_norm else acc


def _combine(dest, x2, gate, g_final, y, final_norm):
    tc = COMBINE_TILE
    n = x2.shape[0]
    n_tiles = n // tc
    dest = dest.reshape(n_tiles, 1, tc * TOP_K)
    return pl.pallas_call(
        functools.partial(_combine_kernel, final_norm=final_norm),
        grid=(n_tiles,),
        in_specs=[pl.BlockSpec((1, 1, tc * TOP_K), lambda i: (i, 0, 0), memory_space=pltpu.SMEM),
                  pl.BlockSpec((1, 1, tc * TOP_K), lambda i: (jnp.minimum(i + 1, n_tiles - 1), 0, 0),
                               memory_space=pltpu.SMEM),
                  pl.BlockSpec((tc, D_MODEL), lambda i: (i, 0)),
                  pl.BlockSpec((tc, LANES), lambda i: (i, 0)),
                  _const_spec((1, D_MODEL)),
                  pl.BlockSpec(memory_space=pl.ANY)],
        out_specs=pl.BlockSpec((tc, D_MODEL), lambda i: (i, 0)),
        out_shape=jax.ShapeDtypeStruct((n, D_MODEL), F32),
        scratch_shapes=[pltpu.VMEM((2, TOP_K * tc, D_MODEL), F32), pltpu.SemaphoreType.DMA((2,))],
        compiler_params=_params(1),
        name="moe_combine",
    )(dest, dest, x2, gate, g_final, y)


def _route(top_idx):
    n_tok = top_idx.shape[0]
    n_assign = n_tok * TOP_K
    pad, item_rows = MOE_PAD, MOE_ITEM_ROWS
    p_rows = -(-(n_assign + N_EXPERTS * (pad - 1)) // pad) * pad
    n_slots = N_EXPERTS + -(-p_rows // item_rows)
    experts = jnp.arange(N_EXPERTS, dtype=jnp.int32)

    flat_e = top_idx.reshape(-1)
    onehot = (top_idx[:, :, None] == experts[None, None, :]).astype(jnp.int32)
    chosen = jnp.sum(onehot, axis=1)
    running = jnp.cumsum(chosen, axis=0)
    rank = jnp.sum((running - chosen)[:, None, :] * onehot, axis=2).reshape(-1)
    counts = running[-1]
    padded = (counts + pad - 1) // pad * pad
    pad_end = jnp.cumsum(padded)
    pad_start = pad_end - padded
    dest = pad_start[flat_e] + rank
    last_unit = jnp.where(padded > 0, pad_end - pad, -1).astype(jnp.int32)

    items_per_e = (padded + item_rows - 1) // item_rows
    item_end = jnp.cumsum(items_per_e)
    n_items = item_end[-1]
    slot = jnp.arange(n_slots, dtype=jnp.int32)
    live = slot < n_items
    s_eff = jnp.minimum(slot, n_items - 1)
    e = jnp.minimum(jnp.searchsorted(item_end, s_eff, side="right"), N_EXPERTS - 1).astype(jnp.int32)
    local = s_eff - (item_end[e] - items_per_e[e])
    start = pad_start[e] + local * item_rows
    n_rows = jnp.clip(padded[e] - local * item_rows, 0, item_rows)
    units = jnp.where(live, n_rows // pad, 0).astype(jnp.int32)
    meta = jnp.stack([n_items, pad_end[-1] // pad]).astype(jnp.int32)
    return dest.astype(jnp.int32), last_unit, p_rows, e, start.astype(jnp.int32), units, meta


def _group_front(x, sb_fn, mk, mv, lw, tm_proj, tm_post, rows_per_batch, blk):
    u, va, q, k, vb = _proj(x, lw["g_mix"], lw["w_in"], tm_proj)
    b_out = sb_fn(q, k, vb)
    x2, h2, idx, gate = _post(x, u, va, b_out, mk, mv, lw, tm_post, rows_per_batch, blk)
    return x2, h2, idx, gate, k, vb, va


def kernel(x_prompt, x_sample, cache_sb_k, cache_sb_v, cache_mem_k, cache_mem_v, mem_prompt, g_mix, w_in, w_sp, b_sp, g_a, g_b, w_out, g_cross, g_mem, w_cq, w_ck, w_cv, w_co, g_moe, w_router, b_router, w_gu, b_gu, w_dn, b_dn, g_final):
    depth = g_mix.shape[0]
    bsz, seq, _ = x_prompt.shape
    dbsz, dseq, _ = x_sample.shape
    n_mem = mem_prompt.shape[1]
    n_p, n_s = bsz * seq, dbsz * dseq

    xp = x_prompt.reshape(n_p, D_MODEL)
    xs = x_sample.reshape(n_s, D_MODEL)
    mem = mem_prompt.reshape(bsz * n_mem, D_MODEL)
    outs = [[] for _ in range(7)]
    for l in range(depth):
        lw = {
            "g_mix": g_mix[l][None], "w_in": w_in[l].astype(BF16),
            "w_sp": w_sp[l], "b_sp": b_sp[l], "g_a": g_a[l][None], "g_b": g_b[l][None],
            "w_out": w_out[l].astype(BF16), "g_cross": g_cross[l][None],
            "w_cq": w_cq[l].astype(BF16), "w_co": w_co[l].astype(BF16), "g_moe": g_moe[l][None],
            "w_router": jnp.pad(w_router[l], ((0, 0), (0, LANES - N_EXPERTS))),
            "b_router": jnp.pad(b_router[l], (0, LANES - N_EXPERTS))[None],
        }
        mk, mv = _memkv(mem, g_mem[l][None], w_ck[l].astype(BF16), w_cv[l].astype(BF16))

        def sb_p(q, k, v):
            shp = (bsz, seq, SB_WIDTH)
            return _sb_prompt(q.reshape(shp), k.reshape(shp), v.reshape(shp)).reshape(n_p, SB_WIDTH)

        def sb_s(q, k, v):
            shp = (dbsz, dseq, SB_WIDTH)
            return _sb_sample(q.reshape(shp), k.reshape(shp), v.reshape(shp),
                              cache_sb_k, cache_sb_v, l).reshape(n_s, SB_WIDTH)

        xp2, hp, idx_p, gate_p, k_p, v_p, _ = _group_front(
            xp, sb_p, mk, mv, lw, 256, 256, seq, A_CHUNK)
        xs2, hs, idx_s, gate_s, k_s, v_s, va_s = _group_front(
            xs, sb_s, cache_mem_k[l].reshape(dbsz * n_mem, CA_WIDTH),
            cache_mem_v[l].reshape(dbsz * n_mem, CA_WIDTH), lw, n_s, n_s, dseq, min(dseq, A_CHUNK))

        top_idx = jnp.concatenate([idx_p[:, :TOP_K], idx_s[:, :TOP_K]], axis=0)
        dest, last_unit, p_rows, item_e, item_start, item_units, meta = _route(top_idx)
        x_sorted = _dispatch(last_unit, meta, dest, hp, hs, p_rows)
        y = _experts(item_e, item_start, item_units, meta, x_sorted, w_gu[l], b_gu[l], w_dn[l], b_dn[l])
        last = l == depth - 1
        dest = dest.reshape(n_p + n_s, TOP_K)
        xp = _combine(dest[:n_p], xp2, gate_p, g_final[None], y, last)
        xs = _combine(dest[n_p:], xs2, gate_s, g_final[None], y, last)

        outs[0].append(k_p.reshape(bsz, seq, SB_HEADS, SB_DH))
        outs[1].append(v_p.reshape(bsz, seq, SB_HEADS, SB_DH))
        outs[2].append(mk.reshape(bsz, n_mem, CA_HEADS, CA_DH))
        outs[3].append(mv.reshape(bsz, n_mem, CA_HEADS, CA_DH))
        outs[4].append(k_s.reshape(dbsz, dseq, SB_HEADS, SB_DH))
        outs[5].append(v_s.reshape(dbsz, dseq, SB_HEADS, SB_DH))
        outs[6].append(va_s.reshape(dbsz, dseq, A_GROUPS, A_DG))

    stacked = [jnp.stack(o, axis=0) for o in outs]
    return (xp.reshape(bsz, seq, D_MODEL), xs.reshape(dbsz, dseq, D_MODEL), *stacked)
```

```python
import functools

import jax
import jax.numpy as jnp
from jax import lax
from jax.experimental import pallas as pl
from jax.experimental.pallas import tpu as pltpu

F32 = jnp.float32
BF16 = jnp.bfloat16

D_MODEL = 2048
A_GROUPS = 8
A_DG = 128
A_WIDTH = A_GROUPS * A_DG
A_CHUNK = 128
SB_HEADS = 8
SB_DH = 128
SB_WIDTH = SB_HEADS * SB_DH
IN_WIDTH = 2 * A_WIDTH + 3 * SB_WIDTH
N_SEG = IN_WIDTH // 1024
CA_HEADS = 4
CA_DH = 128
CA_WIDTH = CA_HEADS * CA_DH
N_EXPERTS = 32
TOP_K = 4
D_EXPERT = D_MODEL
SWIGLU_LIMIT = 7.0
SWIGLU_ALPHA = 1.702
RMS_EPS = 1e-6

LANES = 128
SB_TILE = 256
SB_HEADS_PER_STEP = 8
VMEM_LIMIT = 56 * 1024 * 1024

MOE_PAD = 128
MOE_TILE = 256
MOE_BODY_ROWS = 512
MOE_ITEM_ROWS = 1280
MOE_CHUNK = 512
MOE_NCH = D_EXPERT // MOE_CHUNK
DISPATCH_TILE = 128
COMBINE_TILE = 256


def _rms(x, g):
    return x * lax.rsqrt(jnp.mean(x * x, axis=-1, keepdims=True) + RMS_EPS) * g


def _dot(a, b):
    return jnp.dot(a, b, preferred_element_type=F32)


def _dot_nt(a, b):
    return lax.dot_general(a, b, (((1,), (1,)), ((), ())), preferred_element_type=F32)


def _split_bf16(x):
    hi = x.astype(BF16)
    lo = (x - hi.astype(F32)).astype(BF16)
    return hi, lo


def _pack_bf16_pairs(x):
    c = x.shape[1] // 2
    r = x.astype(BF16).astype(F32)
    lo = lax.bitcast_convert_type(r[:, :c], jnp.uint32) >> 16
    hi = lax.bitcast_convert_type(r[:, c:], jnp.uint32)
    return lax.bitcast_convert_type(hi | lo, F32)


def _unpack_bf16_pairs(w):
    bits = lax.bitcast_convert_type(w, jnp.uint32)
    lo = lax.bitcast_convert_type(bits << 16, F32).astype(BF16)
    hi = lax.bitcast_convert_type(bits & jnp.uint32(0xFFFF0000), F32).astype(BF16)
    return lo, hi


def _params(n_axes):
    return pltpu.CompilerParams(dimension_semantics=("arbitrary",) * n_axes,
                                vmem_limit_bytes=VMEM_LIMIT)


def _const_spec(shape):
    return pl.BlockSpec(shape, lambda *_: (0,) * len(shape), pipeline_mode=pl.Buffered(1))


def _proj_kernel(x_ref, g_ref, w_ref, *out_refs):
    h = _rms(x_ref[...], g_ref[...]).astype(BF16)
    for s, o_ref in enumerate(out_refs):
        o_ref[...] = _dot(h, w_ref[:, s * 1024:(s + 1) * 1024])


def _proj(x, g, w, tm):
    n = x.shape[0]
    seg = jax.ShapeDtypeStruct((n, 1024), F32)
    return pl.pallas_call(
        _proj_kernel,
        grid=(n // tm,),
        in_specs=[pl.BlockSpec((tm, D_MODEL), lambda i: (i, 0)),
                  _const_spec((1, D_MODEL)),
                  _const_spec((D_MODEL, IN_WIDTH))],
        out_specs=[pl.BlockSpec((tm, 1024), lambda i: (i, 0))] * N_SEG,
        out_shape=[seg] * N_SEG,
        compiler_params=_params(1),
        name="proj",
    )(x, g, w)


def _log_sigmoid(x):
    neg_abs = lax.bitcast_convert_type(
        lax.bitcast_convert_type(x, jnp.uint32) | jnp.uint32(0x80000000), F32)
    return jnp.minimum(x, 0.0) - jnp.log(1.0 + jnp.exp(neg_abs))


def _sb_blocks(qbs, kbs, vbs, carries, later, mask):
    n = range(len(qbs))
    nzs = [_dot_nt(qbs[i], kbs[i]) * -(SB_DH ** -0.5) for i in n]
    ls = [_log_sigmoid(nz) for nz in nzs]
    if mask is not None:
        ls = [jnp.where(mask, l, 0.0) for l in ls]
    css = [_dot(jnp.concatenate(_split_bf16(l), axis=1), later) for l in ls]
    aas = [jnp.exp(ls[i] - nzs[i] + css[i] + carries[i]) for i in n]
    if mask is not None:
        aas = [jnp.where(mask, a, 0.0) for a in aas]
    contribs = [_dot(aas[i].astype(BF16), vbs[i]) for i in n]
    new = [carries[i] + css[i][:, 0:1] + ls[i][:, 0:1] for i in n]
    return contribs, new


def _later_matrix(n):
    r = lax.broadcasted_iota(jnp.int32, (n, n), 0)
    c = lax.broadcasted_iota(jnp.int32, (n, n), 1)
    m = (r > c).astype(BF16)
    return jnp.concatenate([m, m], axis=0)


def _sb_prompt_kernel(q_ref, k_ref, v_ref, o_ref, later_ref, carry_ref):
    qi = pl.program_id(2)
    t = SB_TILE
    later_ref[...] = _later_matrix(t)
    r = lax.broadcasted_iota(jnp.int32, (t, t), 0)
    c = lax.broadcasted_iota(jnp.int32, (t, t), 1)
    heads = [slice(h * SB_DH, (h + 1) * SB_DH) for h in range(SB_HEADS_PER_STEP)]

    def sweep(j, first):
        off = pl.multiple_of(j * t, t)
        rows = pl.ds(off, t)
        carries = [jnp.zeros((t, 1), F32) if first else carry_ref[h] for h in range(len(heads))]
        contribs, carries = _sb_blocks(
            [q_ref[:, sl].astype(BF16) for sl in heads],
            [k_ref[rows, sl].astype(BF16) for sl in heads],
            [v_ref[rows, sl].astype(BF16) for sl in heads],
            carries, later_ref[...], (c < r) if first else None)
        for h, sl in enumerate(heads):
            carry_ref[h] = carries[h]
            if first:
                o_ref[:, sl] = contribs[h]
            else:
                o_ref[:, sl] += contribs[h]

    sweep(qi, True)

    def body(step, carry):
        sweep(qi - 1 - step, False)
        return carry

    lax.fori_loop(0, qi, body, 0)


def _sb_prompt(q, k, v):
    bsz, t_len, _ = q.shape
    t = SB_TILE
    w = SB_HEADS_PER_STEP * SB_DH
    kv_spec = pl.BlockSpec((None, t_len, w), lambda b, h, i: (b, 0, h))
    return pl.pallas_call(
        _sb_prompt_kernel,
        grid=(bsz, SB_HEADS // SB_HEADS_PER_STEP, t_len // t),
        in_specs=[pl.BlockSpec((None, t, w), lambda b, h, i: (b, i, h)), kv_spec, kv_spec],
        out_specs=pl.BlockSpec((None, t, w), lambda b, h, i: (b, i, h)),
        out_shape=jax.ShapeDtypeStruct(q.shape, F32),
        scratch_shapes=[pltpu.VMEM((2 * t, t), BF16), pltpu.VMEM((SB_HEADS_PER_STEP, t, 1), F32)],
        compiler_params=_params(3),
        name="sb_prompt",
    )(q, k, v)


def _sb_sample_kernel(q_ref, kn_ref, vn_ref, ck_ref, cv_ref, o_ref, later_ref):
    t = SB_TILE
    n_new = q_ref.shape[0]
    later_ref[...] = _later_matrix(t)
    r = lax.broadcasted_iota(jnp.int32, (n_new, t), 0)
    c = lax.broadcasted_iota(jnp.int32, (n_new, t), 1)
    heads = range(SB_HEADS)
    sls = [slice(h * SB_DH, (h + 1) * SB_DH) for h in heads]
    qbs = [q_ref[:, sl].astype(BF16) for sl in sls]
    accs, carries = _sb_blocks(qbs, [kn_ref[:, sl].astype(BF16) for sl in sls],
                               [vn_ref[:, sl].astype(BF16) for sl in sls],
                               [jnp.zeros((n_new, 1), F32)] * SB_HEADS, later_ref[...], c < r)
    for j in reversed(range(ck_ref.shape[0] // (t * SB_HEADS))):
        rows = [pl.ds(j * t * SB_HEADS + h, t, stride=SB_HEADS) for h in heads]
        contribs, carries = _sb_blocks(qbs, [ck_ref[rows[h], :].astype(BF16) for h in heads],
                                       [cv_ref[rows[h], :].astype(BF16) for h in heads],
                                       carries, later_ref[...], None)
        accs = [a + cb for a, cb in zip(accs, contribs)]
    for sl, acc in zip(sls, accs):
        o_ref[:, sl] = acc


def _sb_sample(q, k_new, v_new, cache_k, cache_v, layer):
    bsz, n_new, _ = q.shape
    past = cache_k.shape[2]
    t = SB_TILE
    assert n_new <= t and past % t == 0
    pad = ((0, 0), (0, t - n_new), (0, 0))
    kn = jnp.pad(k_new, pad)
    vn = jnp.pad(v_new, pad)
    rows = lambda n: pl.BlockSpec((None, n, SB_WIDTH), lambda b: (b, 0, 0))
    depth = cache_k.shape[0]
    flat = (depth, bsz, past * SB_HEADS, SB_DH)
    cache = pl.BlockSpec((None, None, past * SB_HEADS, SB_DH), lambda b: (layer, b, 0, 0))
    return pl.pallas_call(
        _sb_sample_kernel,
        grid=(bsz,),
        in_specs=[rows(n_new), rows(t), rows(t), cache, cache],
        out_specs=rows(n_new),
        out_shape=jax.ShapeDtypeStruct(q.shape, F32),
        scratch_shapes=[pltpu.VMEM((2 * t, t), BF16)],
        compiler_params=_params(1),
        name="sb_sample",
    )(q, kn, vn, cache_k.reshape(flat), cache_v.reshape(flat))


def _memkv_kernel(m_ref, g_ref, wk_ref, wv_ref, k_ref, v_ref):
    h = _rms(m_ref[...], g_ref[...]).astype(BF16)
    k_ref[...] = _dot(h, wk_ref[...])
    v_ref[...] = _dot(h, wv_ref[...])


def _memkv(mem, g, wk, wv):
    n = mem.shape[0]
    tm = 256
    out = jax.ShapeDtypeStruct((n, CA_WIDTH), F32)
    return pl.pallas_call(
        _memkv_kernel,
        grid=(n // tm,),
        in_specs=[pl.BlockSpec((tm, D_MODEL), lambda i: (i, 0)),
                  _const_spec((1, D_MODEL)),
                  _const_spec((D_MODEL, CA_WIDTH)),
                  _const_spec((D_MODEL, CA_WIDTH))],
        out_specs=[pl.BlockSpec((tm, CA_WIDTH), lambda i: (i, 0))] * 2,
        out_shape=[out, out],
        compiler_params=_params(1),
        name="memkv",
    )(mem, g, wk, wv)


def _post_kernel(x_ref, u_ref, va_ref, b_ref, wsp_ref, bsp_ref, ga_ref, gb_ref, wout_ref,
                 gc_ref, wcq_ref, mk_ref, mv_ref, wco_ref, gm_ref, wr_ref, br_ref,
                 x2_ref, h2_ref, idx_ref, gate_ref, *, n_batches):
    tm = x_ref.shape[0]
    blk = wsp_ref.shape[1]

    r = lax.broadcasted_iota(jnp.int32, (blk, blk), 0)
    c = lax.broadcasted_iota(jnp.int32, (blk, blk), 1)
    tri = (c <= r).astype(F32)
    cols = []
    for g in range(A_GROUPS):
        w = (wsp_ref[g] * tri).astype(BF16)
        bias = bsp_ref[:, g:g + 1]
        sl = slice(g * A_DG, (g + 1) * A_DG)
        chunks = [slice(ch * blk, (ch + 1) * blk) for ch in range(tm // blk)]
        rows = []
        for pair in [chunks[p:p + 2] for p in range(0, len(chunks), 2)]:
            v = [va_ref[rs, sl].astype(BF16) for rs in pair]
            mixed = _dot(w, v[0] if len(v) == 1 else jnp.concatenate(v, axis=1))
            for p, rs in enumerate(pair):
                rows.append(u_ref[rs, sl] * (mixed[:, p * A_DG:(p + 1) * A_DG] + bias))
        cols.append(rows[0] if len(rows) == 1 else jnp.concatenate(rows, axis=0))
    a_out = jnp.concatenate(cols, axis=1)

    a_n = _rms(a_out, ga_ref[...]).astype(BF16)
    b_n = _rms(b_ref[...], gb_ref[...]).astype(BF16)
    x1 = x_ref[...] + _dot(a_n, wout_ref[0:A_WIDTH, :]) + _dot(b_n, wout_ref[A_WIDTH:, :])

    hc = _rms(x1, gc_ref[...]).astype(BF16)
    q = _dot(hc, wcq_ref[...])
    rows_b = tm // n_batches
    mem_b = mk_ref.shape[0] // n_batches
    per_batch = []
    for b in range(n_batches):
        rs = slice(b * rows_b, (b + 1) * rows_b)
        ms = slice(b * mem_b, (b + 1) * mem_b)
        heads = []
        for h in range(CA_HEADS):
            sl = slice(h * CA_DH, (h + 1) * CA_DH)
            s = _dot_nt(q[rs, sl].astype(BF16), mk_ref[ms, sl].astype(BF16)) * (CA_DH ** -0.5)
            p = jnp.exp(s - jnp.max(s, axis=-1, keepdims=True))
            p = p / jnp.sum(p, axis=-1, keepdims=True)
            heads.append(_dot(p.astype(BF16), mv_ref[ms, sl].astype(BF16)))
        per_batch.append(jnp.concatenate(heads, axis=1))
    o = (per_batch[0] if n_batches == 1 else jnp.concatenate(per_batch, axis=0)).astype(BF16)
    x2 = x1 + _dot(o, wco_ref[...])
    x2_ref[...] = x2

    h2 = _rms(x2, gm_ref[...])
    h2_ref[...] = _pack_bf16_pairs(h2)
    hh, hl = _split_bf16(h2)
    wh, wl = _split_bf16(wr_ref[...])
    both = _dot(hh, jnp.concatenate([wh, wl], axis=1))
    logits = both[:, :LANES] + both[:, LANES:] + _dot(hl, wh) + br_ref[...]
    lane = lax.broadcasted_iota(jnp.int32, (tm, LANES), 1).astype(F32)
    cur = jnp.where(lane < N_EXPERTS, logits, -jnp.inf)
    vals, idxs = [], []
    for _ in range(TOP_K):
        m = jnp.max(cur, axis=-1, keepdims=True)
        i = jnp.min(jnp.where(cur == m, lane, float(LANES)), axis=-1, keepdims=True)
        vals.append(m)
        idxs.append(i)
        cur = jnp.where(lane == i, -jnp.inf, cur)
    es = [jnp.exp(v - vals[0]) for v in vals]
    denom = es[0] + es[1] + es[2] + es[3]
    idx_out = jnp.zeros((tm, LANES), F32)
    gate_out = jnp.zeros((tm, LANES), F32)
    for k in range(TOP_K):
        idx_out = jnp.where(lane == k, idxs[k], idx_out)
        gate_out = jnp.where(lane == k, es[k] / denom, gate_out)
    idx_ref[...] = idx_out.astype(jnp.int32)
    gate_ref[...] = gate_out


def _post(x, u, va, b_out, mk, mv, lw, tm, rows_per_batch, blk):
    n = x.shape[0]
    n_mem = mk.shape[0] // (n // rows_per_batch)
    row = lambda w: pl.BlockSpec((tm, w), lambda i: (i, 0))
    if tm >= rows_per_batch:
        n_batches = tm // rows_per_batch
        mem = pl.BlockSpec((n_batches * n_mem, CA_WIDTH), lambda i: (i, 0))
    else:
        n_batches = 1
        tiles_per_batch = rows_per_batch // tm
        mem = pl.BlockSpec((n_mem, CA_WIDTH), lambda i: (i // tiles_per_batch, 0))
    wsp = lw["w_sp"][:, :blk, :blk]
    bsp = lw["b_sp"][:, :blk].T
    return pl.pallas_call(
        functools.partial(_post_kernel, n_batches=n_batches),
        grid=(n // tm,),
        in_specs=[row(D_MODEL), row(A_WIDTH), row(A_WIDTH), row(SB_WIDTH),
                  _const_spec((A_GROUPS, blk, blk)), _const_spec((blk, A_GROUPS)),
                  _const_spec((1, A_WIDTH)), _const_spec((1, SB_WIDTH)),
                  _const_spec((2 * A_WIDTH, D_MODEL)),
                  _const_spec((1, D_MODEL)), _const_spec((D_MODEL, CA_WIDTH)),
                  mem, mem, _const_spec((CA_WIDTH, D_MODEL)),
                  _const_spec((1, D_MODEL)), _const_spec((D_MODEL, LANES)), _const_spec((1, LANES))],
        out_specs=[row(D_MODEL), row(D_MODEL // 2), row(LANES), row(LANES)],
        out_shape=[jax.ShapeDtypeStruct((n, D_MODEL), F32), jax.ShapeDtypeStruct((n, D_MODEL // 2), F32),
                   jax.ShapeDtypeStruct((n, LANES), jnp.int32), jax.ShapeDtypeStruct((n, LANES), F32)],
        compiler_params=_params(1),
        name="post",
    )(x, u, va, b_out, wsp, bsp, lw["g_a"], lw["g_b"], lw["w_out"], lw["g_cross"], lw["w_cq"],
      mk, mv, lw["w_co"], lw["g_moe"], lw["w_router"], lw["b_router"])


def _dispatch_kernel(last_unit, meta, dest_ref, hp_ref, hs_ref, xs_hbm, stage, zero, sem, sem_zero, *,
                     n_prompt_tiles, n_tiles):
    i = pl.program_id(0)
    td = stage.shape[1]
    pad = MOE_PAD
    slot = i % 2

    def zero_copy(row0):
        return pltpu.make_async_copy(zero, xs_hbm.at[pl.ds(pl.multiple_of(row0, pad), pad), :], sem_zero)

    @pl.when(i == 0)
    def _():
        zero[...] = jnp.zeros(zero.shape, F32)

        def fill(wait):
            def act(copy):
                if wait:
                    copy.wait()
                else:
                    copy.start()

            def expert_body(e, carry):
                @pl.when(last_unit[e] >= 0)
                def _():
                    act(zero_copy(last_unit[e]))
                return carry

            def tail_body(t, carry):
                act(zero_copy(t * pad))
                return carry

            lax.fori_loop(0, N_EXPERTS, expert_body, 0)
            lax.fori_loop(meta[1], xs_hbm.shape[0] // pad, tail_body, 0)

        fill(False)
        fill(True)

    def wait_slot(slot):
        for _ in range(TOP_K):
            pltpu.make_async_copy(stage.at[slot], xs_hbm.at[pl.ds(0, td), :], sem.at[slot]).wait()

    @pl.when(i >= 2)
    def _():
        wait_slot(slot)

    @pl.when(i < n_prompt_tiles)
    def _():
        stage[slot] = hp_ref[...]

    @pl.when(i >= n_prompt_tiles)
    def _():
        stage[slot] = hs_ref[...]

    def issue(r, carry):
        for k in range(TOP_K):
            pltpu.make_async_copy(stage.at[slot, pl.ds(r, 1), :],
                                  xs_hbm.at[pl.ds(dest_ref[0, 0, r * TOP_K + k], 1), :],
                                  sem.at[slot]).start()
        return carry

    lax.fori_loop(0, td, issue, 0, unroll=4)

    @pl.when(i == n_tiles - 1)
    def _():
        wait_slot(slot)
        if n_tiles > 1:
            wait_slot(1 - slot)


def _dispatch(last_unit, meta, dest, h_prompt, h_sample, p_rows):
    td = DISPATCH_TILE
    n_p, width = h_prompt.shape
    n_s = h_sample.shape[0]
    assert n_p % td == 0 and n_s % td == 0
    n_prompt_tiles, n_tiles = n_p // td, (n_p + n_s) // td
    grid_spec = pltpu.PrefetchScalarGridSpec(
        num_scalar_prefetch=2,
        grid=(n_tiles,),
        in_specs=[pl.BlockSpec((1, 1, td * TOP_K), lambda i, *_: (i, 0, 0), memory_space=pltpu.SMEM),
                  pl.BlockSpec((td, width), lambda i, *_: (jnp.minimum(i, n_prompt_tiles - 1), 0)),
                  pl.BlockSpec((td, width), lambda i, *_: (jnp.maximum(i - n_prompt_tiles, 0), 0))],
        out_specs=pl.BlockSpec(memory_space=pl.ANY),
        scratch_shapes=[pltpu.VMEM((2, td, width), F32), pltpu.VMEM((MOE_PAD, width), F32),
                        pltpu.SemaphoreType.DMA((2,)), pltpu.SemaphoreType.DMA],
    )
    return pl.pallas_call(
        functools.partial(_dispatch_kernel, n_prompt_tiles=n_prompt_tiles, n_tiles=n_tiles),
        grid_spec=grid_spec,
        out_shape=jax.ShapeDtypeStruct((p_rows, width), F32),
        compiler_params=_params(1),
        name="moe_dispatch",
    )(last_unit, meta, dest.reshape(n_tiles, 1, td * TOP_K), h_prompt, h_sample)


def _expert_kernel(item_e, item_start, item_units, meta,
                   xs_hbm, wg_ref, wl_ref, wd_ref, bg_ref, bl_ref, bd_ref,
                   y_hbm, stage, x_buf, y_buf, sem_in, sem_out):
    s = pl.program_id(0)
    j = pl.program_id(1)
    n_slots = pl.num_programs(0)
    units = item_units[s]
    start = item_start[s]
    pad = MOE_PAD
    assert MOE_NCH >= 2

    def unit_rows(t):
        return pl.ds(pl.multiple_of(t * pad, pad), pad)

    def hbm_rows(first_row, t):
        return pl.ds(pl.multiple_of(first_row + t * pad, pad), pad)

    def copy_in(first_row, t):
        return pltpu.make_async_copy(xs_hbm.at[hbm_rows(first_row, t), :], stage.at[unit_rows(t), :], sem_in)

    def copy_out(t):
        return pltpu.make_async_copy(y_buf.at[unit_rows(t), :], y_hbm.at[hbm_rows(start, t), :], sem_out)

    def for_range(lo, hi, fn):
        def body(t, carry):
            fn(t)
            return carry
        lax.fori_loop(lo, hi, body, 0)

    @pl.when(jnp.logical_and(s == 0, j == 0))
    def _():
        for_range(0, units, lambda t: copy_in(start, t).start())

    @pl.when(jnp.logical_and(j == MOE_NCH - 1, s + 1 < n_slots))
    def _():
        nxt = jnp.minimum(s + 1, n_slots - 1)
        for_range(0, item_units[nxt], lambda t: copy_in(item_start[nxt], t).start())

    @pl.when(j == 0)
    def _():
        for_range(0, units, lambda t: copy_in(start, t).wait())

    def ffn(row0, m, first):
        wg = wg_ref[...].astype(BF16)
        wl = wl_ref[...].astype(BF16)
        wd = wd_ref[...].astype(BF16)
        for off in range(0, m, MOE_TILE):
            rows = pl.ds(pl.multiple_of(row0 + off, pad), min(MOE_TILE, m - off))
            if first:
                xb = jnp.concatenate(_unpack_bf16_pairs(stage[rows, :]), axis=1)
                x_buf[rows, :] = xb
            else:
                xb = x_buf[rows, :]
            glu = jnp.minimum(_dot(xb, wg) + bg_ref[...], SWIGLU_LIMIT)
            lin = jnp.clip(_dot(xb, wl) + bl_ref[...], -SWIGLU_LIMIT, SWIGLU_LIMIT)
            act = glu * jax.nn.sigmoid(SWIGLU_ALPHA * glu) * (lin + 1.0)
            down = _dot(act.astype(BF16), wd)
            if first:
                y_buf[rows, :] = bd_ref[...] + down
            else:
                y_buf[rows, :] += down

        @pl.when(j == MOE_NCH - 1)
        def _():
            for u in range(m // pad):
                copy_out(row0 // pad + u).start()

    def run(first):
        big = MOE_BODY_ROWS
        n_big = units // (big // pad)
        for_range(0, n_big, lambda t: ffn(t * big, big, first))
        base = n_big * big
        rem = units * pad - base
        m = big // 2
        while m >= pad:
            take = (rem & m) != 0

            @pl.when(take)
            def _(base=base, m=m):
                ffn(base, m, first)

            base = base + jnp.where(take, m, 0)
            m //= 2

    @pl.when(jnp.logical_and(units > 0, j == 0))
    def _():
        run(True)

    @pl.when(jnp.logical_and(units > 0, j > 0))
    def _():
        run(False)

    @pl.when(jnp.logical_and(j == MOE_NCH - 1, units > 0))
    def _store():
        for_range(0, units, lambda t: copy_out(t).wait())

    @pl.when(jnp.logical_and(s == pl.num_programs(0) - 1, j == MOE_NCH - 1))
    def _fill_tail():
        y_buf[pl.ds(0, pad), :] = jnp.zeros((pad, D_MODEL), F32)

        def tail_copy(t):
            dst = y_hbm.at[pl.ds(pl.multiple_of(t * pad, pad), pad), :]
            return pltpu.make_async_copy(y_buf.at[pl.ds(0, pad), :], dst, sem_out)

        n_units = y_hbm.shape[0] // pad
        for_range(meta[1], n_units, lambda t: tail_copy(t).start())
        for_range(meta[1], n_units, lambda t: tail_copy(t).wait())


def _experts(item_e, item_start, item_units, meta, xs, w_gu, b_gu, w_dn, b_dn):
    n_slots = item_e.shape[0]
    p_rows = xs.shape[0]
    c = MOE_CHUNK

    def chunk(s, j, ie, ist, iu, m):
        return jnp.where(s < m[0], j, MOE_NCH - 1)

    grid_spec = pltpu.PrefetchScalarGridSpec(
        num_scalar_prefetch=4,
        grid=(n_slots, MOE_NCH),
        in_specs=[
            pl.BlockSpec(memory_space=pl.ANY),
            pl.BlockSpec((None, D_MODEL, c), lambda s, j, ie, *a: (ie[s], 0, chunk(s, j, ie, *a))),
            pl.BlockSpec((None, D_MODEL, c), lambda s, j, ie, *a: (ie[s], 0, MOE_NCH + chunk(s, j, ie, *a))),
            pl.BlockSpec((None, c, D_MODEL), lambda s, j, ie, *a: (ie[s], chunk(s, j, ie, *a), 0)),
            pl.BlockSpec((None, 1, c), lambda s, j, ie, *a: (ie[s], 0, chunk(s, j, ie, *a))),
            pl.BlockSpec((None, 1, c), lambda s, j, ie, *a: (ie[s], 0, MOE_NCH + chunk(s, j, ie, *a))),
            pl.BlockSpec((None, 1, D_MODEL), lambda s, j, ie, *a: (ie[s], 0, 0)),
        ],
        out_specs=pl.BlockSpec(memory_space=pl.ANY),
        scratch_shapes=[
            pltpu.VMEM((MOE_ITEM_ROWS, D_MODEL // 2), F32),
            pltpu.VMEM((MOE_ITEM_ROWS, D_MODEL), BF16),
            pltpu.VMEM((MOE_ITEM_ROWS, D_MODEL), F32),
            pltpu.SemaphoreType.DMA,
            pltpu.SemaphoreType.DMA,
        ],
    )
    return pl.pallas_call(
        _expert_kernel,
        grid_spec=grid_spec,
        out_shape=jax.ShapeDtypeStruct((p_rows, D_MODEL), F32),
        compiler_params=_params(2),
        name="moe_experts",
    )(item_e, item_start, item_units, meta, xs, w_gu, w_gu, w_dn,
      b_gu.reshape(N_EXPERTS, 1, 2 * D_EXPERT), b_gu.reshape(N_EXPERTS, 1, 2 * D_EXPERT),
      b_dn.reshape(N_EXPERTS, 1, D_MODEL))


def _combine_kernel(dest_ref, next_dest_ref, x_ref, gate_ref, gf_ref, y_hbm, o_ref, buf, sem, *,
                    final_norm):
    i = pl.program_id(0)
    tc = x_ref.shape[0]
    slot = i % 2

    def issue(ref, slot):
        for r in range(tc):
            for k in range(TOP_K):
                pltpu.make_async_copy(y_hbm.at[pl.ds(ref[0, 0, r * TOP_K + k], 1), :],
                                      buf.at[slot, pl.ds(k * tc + r, 1), :], sem.at[slot]).start()

    @pl.when(i == 0)
    def _():
        issue(dest_ref, 0)

    @pl.when(i + 1 < pl.num_programs(0))
    def _():
        issue(next_dest_ref, 1 - slot)

    pltpu.make_async_copy(y_hbm.at[pl.ds(0, TOP_K * tc), :], buf.at[slot], sem.at[slot]).wait()
    gate = gate_ref[...]
    acc = x_ref[...]
    for k in range(TOP_K):
        acc = acc + gate[:, k:k + 1] * buf[slot, k * tc:(k + 1) * tc, :]
    o_ref[...] = _rms(acc, gf_ref[...]) if final_norm else acc


def _combine(dest, x2, gate, g_final, y, final_norm):
    tc = min(COMBINE_TILE, x2.shape[0])
    n = x2.shape[0]
    n_tiles = n // tc
    dest = dest.reshape(n_tiles, 1, tc * TOP_K)
    return pl.pallas_call(
        functools.partial(_combine_kernel, final_norm=final_norm),
        grid=(n_tiles,),
        in_specs=[pl.BlockSpec((1, 1, tc * TOP_K), lambda i: (i, 0, 0), memory_space=pltpu.SMEM),
                  pl.BlockSpec((1, 1, tc * TOP_K), lambda i: (jnp.minimum(i + 1, n_tiles - 1), 0, 0),
                               memory_space=pltpu.SMEM),
                  pl.BlockSpec((tc, D_MODEL), lambda i: (i, 0)),
                  pl.BlockSpec((tc, LANES), lambda i: (i, 0)),
                  _const_spec((1, D_MODEL)),
                  pl.BlockSpec(memory_space=pl.ANY)],
        out_specs=pl.BlockSpec((tc, D_MODEL), lambda i: (i, 0)),
        out_shape=jax.ShapeDtypeStruct((n, D_MODEL), F32),
        scratch_shapes=[pltpu.VMEM((2, TOP_K * tc, D_MODEL), F32), pltpu.SemaphoreType.DMA((2,))],
        compiler_params=_params(1),
        name="moe_combine",
    )(dest, dest, x2, gate, g_final, y)


def _route(top_idx):
    n_tok = top_idx.shape[0]
    n_assign = n_tok * TOP_K
    pad, item_rows = MOE_PAD, MOE_ITEM_ROWS
    p_rows = -(-(n_assign + N_EXPERTS * (pad - 1)) // pad) * pad
    item_units = item_rows // pad
    n_slots = (p_rows // pad + N_EXPERTS * (item_units - 1)) // item_units
    experts = jnp.arange(N_EXPERTS, dtype=jnp.int32)

    flat_e = top_idx.reshape(-1)
    onehot = (top_idx[:, :, None] == experts[None, None, :]).astype(jnp.int32)
    chosen = jnp.sum(onehot, axis=1)
    running = jnp.cumsum(chosen, axis=0)
    rank = jnp.sum((running - chosen)[:, None, :] * onehot, axis=2).reshape(-1)
    counts = running[-1]
    padded = (counts + pad - 1) // pad * pad
    pad_end = jnp.cumsum(padded)
    pad_start = pad_end - padded
    dest = pad_start[flat_e] + rank
    last_unit = jnp.where(padded > 0, pad_end - pad, -1).astype(jnp.int32)

    items_per_e = (padded + item_rows - 1) // item_rows
    item_end = jnp.cumsum(items_per_e)
    n_items = item_end[-1]
    slot = jnp.arange(n_slots, dtype=jnp.int32)
    live = slot < n_items
    s_eff = jnp.minimum(slot, n_items - 1)
    e = jnp.minimum(jnp.searchsorted(item_end, s_eff, side="right"), N_EXPERTS - 1).astype(jnp.int32)
    local = s_eff - (item_end[e] - items_per_e[e])
    start = pad_start[e] + local * item_rows
    n_rows = jnp.clip(padded[e] - local * item_rows, 0, item_rows)
    units = jnp.where(live, n_rows // pad, 0).astype(jnp.int32)
    meta = jnp.stack([n_items, pad_end[-1] // pad]).astype(jnp.int32)
    return dest.astype(jnp.int32), last_unit, p_rows, e, start.astype(jnp.int32), units, meta


def _group_front(x, sb_fn, mk, mv, lw, tm_proj, tm_post, rows_per_batch, blk):
    u, va, q, k, vb = _proj(x, lw["g_mix"], lw["w_in"], tm_proj)
    b_out = sb_fn(q, k, vb)
    x2, h2, idx, gate = _post(x, u, va, b_out, mk, mv, lw, tm_post, rows_per_batch, blk)
    return x2, h2, idx, gate, k, vb, va


def kernel(x_prompt, x_sample, cache_sb_k, cache_sb_v, cache_mem_k, cache_mem_v, mem_prompt, g_mix, w_in, w_sp, b_sp, g_a, g_b, w_out, g_cross, g_mem, w_cq, w_ck, w_cv, w_co, g_moe, w_router, b_router, w_gu, b_gu, w_dn, b_dn, g_final):
    depth = g_mix.shape[0]
    bsz, seq, _ = x_prompt.shape
    dbsz, dseq, _ = x_sample.shape
    n_mem = mem_prompt.shape[1]
    n_p, n_s = bsz * seq, dbsz * dseq

    xp = x_prompt.reshape(n_p, D_MODEL)
    xs = x_sample.reshape(n_s, D_MODEL)
    mem = mem_prompt.reshape(bsz * n_mem, D_MODEL)
    outs = [[] for _ in range(7)]
    for l in range(depth):
        lw = {
            "g_mix": g_mix[l][None], "w_in": w_in[l].astype(BF16),
            "w_sp": w_sp[l], "b_sp": b_sp[l], "g_a": g_a[l][None], "g_b": g_b[l][None],
            "w_out": w_out[l].astype(BF16), "g_cross": g_cross[l][None],
            "w_cq": w_cq[l].astype(BF16), "w_co": w_co[l].astype(BF16), "g_moe": g_moe[l][None],
            "w_router": jnp.pad(w_router[l], ((0, 0), (0, LANES - N_EXPERTS))),
            "b_router": jnp.pad(b_router[l], (0, LANES - N_EXPERTS))[None],
        }
        mk, mv = _memkv(mem, g_mem[l][None], w_ck[l].astype(BF16), w_cv[l].astype(BF16))

        def sb_p(q, k, v):
            shp = (bsz, seq, SB_WIDTH)
            return _sb_prompt(q.reshape(shp), k.reshape(shp), v.reshape(shp)).reshape(n_p, SB_WIDTH)

        def sb_s(q, k, v):
            shp = (dbsz, dseq, SB_WIDTH)
            return _sb_sample(q.reshape(shp), k.reshape(shp), v.reshape(shp),
                              cache_sb_k, cache_sb_v, l).reshape(n_s, SB_WIDTH)

        xp2, hp, idx_p, gate_p, k_p, v_p, _ = _group_front(
            xp, sb_p, mk, mv, lw, 256, 256, seq, A_CHUNK)
        xs2, hs, idx_s, gate_s, k_s, v_s, va_s = _group_front(
            xs, sb_s, cache_mem_k[l].reshape(dbsz * n_mem, CA_WIDTH),
            cache_mem_v[l].reshape(dbsz * n_mem, CA_WIDTH), lw, n_s, n_s, dseq, min(dseq, A_CHUNK))

        top_idx = jnp.concatenate([idx_p[:, :TOP_K], idx_s[:, :TOP_K]], axis=0)
        dest, last_unit, p_rows, item_e, item_start, item_units, meta = _route(top_idx)
        x_sorted = _dispatch(last_unit, meta, dest, hp, hs, p_rows)
        y = _experts(item_e, item_start, item_units, meta, x_sorted, w_gu[l], b_gu[l], w_dn[l], b_dn[l])
        last = l == depth - 1
        dest = dest.reshape(n_p + n_s, TOP_K)
        xp = _combine(dest[:n_p], xp2, gate_p, g_final[None], y, last)
        xs = _combine(dest[n_p:], xs2, gate_s, g_final[None], y, last)

        outs[0].append(k_p.reshape(bsz, seq, SB_HEADS, SB_DH))
        outs[1].append(v_p.reshape(bsz, seq, SB_HEADS, SB_DH))
        outs[2].append(mk.reshape(bsz, n_mem, CA_HEADS, CA_DH))
        outs[3].append(mv.reshape(bsz, n_mem, CA_HEADS, CA_DH))
        outs[4].append(k_s.reshape(dbsz, dseq, SB_HEADS, SB_DH))
        outs[5].append(v_s.reshape(dbsz, dseq, SB_HEADS, SB_DH))
        outs[6].append(va_s.reshape(dbsz, dseq, A_GROUPS, A_DG))

    stacked = [jnp.stack(o, axis=0) for o in outs]
    return (xp.reshape(bsz, seq, D_MODEL), xs.reshape(dbsz, dseq, D_MODEL), *stacked)
```

```python
import functools

import jax
import jax.numpy as jnp
from jax import lax
from jax.experimental import pallas as pl
from jax.experimental.pallas import tpu as pltpu

F32 = jnp.float32
BF16 = jnp.bfloat16

D_MODEL = 2048
A_GROUPS = 8
A_DG = 128
A_WIDTH = A_GROUPS * A_DG
A_CHUNK = 128
SB_HEADS = 8
SB_DH = 128
SB_WIDTH = SB_HEADS * SB_DH
IN_WIDTH = 2 * A_WIDTH + 3 * SB_WIDTH
N_SEG = IN_WIDTH // 1024
CA_HEADS = 4
CA_DH = 128
CA_WIDTH = CA_HEADS * CA_DH
N_EXPERTS = 32
TOP_K = 4
D_EXPERT = D_MODEL
SWIGLU_LIMIT = 7.0
SWIGLU_ALPHA = 1.702
RMS_EPS = 1e-6

LANES = 128
SB_TILE = 256
SB_HEADS_PER_STEP = 8
VMEM_LIMIT = 56 * 1024 * 1024

MOE_PAD = 128
MOE_TILE = 256
MOE_BODY_ROWS = 512
MOE_ITEM_ROWS = 1280
MOE_CHUNK = 512
MOE_NCH = D_EXPERT // MOE_CHUNK
DISPATCH_TILE = 128
COMBINE_TILE = 128


def _rms(x, g):
    return x * lax.rsqrt(jnp.mean(x * x, axis=-1, keepdims=True) + RMS_EPS) * g


def _dot(a, b):
    return jnp.dot(a, b, preferred_element_type=F32)


def _dot_nt(a, b):
    return lax.dot_general(a, b, (((1,), (1,)), ((), ())), preferred_element_type=F32)


def _split_bf16(x):
    hi = x.astype(BF16)
    lo = (x - hi.astype(F32)).astype(BF16)
    return hi, lo


def _pack_bf16_pairs(x):
    c = x.shape[1] // 2
    r = x.astype(BF16).astype(F32)
    lo = lax.bitcast_convert_type(r[:, :c], jnp.uint32) >> 16
    hi = lax.bitcast_convert_type(r[:, c:], jnp.uint32)
    return lax.bitcast_convert_type(hi | lo, F32)


def _unpack_bf16_pairs(w):
    bits = lax.bitcast_convert_type(w, jnp.uint32)
    lo = lax.bitcast_convert_type(bits << 16, F32).astype(BF16)
    hi = lax.bitcast_convert_type(bits & jnp.uint32(0xFFFF0000), F32).astype(BF16)
    return lo, hi


def _params(n_axes):
    return pltpu.CompilerParams(dimension_semantics=("arbitrary",) * n_axes,
                                vmem_limit_bytes=VMEM_LIMIT)


def _const_spec(shape):
    return pl.BlockSpec(shape, lambda *_: (0,) * len(shape), pipeline_mode=pl.Buffered(1))


def _proj_kernel(x_ref, g_ref, w_ref, *out_refs):
    h = _rms(x_ref[...], g_ref[...]).astype(BF16)
    for s, o_ref in enumerate(out_refs):
        o_ref[...] = _dot(h, w_ref[:, s * 1024:(s + 1) * 1024])


def _proj(x, g, w, tm):
    n = x.shape[0]
    seg = jax.ShapeDtypeStruct((n, 1024), F32)
    return pl.pallas_call(
        _proj_kernel,
        grid=(n // tm,),
        in_specs=[pl.BlockSpec((tm, D_MODEL), lambda i: (i, 0)),
                  _const_spec((1, D_MODEL)),
                  _const_spec((D_MODEL, IN_WIDTH))],
        out_specs=[pl.BlockSpec((tm, 1024), lambda i: (i, 0))] * N_SEG,
        out_shape=[seg] * N_SEG,
        compiler_params=_params(1),
        name="proj",
    )(x, g, w)


def _log_sigmoid(x):
    neg_abs = lax.bitcast_convert_type(
        lax.bitcast_convert_type(x, jnp.uint32) | jnp.uint32(0x80000000), F32)
    return jnp.minimum(x, 0.0) - jnp.log(1.0 + jnp.exp(neg_abs))


def _sb_blocks(qbs, kbs, vbs, carries, later, mask):
    n = range(len(qbs))
    nzs = [_dot_nt(qbs[i], kbs[i]) * -(SB_DH ** -0.5) for i in n]
    ls = [_log_sigmoid(nz) for nz in nzs]
    if mask is not None:
        ls = [jnp.where(mask, l, 0.0) for l in ls]
    css = [_dot(jnp.concatenate(_split_bf16(l), axis=1), later) for l in ls]
    aas = [jnp.exp(ls[i] - nzs[i] + css[i] + carries[i]) for i in n]
    if mask is not None:
        aas = [jnp.where(mask, a, 0.0) for a in aas]
    contribs = [_dot(aas[i].astype(BF16), vbs[i]) for i in n]
    new = [carries[i] + css[i][:, 0:1] + ls[i][:, 0:1] for i in n]
    return contribs, new


def _later_matrix(n):
    r = lax.broadcasted_iota(jnp.int32, (n, n), 0)
    c = lax.broadcasted_iota(jnp.int32, (n, n), 1)
    m = (r > c).astype(BF16)
    return jnp.concatenate([m, m], axis=0)


def _sb_prompt_kernel(q_ref, k_ref, v_ref, o_ref, later_ref, carry_ref):
    qi = pl.program_id(2)
    t = SB_TILE
    later_ref[...] = _later_matrix(t)
    r = lax.broadcasted_iota(jnp.int32, (t, t), 0)
    c = lax.broadcasted_iota(jnp.int32, (t, t), 1)
    heads = [slice(h * SB_DH, (h + 1) * SB_DH) for h in range(SB_HEADS_PER_STEP)]

    def sweep(j, first):
        off = pl.multiple_of(j * t, t)
        rows = pl.ds(off, t)
        carries = [jnp.zeros((t, 1), F32) if first else carry_ref[h] for h in range(len(heads))]
        contribs, carries = _sb_blocks(
            [q_ref[:, sl].astype(BF16) for sl in heads],
            [k_ref[rows, sl].astype(BF16) for sl in heads],
            [v_ref[rows, sl].astype(BF16) for sl in heads],
            carries, later_ref[...], (c < r) if first else None)
        for h, sl in enumerate(heads):
            carry_ref[h] = carries[h]
            if first:
                o_ref[:, sl] = contribs[h]
            else:
                o_ref[:, sl] += contribs[h]

    sweep(qi, True)

    def body(step, carry):
        sweep(qi - 1 - step, False)
        return carry

    lax.fori_loop(0, qi, body, 0)


def _sb_prompt(q, k, v):
    bsz, t_len, _ = q.shape
    t = SB_TILE
    w = SB_HEADS_PER_STEP * SB_DH
    kv_spec = pl.BlockSpec((None, t_len, w), lambda b, h, i: (b, 0, h))
    return pl.pallas_call(
        _sb_prompt_kernel,
        grid=(bsz, SB_HEADS // SB_HEADS_PER_STEP, t_len // t),
        in_specs=[pl.BlockSpec((None, t, w), lambda b, h, i: (b, i, h)), kv_spec, kv_spec],
        out_specs=pl.BlockSpec((None, t, w), lambda b, h, i: (b, i, h)),
        out_shape=jax.ShapeDtypeStruct(q.shape, F32),
        scratch_shapes=[pltpu.VMEM((2 * t, t), BF16), pltpu.VMEM((SB_HEADS_PER_STEP, t, 1), F32)],
        compiler_params=_params(3),
        name="sb_prompt",
    )(q, k, v)


def _sb_sample_kernel(q_ref, kn_ref, vn_ref, ck_ref, cv_ref, o_ref, later_ref):
    t = SB_TILE
    n_new = q_ref.shape[0]
    later_ref[...] = _later_matrix(t)
    r = lax.broadcasted_iota(jnp.int32, (n_new, t), 0)
    c = lax.broadcasted_iota(jnp.int32, (n_new, t), 1)
    heads = range(SB_HEADS)
    sls = [slice(h * SB_DH, (h + 1) * SB_DH) for h in heads]
    qbs = [q_ref[:, sl].astype(BF16) for sl in sls]
    accs, carries = _sb_blocks(qbs, [kn_ref[:, sl].astype(BF16) for sl in sls],
                               [vn_ref[:, sl].astype(BF16) for sl in sls],
                               [jnp.zeros((n_new, 1), F32)] * SB_HEADS, later_ref[...], c < r)
    for j in reversed(range(ck_ref.shape[0] // (t * SB_HEADS))):
        rows = [pl.ds(j * t * SB_HEADS + h, t, stride=SB_HEADS) for h in heads]
        contribs, carries = _sb_blocks(qbs, [ck_ref[rows[h], :].astype(BF16) for h in heads],
                                       [cv_ref[rows[h], :].astype(BF16) for h in heads],
                                       carries, later_ref[...], None)
        accs = [a + cb for a, cb in zip(accs, contribs)]
    for sl, acc in zip(sls, accs):
        o_ref[:, sl] = acc


def _sb_sample(q, k_new, v_new, cache_k, cache_v, layer):
    bsz, n_new, _ = q.shape
    past = cache_k.shape[2]
    t = SB_TILE
    assert n_new <= t and past % t == 0
    pad = ((0, 0), (0, t - n_new), (0, 0))
    kn = jnp.pad(k_new, pad)
    vn = jnp.pad(v_new, pad)
    rows = lambda n: pl.BlockSpec((None, n, SB_WIDTH), lambda b: (b, 0, 0))
    depth = cache_k.shape[0]
    flat = (depth, bsz, past * SB_HEADS, SB_DH)
    cache = pl.BlockSpec((None, None, past * SB_HEADS, SB_DH), lambda b: (layer, b, 0, 0))
    return pl.pallas_call(
        _sb_sample_kernel,
        grid=(bsz,),
        in_specs=[rows(n_new), rows(t), rows(t), cache, cache],
        out_specs=rows(n_new),
        out_shape=jax.ShapeDtypeStruct(q.shape, F32),
        scratch_shapes=[pltpu.VMEM((2 * t, t), BF16)],
        compiler_params=_params(1),
        name="sb_sample",
    )(q, kn, vn, cache_k.reshape(flat), cache_v.reshape(flat))


def _memkv_kernel(m_ref, g_ref, wk_ref, wv_ref, k_ref, v_ref):
    h = _rms(m_ref[...], g_ref[...]).astype(BF16)
    k_ref[...] = _dot(h, wk_ref[...])
    v_ref[...] = _dot(h, wv_ref[...])


def _memkv(mem, g, wk, wv):
    n = mem.shape[0]
    tm = 256
    out = jax.ShapeDtypeStruct((n, CA_WIDTH), F32)
    return pl.pallas_call(
        _memkv_kernel,
        grid=(n // tm,),
        in_specs=[pl.BlockSpec((tm, D_MODEL), lambda i: (i, 0)),
                  _const_spec((1, D_MODEL)),
                  _const_spec((D_MODEL, CA_WIDTH)),
                  _const_spec((D_MODEL, CA_WIDTH))],
        out_specs=[pl.BlockSpec((tm, CA_WIDTH), lambda i: (i, 0))] * 2,
        out_shape=[out, out],
        compiler_params=_params(1),
        name="memkv",
    )(mem, g, wk, wv)


def _post_kernel(x_ref, u_ref, va_ref, b_ref, wsp_ref, bsp_ref, ga_ref, gb_ref, wout_ref,
                 gc_ref, wcq_ref, mk_ref, mv_ref, wco_ref, gm_ref, wr_ref, br_ref,
                 x2_ref, h2_ref, idx_ref, gate_ref, *, n_batches):
    tm = x_ref.shape[0]
    blk = wsp_ref.shape[1]

    r = lax.broadcasted_iota(jnp.int32, (blk, blk), 0)
    c = lax.broadcasted_iota(jnp.int32, (blk, blk), 1)
    tri = (c <= r).astype(F32)
    cols = []
    for g in range(A_GROUPS):
        w = (wsp_ref[g] * tri).astype(BF16)
        bias = bsp_ref[:, g:g + 1]
        sl = slice(g * A_DG, (g + 1) * A_DG)
        chunks = [slice(ch * blk, (ch + 1) * blk) for ch in range(tm // blk)]
        rows = []
        for pair in [chunks[p:p + 2] for p in range(0, len(chunks), 2)]:
            v = [va_ref[rs, sl].astype(BF16) for rs in pair]
            mixed = _dot(w, v[0] if len(v) == 1 else jnp.concatenate(v, axis=1))
            for p, rs in enumerate(pair):
                rows.append(u_ref[rs, sl] * (mixed[:, p * A_DG:(p + 1) * A_DG] + bias))
        cols.append(rows[0] if len(rows) == 1 else jnp.concatenate(rows, axis=0))
    a_out = jnp.concatenate(cols, axis=1)

    a_n = _rms(a_out, ga_ref[...]).astype(BF16)
    b_n = _rms(b_ref[...], gb_ref[...]).astype(BF16)
    x1 = x_ref[...] + _dot(a_n, wout_ref[0:A_WIDTH, :]) + _dot(b_n, wout_ref[A_WIDTH:, :])

    hc = _rms(x1, gc_ref[...]).astype(BF16)
    q = _dot(hc, wcq_ref[...])
    rows_b = tm // n_batches
    mem_b = mk_ref.shape[0] // n_batches
    per_batch = []
    for b in range(n_batches):
        rs = slice(b * rows_b, (b + 1) * rows_b)
        ms = slice(b * mem_b, (b + 1) * mem_b)
        heads = []
        for h in range(CA_HEADS):
            sl = slice(h * CA_DH, (h + 1) * CA_DH)
            s = _dot_nt(q[rs, sl].astype(BF16), mk_ref[ms, sl].astype(BF16)) * (CA_DH ** -0.5)
            p = jnp.exp(s - jnp.max(s, axis=-1, keepdims=True))
            p = p / jnp.sum(p, axis=-1, keepdims=True)
            heads.append(_dot(p.astype(BF16), mv_ref[ms, sl].astype(BF16)))
        per_batch.append(jnp.concatenate(heads, axis=1))
    o = (per_batch[0] if n_batches == 1 else jnp.concatenate(per_batch, axis=0)).astype(BF16)
    x2 = x1 + _dot(o, wco_ref[...])
    x2_ref[...] = x2

    h2 = _rms(x2, gm_ref[...])
    h2_ref[...] = _pack_bf16_pairs(h2)
    hh, hl = _split_bf16(h2)
    wh, wl = _split_bf16(wr_ref[...])
    both = _dot(hh, jnp.concatenate([wh, wl], axis=1))
    logits = both[:, :LANES] + both[:, LANES:] + _dot(hl, wh) + br_ref[...]
    lane = lax.broadcasted_iota(jnp.int32, (tm, LANES), 1).astype(F32)
    cur = jnp.where(lane < N_EXPERTS, logits, -jnp.inf)
    vals, idxs = [], []
    for _ in range(TOP_K):
        m = jnp.max(cur, axis=-1, keepdims=True)
        i = jnp.min(jnp.where(cur == m, lane, float(LANES)), axis=-1, keepdims=True)
        vals.append(m)
        idxs.append(i)
        cur = jnp.where(lane == i, -jnp.inf, cur)
    es = [jnp.exp(v - vals[0]) for v in vals]
    denom = es[0] + es[1] + es[2] + es[3]
    idx_out = jnp.zeros((tm, LANES), F32)
    gate_out = jnp.zeros((tm, LANES), F32)
    for k in range(TOP_K):
        idx_out = jnp.where(lane == k, idxs[k], idx_out)
        gate_out = jnp.where(lane == k, es[k] / denom, gate_out)
    idx_ref[...] = idx_out.astype(jnp.int32)
    gate_ref[...] = gate_out


def _post(x, u, va, b_out, mk, mv, lw, tm, rows_per_batch, blk):
    n = x.shape[0]
    n_mem = mk.shape[0] // (n // rows_per_batch)
    row = lambda w: pl.BlockSpec((tm, w), lambda i: (i, 0))
    if tm >= rows_per_batch:
        n_batches = tm // rows_per_batch
        mem = pl.BlockSpec((n_batches * n_mem, CA_WIDTH), lambda i: (i, 0))
    else:
        n_batches = 1
        tiles_per_batch = rows_per_batch // tm
        mem = pl.BlockSpec((n_mem, CA_WIDTH), lambda i: (i // tiles_per_batch, 0))
    wsp = lw["w_sp"][:, :blk, :blk]
    bsp = lw["b_sp"][:, :blk].T
    return pl.pallas_call(
        functools.partial(_post_kernel, n_batches=n_batches),
        grid=(n // tm,),
        in_specs=[row(D_MODEL), row(A_WIDTH), row(A_WIDTH), row(SB_WIDTH),
                  _const_spec((A_GROUPS, blk, blk)), _const_spec((blk, A_GROUPS)),
                  _const_spec((1, A_WIDTH)), _const_spec((1, SB_WIDTH)),
                  _const_spec((2 * A_WIDTH, D_MODEL)),
                  _const_spec((1, D_MODEL)), _const_spec((D_MODEL, CA_WIDTH)),
                  mem, mem, _const_spec((CA_WIDTH, D_MODEL)),
                  _const_spec((1, D_MODEL)), _const_spec((D_MODEL, LANES)), _const_spec((1, LANES))],
        out_specs=[row(D_MODEL), row(D_MODEL // 2), row(LANES), row(LANES)],
        out_shape=[jax.ShapeDtypeStruct((n, D_MODEL), F32), jax.ShapeDtypeStruct((n, D_MODEL // 2), F32),
                   jax.ShapeDtypeStruct((n, LANES), jnp.int32), jax.ShapeDtypeStruct((n, LANES), F32)],
        compiler_params=_params(1),
        name="post",
    )(x, u, va, b_out, wsp, bsp, lw["g_a"], lw["g_b"], lw["w_out"], lw["g_cross"], lw["w_cq"],
      mk, mv, lw["w_co"], lw["g_moe"], lw["w_router"], lw["b_router"])


def _dispatch_kernel(last_unit, meta, dest_ref, hp_ref, hs_ref, xs_hbm, stage, zero, sem, sem_zero, *,
                     n_prompt_tiles, n_tiles):
    i = pl.program_id(0)
    td = stage.shape[1]
    pad = MOE_PAD
    slot = i % 2

    def zero_copy(row0):
        return pltpu.make_async_copy(zero, xs_hbm.at[pl.ds(pl.multiple_of(row0, pad), pad), :], sem_zero)

    @pl.when(i == 0)
    def _():
        zero[...] = jnp.zeros(zero.shape, F32)

        def fill(wait):
            def act(copy):
                if wait:
                    copy.wait()
                else:
                    copy.start()

            def expert_body(e, carry):
                @pl.when(last_unit[e] >= 0)
                def _():
                    act(zero_copy(last_unit[e]))
                return carry

            def tail_body(t, carry):
                act(zero_copy(t * pad))
                return carry

            lax.fori_loop(0, N_EXPERTS, expert_body, 0)
            lax.fori_loop(meta[1], xs_hbm.shape[0] // pad, tail_body, 0)

        fill(False)
        fill(True)

    def wait_slot(slot):
        for _ in range(TOP_K):
            pltpu.make_async_copy(stage.at[slot], xs_hbm.at[pl.ds(0, td), :], sem.at[slot]).wait()

    @pl.when(i >= 2)
    def _():
        wait_slot(slot)

    @pl.when(i < n_prompt_tiles)
    def _():
        stage[slot] = hp_ref[...]

    @pl.when(i >= n_prompt_tiles)
    def _():
        stage[slot] = hs_ref[...]

    def issue(r, carry):
        for k in range(TOP_K):
            pltpu.make_async_copy(stage.at[slot, pl.ds(r, 1), :],
                                  xs_hbm.at[pl.ds(dest_ref[0, 0, r * TOP_K + k], 1), :],
                                  sem.at[slot]).start(priority=k % 2)
        return carry

    lax.fori_loop(0, td, issue, 0, unroll=4)

    @pl.when(i == n_tiles - 1)
    def _():
        wait_slot(slot)
        if n_tiles > 1:
            wait_slot(1 - slot)


def _dispatch(last_unit, meta, dest, h_prompt, h_sample, p_rows):
    td = DISPATCH_TILE
    n_p, width = h_prompt.shape
    n_s = h_sample.shape[0]
    assert n_p % td == 0 and n_s % td == 0
    n_prompt_tiles, n_tiles = n_p // td, (n_p + n_s) // td
    grid_spec = pltpu.PrefetchScalarGridSpec(
        num_scalar_prefetch=2,
        grid=(n_tiles,),
        in_specs=[pl.BlockSpec((1, 1, td * TOP_K), lambda i, *_: (i, 0, 0), memory_space=pltpu.SMEM),
                  pl.BlockSpec((td, width), lambda i, *_: (jnp.minimum(i, n_prompt_tiles - 1), 0)),
                  pl.BlockSpec((td, width), lambda i, *_: (jnp.maximum(i - n_prompt_tiles, 0), 0))],
        out_specs=pl.BlockSpec(memory_space=pl.ANY),
        scratch_shapes=[pltpu.VMEM((2, td, width), F32), pltpu.VMEM((MOE_PAD, width), F32),
                        pltpu.SemaphoreType.DMA((2,)), pltpu.SemaphoreType.DMA],
    )
    return pl.pallas_call(
        functools.partial(_dispatch_kernel, n_prompt_tiles=n_prompt_tiles, n_tiles=n_tiles),
        grid_spec=grid_spec,
        out_shape=jax.ShapeDtypeStruct((p_rows, width), F32),
        compiler_params=_params(1),
        name="moe_dispatch",
    )(last_unit, meta, dest.reshape(n_tiles, 1, td * TOP_K), h_prompt, h_sample)


def _expert_kernel(item_e, item_start, item_units, meta,
                   xs_hbm, wg_ref, wl_ref, wd_ref, bg_ref, bl_ref, bd_ref,
                   y_hbm, stage, x_buf, y_buf, sem_in, sem_out):
    s = pl.program_id(0)
    j = pl.program_id(1)
    n_slots = pl.num_programs(0)
    units = item_units[s]
    start = item_start[s]
    pad = MOE_PAD
    assert MOE_NCH >= 2

    def unit_rows(t):
        return pl.ds(pl.multiple_of(t * pad, pad), pad)

    def hbm_rows(first_row, t):
        return pl.ds(pl.multiple_of(first_row + t * pad, pad), pad)

    def copy_in(first_row, t):
        return pltpu.make_async_copy(xs_hbm.at[hbm_rows(first_row, t), :], stage.at[unit_rows(t), :], sem_in)

    def copy_out(t):
        return pltpu.make_async_copy(y_buf.at[unit_rows(t), :], y_hbm.at[hbm_rows(start, t), :], sem_out)

    def for_range(lo, hi, fn):
        def body(t, carry):
            fn(t)
            return carry
        lax.fori_loop(lo, hi, body, 0)

    @pl.when(jnp.logical_and(s == 0, j == 0))
    def _():
        for_range(0, units, lambda t: copy_in(start, t).start())

    @pl.when(jnp.logical_and(j == MOE_NCH - 1, s + 1 < n_slots))
    def _():
        nxt = jnp.minimum(s + 1, n_slots - 1)
        for_range(0, item_units[nxt], lambda t: copy_in(item_start[nxt], t).start())

    @pl.when(j == 0)
    def _():
        for_range(0, units, lambda t: copy_in(start, t).wait())

    def ffn(row0, m, first):
        wg = wg_ref[...].astype(BF16)
        wl = wl_ref[...].astype(BF16)
        wd = wd_ref[...].astype(BF16)
        for off in range(0, m, MOE_TILE):
            rows = pl.ds(pl.multiple_of(row0 + off, pad), min(MOE_TILE, m - off))
            if first:
                xb = jnp.concatenate(_unpack_bf16_pairs(stage[rows, :]), axis=1)
                x_buf[rows, :] = xb
            else:
                xb = x_buf[rows, :]
            glu = jnp.minimum(_dot(xb, wg) + bg_ref[...], SWIGLU_LIMIT)
            lin = jnp.clip(_dot(xb, wl) + bl_ref[...], -SWIGLU_LIMIT, SWIGLU_LIMIT)
            act = glu * jax.nn.sigmoid(SWIGLU_ALPHA * glu) * (lin + 1.0)
            down = _dot(act.astype(BF16), wd)
            if first:
                y_buf[rows, :] = bd_ref[...] + down
            else:
                y_buf[rows, :] += down

        @pl.when(j == MOE_NCH - 1)
        def _():
            for u in range(m // pad):
                copy_out(row0 // pad + u).start()

    def run(first):
        big = MOE_BODY_ROWS
        n_big = units // (big // pad)
        for_range(0, n_big, lambda t: ffn(t * big, big, first))
        base = n_big * big
        rem = units * pad - base
        m = big // 2
        while m >= pad:
            take = (rem & m) != 0

            @pl.when(take)
            def _(base=base, m=m):
                ffn(base, m, first)

            base = base + jnp.where(take, m, 0)
            m //= 2

    @pl.when(jnp.logical_and(units > 0, j == 0))
    def _():
        run(True)

    @pl.when(jnp.logical_and(units > 0, j > 0))
    def _():
        run(False)

    @pl.when(jnp.logical_and(j == MOE_NCH - 1, units > 0))
    def _store():
        for_range(0, units, lambda t: copy_out(t).wait())

    @pl.when(jnp.logical_and(s == pl.num_programs(0) - 1, j == MOE_NCH - 1))
    def _fill_tail():
        y_buf[pl.ds(0, pad), :] = jnp.zeros((pad, D_MODEL), F32)

        def tail_copy(t):
            dst = y_hbm.at[pl.ds(pl.multiple_of(t * pad, pad), pad), :]
            return pltpu.make_async_copy(y_buf.at[pl.ds(0, pad), :], dst, sem_out)

        n_units = y_hbm.shape[0] // pad
        for_range(meta[1], n_units, lambda t: tail_copy(t).start())
        for_range(meta[1], n_units, lambda t: tail_copy(t).wait())


def _experts(item_e, item_start, item_units, meta, xs, w_gu, b_gu, w_dn, b_dn):
    n_slots = item_e.shape[0]
    p_rows = xs.shape[0]
    c = MOE_CHUNK

    def chunk(s, j, ie, ist, iu, m):
        return jnp.where(s < m[0], j, MOE_NCH - 1)

    grid_spec = pltpu.PrefetchScalarGridSpec(
        num_scalar_prefetch=4,
        grid=(n_slots, MOE_NCH),
        in_specs=[
            pl.BlockSpec(memory_space=pl.ANY),
            pl.BlockSpec((None, D_MODEL, c), lambda s, j, ie, *a: (ie[s], 0, chunk(s, j, ie, *a))),
            pl.BlockSpec((None, D_MODEL, c), lambda s, j, ie, *a: (ie[s], 0, MOE_NCH + chunk(s, j, ie, *a))),
            pl.BlockSpec((None, c, D_MODEL), lambda s, j, ie, *a: (ie[s], chunk(s, j, ie, *a), 0)),
            pl.BlockSpec((None, 1, c), lambda s, j, ie, *a: (ie[s], 0, chunk(s, j, ie, *a))),
            pl.BlockSpec((None, 1, c), lambda s, j, ie, *a: (ie[s], 0, MOE_NCH + chunk(s, j, ie, *a))),
            pl.BlockSpec((None, 1, D_MODEL), lambda s, j, ie, *a: (ie[s], 0, 0)),
        ],
        out_specs=pl.BlockSpec(memory_space=pl.ANY),
        scratch_shapes=[
            pltpu.VMEM((MOE_ITEM_ROWS, D_MODEL // 2), F32),
            pltpu.VMEM((MOE_ITEM_ROWS, D_MODEL), BF16),
            pltpu.VMEM((MOE_ITEM_ROWS, D_MODEL), F32),
            pltpu.SemaphoreType.DMA,
            pltpu.SemaphoreType.DMA,
        ],
    )
    return pl.pallas_call(
        _expert_kernel,
        grid_spec=grid_spec,
        out_shape=jax.ShapeDtypeStruct((p_rows, D_MODEL), F32),
        compiler_params=_params(2),
        name="moe_experts",
    )(item_e, item_start, item_units, meta, xs, w_gu, w_gu, w_dn,
      b_gu.reshape(N_EXPERTS, 1, 2 * D_EXPERT), b_gu.reshape(N_EXPERTS, 1, 2 * D_EXPERT),
      b_dn.reshape(N_EXPERTS, 1, D_MODEL))


def _combine_kernel(dest_ref, next_dest_ref, x_ref, gate_ref, gf_ref, y_hbm, o_ref, buf, sem, *,
                    final_norm):
    i = pl.program_id(0)
    tc = x_ref.shape[0]
    slot = i % 2

    def issue(ref, slot):
        for r in range(tc):
            for k in range(TOP_K):
                pltpu.make_async_copy(y_hbm.at[pl.ds(ref[0, 0, r * TOP_K + k], 1), :],
                                      buf.at[slot, pl.ds(k * tc + r, 1), :],
                                      sem.at[slot]).start(priority=k % 2)

    @pl.when(i == 0)
    def _():
        issue(dest_ref, 0)

    @pl.when(i + 1 < pl.num_programs(0))
    def _():
        issue(next_dest_ref, 1 - slot)

    pltpu.make_async_copy(y_hbm.at[pl.ds(0, TOP_K * tc), :], buf.at[slot], sem.at[slot]).wait()
    gate = gate_ref[...]
    acc = x_ref[...]
    for k in range(TOP_K):
        acc = acc + gate[:, k:k + 1] * buf[slot, k * tc:(k + 1) * tc, :]
    o_ref[...] = _rms(acc, gf_ref[...]) if final_norm else acc


def _combine(dest, x2, gate, g_final, y, final_norm):
    tc = COMBINE_TILE
    n = x2.shape[0]
    n_tiles = n // tc
    dest = dest.reshape(n_tiles, 1, tc * TOP_K)
    return pl.pallas_call(
        functools.partial(_combine_kernel, final_norm=final_norm),
        grid=(n_tiles,),
        in_specs=[pl.BlockSpec((1, 1, tc * TOP_K), lambda i: (i, 0, 0), memory_space=pltpu.SMEM),
                  pl.BlockSpec((1, 1, tc * TOP_K), lambda i: (jnp.minimum(i + 1, n_tiles - 1), 0, 0),
                               memory_space=pltpu.SMEM),
                  pl.BlockSpec((tc, D_MODEL), lambda i: (i, 0)),
                  pl.BlockSpec((tc, LANES), lambda i: (i, 0)),
                  _const_spec((1, D_MODEL)),
                  pl.BlockSpec(memory_space=pl.ANY)],
        out_specs=pl.BlockSpec((tc, D_MODEL), lambda i: (i, 0)),
        out_shape=jax.ShapeDtypeStruct((n, D_MODEL), F32),
        scratch_shapes=[pltpu.VMEM((2, TOP_K * tc, D_MODEL), F32), pltpu.SemaphoreType.DMA((2,))],
        compiler_params=_params(1),
        name="moe_combine",
    )(dest, dest, x2, gate, g_final, y)


def _route(top_idx):
    n_tok = top_idx.shape[0]
    n_assign = n_tok * TOP_K
    pad, item_rows = MOE_PAD, MOE_ITEM_ROWS
    p_rows = -(-(n_assign + N_EXPERTS * (pad - 1)) // pad) * pad
    n_slots = N_EXPERTS + -(-p_rows // item_rows)
    experts = jnp.arange(N_EXPERTS, dtype=jnp.int32)

    flat_e = top_idx.reshape(-1)
    onehot = (top_idx[:, :, None] == experts[None, None, :]).astype(jnp.int32)
    chosen = jnp.sum(onehot, axis=1)
    running = jnp.cumsum(chosen, axis=0)
    rank = jnp.sum((running - chosen)[:, None, :] * onehot, axis=2).reshape(-1)
    counts = running[-1]
    padded = (counts + pad - 1) // pad * pad
    pad_end = jnp.cumsum(padded)
    pad_start = pad_end - padded
    dest = pad_start[flat_e] + rank
    last_unit = jnp.where(padded > 0, pad_end - pad, -1).astype(jnp.int32)

    items_per_e = (padded + item_rows - 1) // item_rows
    item_end = jnp.cumsum(items_per_e)
    n_items = item_end[-1]
    slot = jnp.arange(n_slots, dtype=jnp.int32)
    live = slot < n_items
    s_eff = jnp.minimum(slot, n_items - 1)
    e = jnp.minimum(jnp.searchsorted(item_end, s_eff, side="right"), N_EXPERTS - 1).astype(jnp.int32)
    local = s_eff - (item_end[e] - items_per_e[e])
    start = pad_start[e] + local * item_rows
    n_rows = jnp.clip(padded[e] - local * item_rows, 0, item_rows)
    units = jnp.where(live, n_rows // pad, 0).astype(jnp.int32)
    meta = jnp.stack([n_items, pad_end[-1] // pad]).astype(jnp.int32)
    return dest.astype(jnp.int32), last_unit, p_rows, e, start.astype(jnp.int32), units, meta


def _group_front(x, sb_fn, mk, mv, lw, tm_proj, tm_post, rows_per_batch, blk):
    u, va, q, k, vb = _proj(x, lw["g_mix"], lw["w_in"], tm_proj)
    b_out = sb_fn(q, k, vb)
    x2, h2, idx, gate = _post(x, u, va, b_out, mk, mv, lw, tm_post, rows_per_batch, blk)
    return x2, h2, idx, gate, k, vb, va


def kernel(x_prompt, x_sample, cache_sb_k, cache_sb_v, cache_mem_k, cache_mem_v, mem_prompt, g_mix, w_in, w_sp, b_sp, g_a, g_b, w_out, g_cross, g_mem, w_cq, w_ck, w_cv, w_co, g_moe, w_router, b_router, w_gu, b_gu, w_dn, b_dn, g_final):
    depth = g_mix.shape[0]
    bsz, seq, _ = x_prompt.shape
    dbsz, dseq, _ = x_sample.shape
    n_mem = mem_prompt.shape[1]
    n_p, n_s = bsz * seq, dbsz * dseq

    xp = x_prompt.reshape(n_p, D_MODEL)
    xs = x_sample.reshape(n_s, D_MODEL)
    mem = mem_prompt.reshape(bsz * n_mem, D_MODEL)
    outs = [[] for _ in range(7)]
    for l in range(depth):
        lw = {
            "g_mix": g_mix[l][None], "w_in": w_in[l].astype(BF16),
            "w_sp": w_sp[l], "b_sp": b_sp[l], "g_a": g_a[l][None], "g_b": g_b[l][None],
            "w_out": w_out[l].astype(BF16), "g_cross": g_cross[l][None],
            "w_cq": w_cq[l].astype(BF16), "w_co": w_co[l].astype(BF16), "g_moe": g_moe[l][None],
            "w_router": jnp.pad(w_router[l], ((0, 0), (0, LANES - N_EXPERTS))),
            "b_router": jnp.pad(b_router[l], (0, LANES - N_EXPERTS))[None],
        }
        mk, mv = _memkv(mem, g_mem[l][None], w_ck[l].astype(BF16), w_cv[l].astype(BF16))

        def sb_p(q, k, v):
            shp = (bsz, seq, SB_WIDTH)
            return _sb_prompt(q.reshape(shp), k.reshape(shp), v.reshape(shp)).reshape(n_p, SB_WIDTH)

        def sb_s(q, k, v):
            shp = (dbsz, dseq, SB_WIDTH)
            return _sb_sample(q.reshape(shp), k.reshape(shp), v.reshape(shp),
                              cache_sb_k, cache_sb_v, l).reshape(n_s, SB_WIDTH)

        xp2, hp, idx_p, gate_p, k_p, v_p, _ = _group_front(
            xp, sb_p, mk, mv, lw, 256, 256, seq, A_CHUNK)
        xs2, hs, idx_s, gate_s, k_s, v_s, va_s = _group_front(
            xs, sb_s, cache_mem_k[l].reshape(dbsz * n_mem, CA_WIDTH),
            cache_mem_v[l].reshape(dbsz * n_mem, CA_WIDTH), lw, n_s, n_s, dseq, min(dseq, A_CHUNK))

        top_idx = jnp.concatenate([idx_p[:, :TOP_K], idx_s[:, :TOP_K]], axis=0)
        dest, last_unit, p_rows, item_e, item_start, item_units, meta = _route(top_idx)
        x_sorted = _dispatch(last_unit, meta, dest, hp, hs, p_rows)
        y = _experts(item_e, item_start, item_units, meta, x_sorted, w_gu[l], b_gu[l], w_dn[l], b_dn[l])
        last = l == depth - 1
        dest = dest.reshape(n_p + n_s, TOP_K)
        xp = _combine(dest[:n_p], xp2, gate_p, g_final[None], y, last)
        xs = _combine(dest[n_p:], xs2, gate_s, g_final[None], y, last)

        outs[0].append(k_p.reshape(bsz, seq, SB_HEADS, SB_DH))
        outs[1].append(v_p.reshape(bsz, seq, SB_HEADS, SB_DH))
        outs[2].append(mk.reshape(bsz, n_mem, CA_HEADS, CA_DH))
        outs[3].append(mv.reshape(bsz, n_mem, CA_HEADS, CA_DH))
        outs[4].append(k_s.reshape(dbsz, dseq, SB_HEADS, SB_DH))
        outs[5].append(v_s.reshape(dbsz, dseq, SB_HEADS, SB_DH))
        outs[6].append(va_s.reshape(dbsz, dseq, A_GROUPS, A_DG))

    stacked = [jnp.stack(o, axis=0) for o in outs]
    return (xp.reshape(bsz, seq, D_MODEL), xs.reshape(dbsz, dseq, D_MODEL), *stacked)
```
